```python
import math
import jax, jax.numpy as jnp
from jax import lax
import numpy as np

D_MODEL = 2048
BATCH = 4
SEQ = 4096
DEPTH = 1

D_MIX = D_MODEL
D_ATTN = D_MIX // 2
D_RWKV = D_MIX - D_ATTN
ATTN_HEAD_DIM = 64
ATTN_HEADS = D_ATTN // (2 * ATTN_HEAD_DIM)
ATTN_V_DIM = 2 * ATTN_HEAD_DIM
RWKV_HEAD = 64
RWKV_HEADS = D_RWKV // RWKV_HEAD
DECAY_LORA = 64
ICLR_LORA = 64
GATE_LORA = 160
D_IN = 3 * D_ATTN + 3 * D_RWKV
D_FF = 5632
CONV_W = 3
NUM_BUCKETS = 32
MAX_EXACT = NUM_BUCKETS // 2
MAX_DISTANCE = 128
Q_BLOCK = 128
NORM_EPS = 1e-6
RWKV_GN_EPS = 64e-5
NEG_INF = -1e30

kernel_name = "hybrid_diffattn_rwkv7_convffn_block"


def rmsnorm(x, g, eps=NORM_EPS):
    xf = x.astype(jnp.float32)
    y = xf * lax.rsqrt(jnp.mean(xf * xf, axis=-1, keepdims=True) + eps)
    return (y * g.astype(jnp.float32)).astype(x.dtype)


def token_shift(x):
    return jnp.pad(x, ((0, 0), (1, 0), (0, 0)))[:, :-1]


def t5_bucket(dist):
    n = jnp.maximum(dist, 0)
    nf = jnp.maximum(n, 1).astype(jnp.float32)
    large = MAX_EXACT + (jnp.log(nf / MAX_EXACT) / math.log(MAX_DISTANCE / MAX_EXACT)
                         * (NUM_BUCKETS - MAX_EXACT)).astype(jnp.int32)
    large = jnp.minimum(large, NUM_BUCKETS - 1)
    return jnp.where(n < MAX_EXACT, n, large)


def diff_attention(q, k, v, q_norm_g, k_norm_g, rel_bias, lam, subln_g, lambda_init):
    B, S = q.shape[0], q.shape[1]
    n_blk = S // Q_BLOCK
    q = rmsnorm(q, q_norm_g) * (ATTN_HEAD_DIM ** -0.5)
    k = rmsnorm(k, k_norm_g)
    kt = jnp.transpose(k, (0, 2, 3, 1, 4))
    vt = jnp.transpose(v, (0, 2, 1, 3))
    qb = jnp.transpose(q, (0, 2, 3, 1, 4)).reshape(B, ATTN_HEADS, 2, n_blk, Q_BLOCK, ATTN_HEAD_DIM)
    qb = jnp.transpose(qb, (3, 0, 1, 2, 4, 5))
    starts = jnp.arange(n_blk, dtype=jnp.int32) * Q_BLOCK
    key_pos = jnp.arange(S, dtype=jnp.int32)
    lam32 = lam.astype(jnp.float32)

    def block(args):
        q_blk, start = args
        s = jnp.einsum('bhiqd,bhikd->bhiqk', q_blk, kt).astype(jnp.float32)
        dist = (start + jnp.arange(Q_BLOCK, dtype=jnp.int32))[:, None] - key_pos[None, :]
        bias = jnp.transpose(rel_bias[t5_bucket(dist)], (2, 0, 1)).astype(jnp.float32)
        s = jnp.where(dist >= 0, s + bias[None, :, None], NEG_INF)
        p = jax.nn.softmax(s, axis=-1)
        attn = p[:, :, 0] - lam32 * p[:, :, 1]
        return jnp.einsum('bhqk,bhkv->bhqv', attn.astype(vt.dtype), vt)

    out = lax.map(block, (qb, starts))
    out = jnp.transpose(out, (1, 0, 3, 2, 4)).reshape(B, S, ATTN_HEADS, ATTN_V_DIM)
    out = rmsnorm(out, subln_g) * (1.0 - lambda_init)
    return out.reshape(B, S, D_ATTN)


def rwkv7_time_mix(h, r_p, k_p, v_p, mu_rkv, mu_wag, w0, w1, w2, a0, a1, a2, g1, g2,
                   k_k, k_a, r_k, ln_x_g, ln_x_b):
    B, S = h.shape[0], h.shape[1]
    f32 = jnp.float32
    r = r_p + (token_shift(r_p) - r_p) * mu_rkv[0]
    k = k_p + (token_shift(k_p) - k_p) * mu_rkv[1]
    v = v_p + (token_shift(v_p) - v_p) * mu_rkv[2]
    dh = token_shift(h) - h
    xw = h + dh * mu_wag[0]
    xa = h + dh * mu_wag[1]
    xg = h + dh * mu_wag[2]
    w = -jax.nn.softplus(-(w0 + jnp.tanh(xw @ w1) @ w2).astype(f32)) - 0.5
    decay = jnp.exp(-jnp.exp(w))
    a = jax.nn.sigmoid((a0 + (xa @ a1) @ a2).astype(f32))
    g = jax.nn.sigmoid(xg @ g1) @ g2
    k32 = k.astype(f32)
    kk = (k32 * k_k.astype(f32)).reshape(B, S, RWKV_HEADS, RWKV_HEAD)
    kk = kk / jnp.maximum(jnp.linalg.norm(kk, axis=-1, keepdims=True), 1e-12)
    k32 = k32 * (1.0 + (a - 1.0) * k_a.astype(f32))
    hs = lambda t: t.astype(f32).reshape(B, S, RWKV_HEADS, RWKV_HEAD)
    r32, w32, k32, v32, a32 = hs(r), hs(decay), hs(k32), hs(v), hs(a)
    tm = lambda t: jnp.moveaxis(t, 1, 0)

    def step(state, inp):
        r_t, w_t, k_t, v_t, kk_t, a_t = inp
        sa = jnp.einsum('bhvk,bhk->bhv', state, -kk_t)
        state = (state * w_t[:, :, None, :] + sa[..., None] * (kk_t * a_t)[:, :, None, :]
                 + v_t[..., None] * k_t[:, :, None, :])
        y_t = jnp.einsum('bhvk,bhk->bhv', state, r_t)
        return state, y_t

    s0 = jnp.zeros((B, RWKV_HEADS, RWKV_HEAD, RWKV_HEAD), f32)
    _, y = lax.scan(step, s0, (tm(r32), tm(w32), tm(k32), tm(v32), tm(kk), tm(a32)))
    y = jnp.moveaxis(y, 0, 1)
    mu = jnp.mean(y, axis=-1, keepdims=True)
    var = jnp.mean(jnp.square(y - mu), axis=-1, keepdims=True)
    yn = ((y - mu) * lax.rsqrt(var + RWKV_GN_EPS) * ln_x_g.astype(f32).reshape(RWKV_HEADS, RWKV_HEAD)
          + ln_x_b.astype(f32).reshape(RWKV_HEADS, RWKV_HEAD))
    bonus = jnp.sum(r32 * k32 * r_k.astype(f32), axis=-1, keepdims=True) * v32
    out = (yn + bonus).reshape(B, S, D_RWKV)
    return (out * g.astype(f32)).astype(h.dtype)


def conv_glu_ffn(h, w_up, conv_w, conv_b, w_down):
    S = h.shape[1]
    up = h @ w_up
    upp = jnp.pad(up, ((0, 0), (CONV_W - 1, 0), (0, 0)))
    y = conv_b + conv_w[0] * upp[:, 0:S] + conv_w[1] * upp[:, 1:S + 1] + conv_w[2] * upp[:, 2:S + 2]
    gate, val = jnp.split(y, 2, axis=-1)
    return (jax.nn.silu(gate) * val) @ w_down


def setup_inputs(seed: int = 0) -> dict:
    key = jax.random.key(seed)
    ks = iter(jax.random.split(key, 40))
    f32 = jnp.float32
    nrm = lambda shape, scale: jax.random.normal(next(ks), shape, f32) * scale
    L = DEPTH
    return {
        "x": nrm((BATCH, SEQ, D_MODEL), 1.0),
        "c": nrm((BATCH, D_MODEL), 1.0),
        "rel_bias": nrm((NUM_BUCKETS, ATTN_HEADS), 0.5),
        "w_ada": nrm((L, D_MODEL, 6 * D_MODEL), 0.5 * D_MODEL ** -0.5),
        "b_ada": nrm((L, 6 * D_MODEL), 0.02),
        "norm_mix_g": 1.0 + nrm((L, D_MODEL), 0.02),
        "w_in": nrm((L, D_MODEL, D_IN), D_MODEL ** -0.5),
        "q_norm_g": 1.0 + nrm((L, ATTN_HEAD_DIM), 0.02),
        "k_norm_g": 1.0 + nrm((L, ATTN_HEAD_DIM), 0.02),
        "lambda_q1": nrm((L, ATTN_HEAD_DIM), 0.1),
        "lambda_k1": nrm((L, ATTN_HEAD_DIM), 0.1),
        "lambda_q2": nrm((L, ATTN_HEAD_DIM), 0.1),
        "lambda_k2": nrm((L, ATTN_HEAD_DIM), 0.1),
        "attn_subln_g": 1.0 + nrm((L, ATTN_V_DIM), 0.02),
        "mu_rkv": jax.random.uniform(next(ks), (L, 3, D_RWKV), f32),
        "mu_wag": jax.random.uniform(next(ks), (L, 3, D_MODEL), f32),
        "w0": jax.random.uniform(next(ks), (L, D_RWKV), f32, -4.0, 1.0),
        "w1": nrm((L, D_MODEL, DECAY_LORA), D_MODEL ** -0.5),
        "w2": nrm((L, DECAY_LORA, D_RWKV), 0.3 * DECAY_LORA ** -0.5),
        "a0": nrm((L, D_RWKV), 0.1),
        "a1": nrm((L, D_MODEL, ICLR_LORA), D_MODEL ** -0.5),
        "a2": nrm((L, ICLR_LORA, D_RWKV), 0.5 * ICLR_LORA ** -0.5),
        "g1": nrm((L, D_MODEL, GATE_LORA), D_MODEL ** -0.5),
        "g2": nrm((L, GATE_LORA, D_RWKV), GATE_LORA ** -0.5),
        "k_k": 0.85 + nrm((L, D_RWKV), 0.02),
        "k_a": 1.0 + nrm((L, D_RWKV), 0.02),
        "r_k": nrm((L, RWKV_HEADS, RWKV_HEAD), 0.1),
        "ln_x_g": 1.0 + nrm((L, D_RWKV), 0.02),
        "ln_x_b": nrm((L, D_RWKV), 0.02),
        "w_out": nrm((L, D_MIX, D_MODEL), D_MIX ** -0.5),
        "norm_ffn_g": 1.0 + nrm((L, D_MODEL), 0.02),
        "w_up": nrm((L, D_MODEL, 2 * D_FF), D_MODEL ** -0.5),
        "conv_w": nrm((L, CONV_W, 2 * D_FF), CONV_W ** -0.5),
        "conv_b": nrm((L, 2 * D_FF), 0.02),
        "w_down": nrm((L, D_FF, D_MODEL), D_FF ** -0.5),
    }


def reference(x, c, rel_bias, w_ada, b_ada, norm_mix_g, w_in, q_norm_g, k_norm_g,
              lambda_q1, lambda_k1, lambda_q2, lambda_k2, attn_subln_g, mu_rkv, mu_wag,
              w0, w1, w2, a0, a1, a2, g1, g2, k_k, k_a, r_k, ln_x_g, ln_x_b, w_out,
              norm_ffn_g, w_up, conv_w, conv_b, w_down):
    B, S = x.shape[0], x.shape[1]
    c_act = jax.nn.silu(c)
    for l in range(DEPTH):
        lambda_init = 0.8 - 0.6 * math.exp(-0.3 * l)
        mod = c_act @ w_ada[l] + b_ada[l]
        shift_m, scale_m, gate_m, shift_f, scale_f, gate_f = jnp.split(mod, 6, axis=-1)

        h = rmsnorm(x, norm_mix_g[l]) * (1.0 + scale_m[:, None]) + shift_m[:, None]
        p = h @ w_in[l]
        q, k, v, r_p, kr_p, vr_p = jnp.split(p, 6, axis=-1)
        q = q.reshape(B, S, ATTN_HEADS, 2, ATTN_HEAD_DIM)
        k = k.reshape(B, S, ATTN_HEADS, 2, ATTN_HEAD_DIM)
        v = v.reshape(B, S, ATTN_HEADS, ATTN_V_DIM)
        lam = (jnp.exp(jnp.sum(lambda_q1[l].astype(jnp.float32) * lambda_k1[l].astype(jnp.float32)))
               - jnp.exp(jnp.sum(lambda_q2[l].astype(jnp.float32) * lambda_k2[l].astype(jnp.float32)))
               + lambda_init)
        o_attn = diff_attention(q, k, v, q_norm_g[l], k_norm_g[l], rel_bias, lam,
                                attn_subln_g[l], lambda_init)
        o_rwkv = rwkv7_time_mix(h, r_p, kr_p, vr_p, mu_rkv[l], mu_wag[l], w0[l], w1[l], w2[l],
                                a0[l], a1[l], a2[l], g1[l], g2[l], k_k[l], k_a[l], r_k[l],
                                ln_x_g[l], ln_x_b[l])
        mix = jnp.concatenate([o_attn, o_rwkv], axis=-1) @ w_out[l]
        x = x + gate_m[:, None] * mix

        h2 = rmsnorm(x, norm_ffn_g[l]) * (1.0 + scale_f[:, None]) + shift_f[:, None]
        x = x + gate_f[:, None] * conv_glu_ffn(h2, w_up[l], conv_w[l], conv_b[l], w_down[l])
    return x
```

```python
import functools
import math

import jax
import jax.numpy as jnp
from jax import lax
from jax.experimental import pallas as pl
from jax.experimental.pallas import tpu as pltpu

F32 = jnp.float32
BF16 = jnp.bfloat16

D_MODEL = 2048
D_ATTN = 1024
D_RWKV = 1024
HEAD_DIM = 64
ATTN_HEADS = 8
ATTN_V_DIM = 128
RWKV_HEADS = 16
D_FF = 5632
NUM_BUCKETS = 32
MAX_EXACT = 16
MAX_DISTANCE = 128
NORM_EPS = 1e-6
RWKV_GN_EPS = 64e-5
NEG_INF = -1e30
LAMBDA_INIT = 0.8 - 0.6 * math.exp(0.0)

LANES = 128
LORA_W = 128
LORA_G = 256
LORA_COLS = 2 * LORA_W + LORA_G

TM = 1024
TN = 512
TM_DOWN = 512
TQ = 256
CHUNK = 64
GROUP = 256
HALO = 8

VMEM_LIMIT = 56 * 1024 * 1024


def _mm(a, b):
    return jnp.dot(a.astype(BF16), b.astype(BF16), preferred_element_type=F32)


def _mm_nt(a, b):
    return lax.dot_general(a.astype(BF16), b.astype(BF16), (((1,), (1,)), ((), ())),
                           preferred_element_type=F32)


def _mm_tn(a, b):
    return lax.dot_general(a.astype(BF16), b.astype(BF16), (((0,), (0,)), ((), ())),
                           preferred_element_type=F32)


def _params(*sem):
    return pltpu.CompilerParams(dimension_semantics=sem, vmem_limit_bytes=VMEM_LIMIT)


def _adaln_kernel(c_ref, w_ref, b_ref, o_ref):
    c = c_ref[...]
    c_act = c * (1.0 / (1.0 + jnp.exp(-c)))
    acc = jnp.dot(c_act, w_ref[...], preferred_element_type=F32, precision=lax.Precision.HIGHEST)
    o_ref[...] = acc + b_ref[...]


def _adaln(c, w_ada, b_ada):
    bsz = c.shape[0]
    n_out = w_ada.shape[1]
    tn = 1024
    return pl.pallas_call(
        _adaln_kernel,
        grid=(n_out // tn,),
        in_specs=[pl.BlockSpec((bsz, D_MODEL), lambda j: (0, 0)),
                  pl.BlockSpec((D_MODEL, tn), lambda j: (0, j)),
                  pl.BlockSpec((1, tn), lambda j: (0, j))],
        out_specs=pl.BlockSpec((bsz, tn), lambda j: (0, j)),
        out_shape=jax.ShapeDtypeStruct((bsz, n_out), F32),
        compiler_params=_params("arbitrary"),
        name="adaln",
    )(c, w_ada, b_ada.reshape(1, n_out))


def _modulated_norm(x, g, scale, shift):
    y = x * lax.rsqrt(jnp.mean(x * x, axis=-1, keepdims=True) + NORM_EPS)
    return (y * g) * (1.0 + scale) + shift


def _proj_attn_kernel(x_ref, g_ref, sc_ref, sh_ref, w_ref, gain_ref, e_ref, o_ref, h_ref):
    j = pl.program_id(1)

    @pl.when(j == 0)
    def _():
        h_ref[...] = _modulated_norm(x_ref[...], g_ref[...], sc_ref[...], sh_ref[...]).astype(BF16)

    acc = jnp.dot(h_ref[...], w_ref[...], preferred_element_type=F32)

    @pl.when(j < (2 * D_ATTN) // TN)
    def _():
        half = TN // 2
        for s in range(2):
            a = acc[:, s * half:(s + 1) * half]
            ss = _mm(a * a, e_ref[...])
            y = a * lax.rsqrt(ss * (1.0 / HEAD_DIM) + NORM_EPS) * gain_ref[:, s * half:(s + 1) * half]
            o_ref[:, s * half:(s + 1) * half] = y.astype(o_ref.dtype)

    @pl.when(j >= (2 * D_ATTN) // TN)
    def _():
        o_ref[...] = acc.astype(o_ref.dtype)


def _proj_rwkv_kernel(x_ref, g_ref, sc_ref, sh_ref, w_ref, o_ref, h_ref):
    j = pl.program_id(1)

    @pl.when(j == 0)
    def _():
        h_ref[...] = _modulated_norm(x_ref[...], g_ref[...], sc_ref[...], sh_ref[...]).astype(BF16)

    o_ref[...] = jnp.dot(h_ref[...], w_ref[...], preferred_element_type=F32)


def _norm_proj_specs(seq, col_scale, col_shift):
    tiles_per_batch = seq // TM
    return [pl.BlockSpec((TM, D_MODEL), lambda i, j: (i, 0)),
            pl.BlockSpec((1, D_MODEL), lambda i, j: (0, 0)),
            pl.BlockSpec((None, 1, D_MODEL), lambda i, j: (i // tiles_per_batch, 0, col_scale)),
            pl.BlockSpec((None, 1, D_MODEL), lambda i, j: (i // tiles_per_batch, 0, col_shift)),
            pl.BlockSpec((D_MODEL, TN), lambda i, j: (0, j))]


def _proj_attn(x2, norm_g, mod3, w_bf, gain, e256, seq):
    n = x2.shape[0]
    n_out = w_bf.shape[1]
    return pl.pallas_call(
        _proj_attn_kernel,
        grid=(n // TM, n_out // TN),
        in_specs=_norm_proj_specs(seq, 1, 0) + [
            pl.BlockSpec((1, TN), lambda i, j: (0, j)),
            pl.BlockSpec((TN // 2, TN // 2), lambda i, j: (0, 0))],
        out_specs=pl.BlockSpec((TM, TN), lambda i, j: (i, j)),
        out_shape=jax.ShapeDtypeStruct((n, n_out), BF16),
        scratch_shapes=[pltpu.VMEM((TM, D_MODEL), BF16)],
        compiler_params=_params("arbitrary", "arbitrary"),
        name="proj_attn",
    )(x2, norm_g, mod3, mod3, w_bf, gain, e256)


def _proj_rwkv(x2, norm_g, mod3, w_bf, seq):
    n = x2.shape[0]
    n_out = w_bf.shape[1]
    return pl.pallas_call(
        _proj_rwkv_kernel,
        grid=(n // TM, n_out // TN),
        in_specs=_norm_proj_specs(seq, 1, 0),
        out_specs=pl.BlockSpec((TM, TN), lambda i, j: (i, j)),
        out_shape=jax.ShapeDtypeStruct((n, n_out), F32),
        scratch_shapes=[pltpu.VMEM((TM, D_MODEL), BF16)],
        compiler_params=_params("arbitrary", "arbitrary"),
        name="proj_rwkv",
    )(x2, norm_g, mod3, mod3, w_bf)


def _attn_tables_kernel(rb_ref, lq1_ref, lk1_ref, lq2_ref, lk2_ref, bias_ref, lam_ref):
    h = pl.program_id(0)
    rows = lax.broadcasted_iota(jnp.int32, (TQ, TQ), 0)
    cols = lax.broadcasted_iota(jnp.int32, (TQ, TQ), 1)
    far = rb_ref[NUM_BUCKETS - 1, h]
    for t in range(2):
        dist = rows - cols + t * TQ
        n = jnp.maximum(dist, 0)
        nf = jnp.maximum(n, 1).astype(F32)
        large = MAX_EXACT + (jnp.log(nf / MAX_EXACT) / math.log(MAX_DISTANCE / MAX_EXACT)
                             * (NUM_BUCKETS - MAX_EXACT)).astype(jnp.int32)
        large = jnp.minimum(large, NUM_BUCKETS - 1)
        bucket = jnp.where(n < MAX_EXACT, n, large)
        bias = jnp.zeros((TQ, TQ), F32)
        for b in range(NUM_BUCKETS):
            bias = jnp.where(bucket == b, rb_ref[b, h], bias)
        bias_ref[t] = jnp.where(dist >= 0, bias - far, NEG_INF)
    s1 = jnp.sum(lq1_ref[...] * lk1_ref[...], axis=1, keepdims=True)
    s2 = jnp.sum(lq2_ref[...] * lk2_ref[...], axis=1, keepdims=True)
    lam = jnp.exp(s1) - jnp.exp(s2) + LAMBDA_INIT
    lam_ref[...] = jnp.broadcast_to(lam, lam_ref.shape)


def _attn_tables(rel_bias, lq1, lk1, lq2, lk2):
    vec = pl.BlockSpec((1, HEAD_DIM), lambda h: (0, 0))
    return pl.pallas_call(
        _attn_tables_kernel,
        grid=(ATTN_HEADS,),
        in_specs=[pl.BlockSpec(memory_space=pltpu.SMEM), vec, vec, vec, vec],
        out_specs=[pl.BlockSpec((None, 2, TQ, TQ), lambda h: (h, 0, 0, 0)),
                   pl.BlockSpec((8, LANES), lambda h: (0, 0))],
        out_shape=[jax.ShapeDtypeStruct((ATTN_HEADS, 2, TQ, TQ), F32),
                   jax.ShapeDtypeStruct((8, LANES), F32)],
        compiler_params=_params("arbitrary"),
        name="attn_tables",
    )(rel_bias, lq1, lk1, lq2, lk2)


def _attn_block(qs_ref, k_ref, v_ref, m_ref, l_ref, acc_ref, j, bias):
    start = pl.multiple_of(j * TQ, TQ)
    kb = k_ref[pl.ds(start, TQ), :]
    vb = v_ref[pl.ds(start, TQ), :]
    s = lax.dot_general(qs_ref[...], kb, (((1,), (1,)), ((), ())), preferred_element_type=F32)
    if bias is not None:
        s = s + bias
    m_prev = m_ref[...]
    m_new = jnp.maximum(m_prev, jnp.max(s, axis=1, keepdims=True))
    alpha = jnp.exp(m_prev - m_new)
    p = jnp.exp(s - jnp.tile(m_new, (1, TQ // LANES)))
    l_ref[...] = alpha * l_ref[...] + jnp.sum(p, axis=1, keepdims=True)
    acc_ref[...] = alpha * acc_ref[...] + jnp.dot(p.astype(BF16), vb, preferred_element_type=F32)
    m_ref[...] = m_new


def _diff_attn_kernel(q_ref, k_ref, v_ref, bias_ref, lam_ref, sg_ref, o_ref,
                      qs_ref, m_ref, l_ref, acc_ref):
    qi = pl.program_id(2)
    q = q_ref[...]
    lane = lax.broadcasted_iota(jnp.int32, q.shape, 1)
    zero = jnp.zeros_like(q)
    qs_ref[0:TQ, :] = jnp.where(lane < HEAD_DIM, q, zero)
    qs_ref[TQ:2 * TQ, :] = jnp.where(lane >= HEAD_DIM, q, zero)
    m_ref[...] = jnp.full(m_ref.shape, NEG_INF, F32)
    l_ref[...] = jnp.zeros(l_ref.shape, F32)
    acc_ref[...] = jnp.zeros(acc_ref.shape, F32)

    def far_body(j, carry):
        _attn_block(qs_ref, k_ref, v_ref, m_ref, l_ref, acc_ref, j, None)
        return carry

    lax.fori_loop(0, jnp.maximum(qi - 1, 0), far_body, 0)

    @pl.when(qi >= 1)
    def _():
        b = bias_ref[1]
        _attn_block(qs_ref, k_ref, v_ref, m_ref, l_ref, acc_ref, qi - 1, jnp.concatenate([b, b], axis=0))

    b = bias_ref[0]
    _attn_block(qs_ref, k_ref, v_ref, m_ref, l_ref, acc_ref, qi, jnp.concatenate([b, b], axis=0))

    o = acc_ref[...] / l_ref[...]
    d = o[0:TQ, :] - lam_ref[0:1, :] * o[TQ:2 * TQ, :]
    y = d * lax.rsqrt(jnp.mean(d * d, axis=-1, keepdims=True) + NORM_EPS)
    o_ref[...] = (y * sg_ref[...] * (1.0 - LAMBDA_INIT)).astype(o_ref.dtype)


def _diff_attn(qkv, bias_tiles, lam, subln_g, bsz, seq):
    nq = seq // TQ
    h_blocks = D_ATTN // ATTN_V_DIM
    return pl.pallas_call(
        _diff_attn_kernel,
        grid=(bsz, ATTN_HEADS, nq),
        in_specs=[pl.BlockSpec((TQ, ATTN_V_DIM), lambda b, h, i: (b * nq + i, h)),
                  pl.BlockSpec((seq, ATTN_V_DIM), lambda b, h, i: (b, h_blocks + h)),
                  pl.BlockSpec((seq, ATTN_V_DIM), lambda b, h, i: (b, 2 * h_blocks + h)),
                  pl.BlockSpec((None, 2, TQ, TQ), lambda b, h, i: (h, 0, 0, 0)),
                  pl.BlockSpec((8, LANES), lambda b, h, i: (0, 0)),
                  pl.BlockSpec((1, ATTN_V_DIM), lambda b, h, i: (0, 0))],
        out_specs=pl.BlockSpec((TQ, ATTN_V_DIM), lambda b, h, i: (b * nq + i, h)),
        out_shape=jax.ShapeDtypeStruct((bsz * seq, D_ATTN), BF16),
        scratch_shapes=[pltpu.VMEM((2 * TQ, ATTN_V_DIM), BF16),
                        pltpu.VMEM((2 * TQ, LANES), F32),
                        pltpu.VMEM((2 * TQ, LANES), F32),
                        pltpu.VMEM((2 * TQ, ATTN_V_DIM), F32)],
        compiler_params=_params("arbitrary", "arbitrary", "arbitrary"),
        name="diff_attn",
    )(qkv, qkv, qkv, bias_tiles, lam, subln_g)


P_MU_R, P_MU_K, P_MU_V, P_W0, P_A0, P_KK, P_KA, P_RK, P_LNG, P_LNB = range(10)
P_ROWS = 16


def _block_diag(x, mask):
    return jnp.where(mask, jnp.concatenate([x] * (GROUP // HEAD_DIM), axis=0), 0.0)


def _rwkv_chunk(r, wlog, cum, k, v, an, b, st_ref, gi, bd_mask, m_strict, m_incl, eye):
    tot = cum[CHUNK - 1:CHUNK, :]
    e_neg = jnp.exp(-cum)
    e_rem = jnp.exp(tot - cum)
    at = jnp.exp(cum - wlog) * an
    rt = jnp.exp(cum) * r
    bd = functools.partial(_block_diag, mask=bd_mask)

    aa = _mm_nt(jnp.concatenate([at, rt], axis=0),
                jnp.concatenate([bd(e_neg * b), bd(e_neg * k)], axis=0))
    a_ab = jnp.where(m_strict, aa[:CHUNK, :GROUP], 0.0)
    a_ak = jnp.where(m_strict, aa[:CHUNK, GROUP:], 0.0)
    a_rb = jnp.where(m_incl, aa[CHUNK:, :GROUP], 0.0)
    a_rk = jnp.where(m_incl, aa[CHUNK:, GROUP:], 0.0)

    nk = a_ab
    minv = eye + nk
    nk = _mm(nk, bd(nk))
    steps = int(math.log2(CHUNK)) - 2
    for _ in range(steps):
        res = _mm(jnp.concatenate([nk, minv], axis=0), bd(nk))
        nk = res[:CHUNK]
        minv = minv + res[CHUNK:]
    minv = minv + _mm(minv, bd(nk))

    bd_v = bd(v)
    x1 = _mm(a_ak, bd_v)
    ma = _mm(minv, jnp.concatenate([bd(at), bd(x1)], axis=1))
    am, u0 = ma[:, :GROUP], ma[:, GROUP:]

    st = st_ref[gi]
    c1 = _mm(jnp.concatenate([am, rt], axis=0), st)
    u = c1[:CHUNK] + u0
    uv = jnp.concatenate([bd(u), bd_v], axis=0)
    y = c1[CHUNK:] + _mm(jnp.concatenate([a_rb, a_rk], axis=1), uv)
    upd = _mm_tn(jnp.concatenate([bd(e_rem * b), bd(e_rem * k)], axis=0), uv)
    tot_col = lax.dot_general(wlog, jnp.ones((CHUNK, GROUP), F32), (((0,), (0,)), ((), ())),
                              preferred_element_type=F32, precision=lax.Precision.HIGHEST)
    st_ref[gi] = jnp.where(bd_mask, jnp.exp(tot_col) * st + upd, 0.0)
    return y


def _rwkv_kernel(rp_ref, kp_ref, vp_ref, lo_ref, pv_ref, w2_ref, a2_ref, g2_ref, e_ref, tri_ref,
                 o_ref, st_ref, prev_ref):
    c = pl.program_id(1)

    @pl.when(c == 0)
    def _():
        st_ref[...] = jnp.zeros(st_ref.shape, F32)
        prev_ref[...] = jnp.zeros(prev_ref.shape, F32)

    row0 = lax.broadcasted_iota(jnp.int32, (CHUNK, D_RWKV), 0) == 0

    def shifted(x, col):
        prev = prev_ref[0:1, col * D_RWKV:(col + 1) * D_RWKV]
        return jnp.where(row0, prev, pltpu.roll(x, 1, axis=0))

    pv = lambda i: pv_ref[i:i + 1, :]
    rp, kp, vp, lo = rp_ref[...], kp_ref[...], vp_ref[...], lo_ref[...]
    lo_prev = shifted(lo, 3)
    r = rp + (shifted(rp, 0) - rp) * pv(P_MU_R)
    k = kp + (shifted(kp, 1) - kp) * pv(P_MU_K)
    v = vp + (shifted(vp, 2) - vp) * pv(P_MU_V)
    for col, ref in enumerate((rp_ref, kp_ref, vp_ref, lo_ref)):
        prev_ref[0:1, col * D_RWKV:(col + 1) * D_RWKV] = ref[CHUNK - 1:CHUNK, :]

    lora = lo[:, :LORA_COLS] + lo_prev[:, LORA_COLS:]
    w_in = pv(P_W0) + _mm(jnp.tanh(lora[:, :LORA_W]), w2_ref[...])
    w = -(jnp.maximum(-w_in, 0.0) + jnp.log(1.0 + jnp.exp(-jnp.abs(w_in)))) - 0.5
    wlog = -jnp.exp(w)
    a = 1.0 / (1.0 + jnp.exp(-(pv(P_A0) + _mm(lora[:, LORA_W:2 * LORA_W], a2_ref[...]))))
    g = _mm(1.0 / (1.0 + jnp.exp(-lora[:, 2 * LORA_W:])), g2_ref[...])

    def head_sum(x):
        return jnp.concatenate([_mm(x[:, i * GROUP:(i + 1) * GROUP], e_ref[...])
                                for i in range(D_RWKV // GROUP)], axis=1)

    kk = k * pv(P_KK)
    kk = kk / jnp.maximum(jnp.sqrt(head_sum(kk * kk)), 1e-12)
    k2 = k * (1.0 + (a - 1.0) * pv(P_KA))
    an = -kk
    b = kk * a
    cum = jnp.dot(tri_ref[...], wlog, preferred_element_type=F32, precision=lax.Precision.HIGHEST)

    rows_bd = lax.broadcasted_iota(jnp.int32, (GROUP, GROUP), 0) // HEAD_DIM
    lanes_bd = lax.broadcasted_iota(jnp.int32, (GROUP, GROUP), 1) // HEAD_DIM
    bd_mask = rows_bd == lanes_bd
    t_idx = lax.broadcasted_iota(jnp.int32, (CHUNK, GROUP), 0)
    s_idx = lax.broadcasted_iota(jnp.int32, (CHUNK, GROUP), 1) % HEAD_DIM
    m_strict = s_idx < t_idx
    m_incl = s_idx <= t_idx
    eye = jnp.where(s_idx == t_idx, 1.0, 0.0).astype(F32)

    ys = []
    for gi in range(D_RWKV // GROUP):
        sl = slice(gi * GROUP, (gi + 1) * GROUP)
        ys.append(_rwkv_chunk(r[:, sl], wlog[:, sl], cum[:, sl], k2[:, sl], v[:, sl], an[:, sl],
                              b[:, sl], st_ref, gi, bd_mask, m_strict, m_incl, eye))
    y = jnp.concatenate(ys, axis=1)

    mu = head_sum(y) * (1.0 / HEAD_DIM)
    d = y - mu
    var = head_sum(d * d) * (1.0 / HEAD_DIM)
    yn = d * lax.rsqrt(var + RWKV_GN_EPS) * pv(P_LNG) + pv(P_LNB)
    bonus = head_sum(r * k2 * pv(P_RK)) * v
    o_ref[...] = ((yn + bonus) * g).astype(o_ref.dtype)


def _rwkv(pr, pvec, w2p, a2p, g2p, e256, tri, bsz, seq):
    nc = seq // CHUNK
    col = lambda cb: pl.BlockSpec((CHUNK, D_RWKV), lambda b, c: (b * nc + c, cb))
    full = lambda shape: pl.BlockSpec(shape, lambda b, c: (0, 0))
    return pl.pallas_call(
        _rwkv_kernel,
        grid=(bsz, nc),
        in_specs=[col(0), col(1), col(2), col(3),
                  full((P_ROWS, D_RWKV)), full((LORA_W, D_RWKV)), full((LORA_W, D_RWKV)),
                  full((LORA_G, D_RWKV)), full((GROUP, GROUP)), full((CHUNK, CHUNK))],
        out_specs=pl.BlockSpec((CHUNK, D_RWKV), lambda b, c: (b * nc + c, 0)),
        out_shape=jax.ShapeDtypeStruct((bsz * seq, D_RWKV), BF16),
        scratch_shapes=[pltpu.VMEM((D_RWKV // GROUP, GROUP, GROUP), F32),
                        pltpu.VMEM((8, 4 * D_RWKV), F32)],
        compiler_params=_params("arbitrary", "arbitrary"),
        name="rwkv7",
    )(pr, pr, pr, pr, pvec, w2p, a2p, g2p, e256, tri)


def _out_proj_kernel(oa_ref, or_ref, wa_ref, wr_ref, x_ref, gate_ref, o_ref):
    acc = jnp.dot(oa_ref[...], wa_ref[...], preferred_element_type=F32)
    acc = acc + jnp.dot(or_ref[...], wr_ref[...], preferred_element_type=F32)
    o_ref[...] = x_ref[...] + gate_ref[...] * acc


def _out_proj(o_attn, o_rwkv, w_bf, x2, mod3, seq):
    n = x2.shape[0]
    tiles_per_batch = seq // TM
    gate_col = 2 * (D_MODEL // TN)
    return pl.pallas_call(
        _out_proj_kernel,
        grid=(n // TM, D_MODEL // TN),
        in_specs=[pl.BlockSpec((TM, D_ATTN), lambda i, j: (i, 0)),
                  pl.BlockSpec((TM, D_RWKV), lambda i, j: (i, 0)),
                  pl.BlockSpec((D_ATTN, TN), lambda i, j: (0, j)),
                  pl.BlockSpec((D_RWKV, TN), lambda i, j: (1, j)),
                  pl.BlockSpec((TM, TN), lambda i, j: (i, j)),
                  pl.BlockSpec((None, 1, TN), lambda i, j: (i // tiles_per_batch, 0, gate_col + j))],
        out_specs=pl.BlockSpec((TM, TN), lambda i, j: (i, j)),
        out_shape=jax.ShapeDtypeStruct((n, D_MODEL), F32),
        compiler_params=_params("arbitrary", "arbitrary"),
        name="out_proj",
    )(o_attn, o_rwkv, w_bf, w_bf, x2, mod3)


def _ffn_up_kernel(x_ref, xh_ref, g_ref, sc_ref, sh_ref, wg_ref, wv_ref, cwg_ref, cwv_ref,
                   cbg_ref, cbv_ref, o_ref, h_ref, *, tiles_per_batch):
    i = pl.program_id(0)
    j = pl.program_id(1)

    @pl.when(j == 0)
    def _():
        g, sc, sh = g_ref[...], sc_ref[...], sh_ref[...]
        h_ref[0:HALO, :] = _modulated_norm(xh_ref[...], g, sc, sh).astype(BF16)
        h_ref[HALO:, :] = _modulated_norm(x_ref[...], g, sc, sh).astype(BF16)

    first = (i % tiles_per_batch) == 0
    halo_rows = lax.broadcasted_iota(jnp.int32, (TM + HALO, TN), 0) < HALO
    kill = jnp.logical_and(first, halo_rows)

    def conv(w_ref, cw_ref, cb_ref):
        up = jnp.dot(h_ref[...], w_ref[...], preferred_element_type=F32)
        up = jnp.where(kill, 0.0, up)
        y = (cb_ref[...] + cw_ref[0:1, :] * pltpu.roll(up, 2, axis=0)
             + cw_ref[1:2, :] * pltpu.roll(up, 1, axis=0) + cw_ref[2:3, :] * up)
        return y[HALO:, :]

    gate = conv(wg_ref, cwg_ref, cbg_ref)
    val = conv(wv_ref, cwv_ref, cbv_ref)
    o_ref[...] = (gate * (1.0 / (1.0 + jnp.exp(-gate))) * val).astype(o_ref.dtype)


def _ffn_up(x1, norm_g, mod3, w_bf, conv_w8, conv_b, seq):
    n = x1.shape[0]
    tiles_per_batch = seq // TM
    nf = D_FF // TN
    batch = lambda i, j: i // tiles_per_batch
    return pl.pallas_call(
        functools.partial(_ffn_up_kernel, tiles_per_batch=tiles_per_batch),
        grid=(n // TM, nf),
        in_specs=[pl.BlockSpec((TM, D_MODEL), lambda i, j: (i, 0)),
                  pl.BlockSpec((HALO, D_MODEL), lambda i, j: (jnp.maximum(i * (TM // HALO) - 1, 0), 0)),
                  pl.BlockSpec((1, D_MODEL), lambda i, j: (0, 0)),
                  pl.BlockSpec((None, 1, D_MODEL), lambda i, j: (batch(i, j), 0, 4)),
                  pl.BlockSpec((None, 1, D_MODEL), lambda i, j: (batch(i, j), 0, 3)),
                  pl.BlockSpec((D_MODEL, TN), lambda i, j: (0, j)),
                  pl.BlockSpec((D_MODEL, TN), lambda i, j: (0, nf + j)),
                  pl.BlockSpec((8, TN), lambda i, j: (0, j)),
                  pl.BlockSpec((8, TN), lambda i, j: (0, nf + j)),
                  pl.BlockSpec((1, TN), lambda i, j: (0, j)),
                  pl.BlockSpec((1, TN), lambda i, j: (0, nf + j))],
        out_specs=pl.BlockSpec((TM, TN), lambda i, j: (i, j)),
        out_shape=jax.ShapeDtypeStruct((n, D_FF), BF16),
        scratch_shapes=[pltpu.VMEM((TM + HALO, D_MODEL), BF16)],
        compiler_params=_params("arbitrary", "arbitrary"),
        name="ffn_up",
    )(x1, x1, norm_g, mod3, mod3, w_bf, w_bf, conv_w8, conv_w8, conv_b, conv_b)


def _ffn_down_kernel(a_ref, w_ref, x_ref, gate_ref, o_ref):
    acc = jnp.dot(a_ref[...], w_ref[...], preferred_element_type=F32)
    o_ref[...] = x_ref[...] + gate_ref[...] * acc


def _ffn_down(act, w_bf, x1, mod3, seq):
    n = x1.shape[0]
    tiles_per_batch = seq // TM_DOWN
    gate_col = 5 * (D_MODEL // TN)
    return pl.pallas_call(
        _ffn_down_kernel,
        grid=(n // TM_DOWN, D_MODEL // TN),
        in_specs=[pl.BlockSpec((TM_DOWN, D_FF), lambda i, j: (i, 0)),
                  pl.BlockSpec((D_FF, TN), lambda i, j: (0, j)),
                  pl.BlockSpec((TM_DOWN, TN), lambda i, j: (i, j)),
                  pl.BlockSpec((None, 1, TN), lambda i, j: (i // tiles_per_batch, 0, gate_col + j))],
        out_specs=pl.BlockSpec((TM_DOWN, TN), lambda i, j: (i, j)),
        out_shape=jax.ShapeDtypeStruct((n, D_MODEL), F32),
        compiler_params=_params("arbitrary", "arbitrary"),
        name="ffn_down",
    )(act, w_bf, x1, mod3)


def _pad_cols(w, width):
    return jnp.pad(w, ((0, 0), (0, width - w.shape[1])))


def _pad_rows(w, height):
    return jnp.pad(w, ((0, height - w.shape[0]), (0, 0)))


def kernel(x, c, rel_bias, w_ada, b_ada, norm_mix_g, w_in, q_norm_g, k_norm_g, lambda_q1, lambda_k1,
           lambda_q2, lambda_k2, attn_subln_g, mu_rkv, mu_wag, w0, w1, w2, a0, a1, a2, g1, g2, k_k,
           k_a, r_k, ln_x_g, ln_x_b, w_out, norm_ffn_g, w_up, conv_w, conv_b, w_down):
    bsz, seq, _ = x.shape
    n = bsz * seq
    x2 = x.reshape(n, D_MODEL)

    mod = _adaln(c, w_ada[0], b_ada[0])
    mod3 = mod.reshape(bsz, 1, 6 * D_MODEL)

    idx = jnp.arange(GROUP) // HEAD_DIM
    e256 = (idx[:, None] == idx[None, :]).astype(BF16)
    tri = (jnp.arange(CHUNK)[:, None] >= jnp.arange(CHUNK)[None, :]).astype(F32)

    w_in0 = w_in[0]
    gain = jnp.concatenate([jnp.tile(q_norm_g[0], D_ATTN // HEAD_DIM) * (HEAD_DIM ** -0.5),
                            jnp.tile(k_norm_g[0], D_ATTN // HEAD_DIM),
                            jnp.ones((D_ATTN,), F32)]).reshape(1, 3 * D_ATTN)
    qkv = _proj_attn(x2, norm_mix_g, mod3, w_in0[:, :3 * D_ATTN].astype(BF16), gain, e256, seq)

    mu = mu_wag[0]
    lora_w = (_pad_cols(w1[0], LORA_W), _pad_cols(a1[0], LORA_W), _pad_cols(g1[0], LORA_G))
    lora_cur = jnp.concatenate([(1.0 - mu[i])[:, None] * w for i, w in enumerate(lora_w)], axis=1)
    lora_prev = jnp.concatenate([mu[i][:, None] * w for i, w in enumerate(lora_w)], axis=1)
    w_rwkv = jnp.concatenate([w_in0[:, 3 * D_ATTN:], lora_cur, lora_prev], axis=1).astype(BF16)
    pr = _proj_rwkv(x2, norm_mix_g, mod3, w_rwkv, seq)

    bias_tiles, lam = _attn_tables(rel_bias, lambda_q1, lambda_k1, lambda_q2, lambda_k2)
    o_attn = _diff_attn(qkv, bias_tiles, lam, attn_subln_g, bsz, seq)

    pvec = jnp.concatenate([mu_rkv[0], w0, a0, k_k, k_a, r_k.reshape(1, D_RWKV), ln_x_g, ln_x_b,
                            jnp.zeros((P_ROWS - 10, D_RWKV), F32)], axis=0)
    o_rwkv = _rwkv(pr, pvec, _pad_rows(w2[0], LORA_W).astype(BF16), _pad_rows(a2[0], LORA_W).astype(BF16),
                   _pad_rows(g2[0], LORA_G).astype(BF16), e256, tri, bsz, seq)

    x1 = _out_proj(o_attn, o_rwkv, w_out[0].astype(BF16), x2, mod3, seq)

    act = _ffn_up(x1, norm_ffn_g, mod3, w_up[0].astype(BF16), _pad_rows(conv_w[0], 8), conv_b, seq)
    out = _ffn_down(act, w_down[0].astype(BF16), x1, mod3, seq)
    return out.reshape(bsz, seq, D_MODEL)
```

```python
import functools
import math

import jax
import jax.numpy as jnp
from jax import lax
from jax.experimental import pallas as pl
from jax.experimental.pallas import tpu as pltpu

F32 = jnp.float32
BF16 = jnp.bfloat16

D_MODEL = 2048
D_ATTN = 1024
D_RWKV = 1024
HEAD_DIM = 64
ATTN_HEADS = 8
ATTN_V_DIM = 128
RWKV_HEADS = 16
D_FF = 5632
NUM_BUCKETS = 32
MAX_EXACT = 16
MAX_DISTANCE = 128
NORM_EPS = 1e-6
RWKV_GN_EPS = 64e-5
NEG_INF = -1e30
LOG2E = math.log2(math.e)
LAMBDA_INIT = 0.8 - 0.6 * math.exp(0.0)

LANES = 128
LORA_W = 128
LORA_G = 256
LORA_COLS = 2 * LORA_W + LORA_G

TM = 1024
TN = 512
TM_DOWN = 512
TQ = 256
CHUNK = 64
GROUP = 256
N_GROUPS = D_RWKV // GROUP
HALO = 8

VMEM_LIMIT = 56 * 1024 * 1024


def _mm(a, b):
    return jnp.dot(a.astype(BF16), b.astype(BF16), preferred_element_type=F32)


def _mm_nt(a, b):
    return lax.dot_general(a.astype(BF16), b.astype(BF16), (((1,), (1,)), ((), ())),
                           preferred_element_type=F32)


def _mm_tn(a, b):
    return lax.dot_general(a.astype(BF16), b.astype(BF16), (((0,), (0,)), ((), ())),
                           preferred_element_type=F32)


def _params(*sem):
    return pltpu.CompilerParams(dimension_semantics=sem, vmem_limit_bytes=VMEM_LIMIT)


def _adaln_kernel(c_ref, w_ref, b_ref, o_ref):
    c = c_ref[...]
    c_act = c * (1.0 / (1.0 + jnp.exp(-c)))
    acc = jnp.dot(c_act, w_ref[...], preferred_element_type=F32, precision=lax.Precision.HIGHEST)
    o_ref[...] = acc + b_ref[...]


def _adaln(c, w_ada, b_ada):
    bsz = c.shape[0]
    n_out = w_ada.shape[1]
    tn = 1024
    return pl.pallas_call(
        _adaln_kernel,
        grid=(n_out // tn,),
        in_specs=[pl.BlockSpec((bsz, D_MODEL), lambda j: (0, 0)),
                  pl.BlockSpec((D_MODEL, tn), lambda j: (0, j)),
                  pl.BlockSpec((1, tn), lambda j: (0, j))],
        out_specs=pl.BlockSpec((bsz, tn), lambda j: (0, j)),
        out_shape=jax.ShapeDtypeStruct((bsz, n_out), F32),
        compiler_params=_params("arbitrary"),
        name="adaln",
    )(c, w_ada, b_ada.reshape(1, n_out))


def _modulated_norm(x, g, scale, shift):
    y = x * lax.rsqrt(jnp.mean(x * x, axis=-1, keepdims=True) + NORM_EPS)
    return (y * g) * (1.0 + scale) + shift


def _proj_attn_kernel(x_ref, g_ref, sc_ref, sh_ref, w_ref, gain_ref, e_ref, o_ref, h_ref):
    j = pl.program_id(1)

    @pl.when(j == 0)
    def _():
        h_ref[...] = _modulated_norm(x_ref[...], g_ref[...], sc_ref[...], sh_ref[...]).astype(BF16)

    acc = jnp.dot(h_ref[...], w_ref[...], preferred_element_type=F32)

    @pl.when(j < (2 * D_ATTN) // TN)
    def _():
        half = TN // 2
        for s in range(2):
            a = acc[:, s * half:(s + 1) * half]
            ss = _mm(a * a, e_ref[...])
            y = a * lax.rsqrt(ss * (1.0 / HEAD_DIM) + NORM_EPS) * gain_ref[:, s * half:(s + 1) * half]
            o_ref[:, s * half:(s + 1) * half] = y.astype(o_ref.dtype)

    @pl.when(j >= (2 * D_ATTN) // TN)
    def _():
        o_ref[...] = acc.astype(o_ref.dtype)


def _proj_rwkv_kernel(x_ref, g_ref, sc_ref, sh_ref, w_ref, o_ref, h_ref):
    j = pl.program_id(1)

    @pl.when(j == 0)
    def _():
        h_ref[...] = _modulated_norm(x_ref[...], g_ref[...], sc_ref[...], sh_ref[...]).astype(BF16)

    o_ref[...] = jnp.dot(h_ref[...], w_ref[...], preferred_element_type=F32)


def _norm_proj_specs(seq, col_scale, col_shift):
    tiles_per_batch = seq // TM
    return [pl.BlockSpec((TM, D_MODEL), lambda i, j: (i, 0)),
            pl.BlockSpec((1, D_MODEL), lambda i, j: (0, 0)),
            pl.BlockSpec((None, 1, D_MODEL), lambda i, j: (i // tiles_per_batch, 0, col_scale)),
            pl.BlockSpec((None, 1, D_MODEL), lambda i, j: (i // tiles_per_batch, 0, col_shift)),
            pl.BlockSpec((D_MODEL, TN), lambda i, j: (0, j))]


def _proj_attn(x2, norm_g, mod3, w_bf, gain, e256, seq):
    n = x2.shape[0]
    n_out = w_bf.shape[1]
    return pl.pallas_call(
        _proj_attn_kernel,
        grid=(n // TM, n_out // TN),
        in_specs=_norm_proj_specs(seq, 1, 0) + [
            pl.BlockSpec((1, TN), lambda i, j: (0, j)),
            pl.BlockSpec((TN // 2, TN // 2), lambda i, j: (0, 0))],
        out_specs=pl.BlockSpec((TM, TN), lambda i, j: (i, j)),
        out_shape=jax.ShapeDtypeStruct((n, n_out), BF16),
        scratch_shapes=[pltpu.VMEM((TM, D_MODEL), BF16)],
        compiler_params=_params("arbitrary", "arbitrary"),
        name="proj_attn",
    )(x2, norm_g, mod3, mod3, w_bf, gain, e256)


def _proj_rwkv(x2, norm_g, mod3, w_bf, seq):
    n = x2.shape[0]
    n_out = w_bf.shape[1]
    return pl.pallas_call(
        _proj_rwkv_kernel,
        grid=(n // TM, n_out // TN),
        in_specs=_norm_proj_specs(seq, 1, 0),
        out_specs=pl.BlockSpec((TM, TN), lambda i, j: (i, j)),
        out_shape=jax.ShapeDtypeStruct((n, n_out), F32),
        scratch_shapes=[pltpu.VMEM((TM, D_MODEL), BF16)],
        compiler_params=_params("arbitrary", "arbitrary"),
        name="proj_rwkv",
    )(x2, norm_g, mod3, mod3, w_bf)


def _attn_tables_kernel(rb_ref, lq1_ref, lk1_ref, lq2_ref, lk2_ref, bias_ref, lam_ref):
    h = pl.program_id(0)
    rows = lax.broadcasted_iota(jnp.int32, (2 * TQ, TQ), 0)
    cols = lax.broadcasted_iota(jnp.int32, (2 * TQ, TQ), 1)
    far = rb_ref[NUM_BUCKETS - 1, h]
    dist = cols - rows + TQ
    n = jnp.maximum(dist, 0)
    nf = jnp.maximum(n, 1).astype(F32)
    large = MAX_EXACT + (jnp.log(nf / MAX_EXACT) / math.log(MAX_DISTANCE / MAX_EXACT)
                         * (NUM_BUCKETS - MAX_EXACT)).astype(jnp.int32)
    large = jnp.minimum(large, NUM_BUCKETS - 1)
    bucket = jnp.where(n < MAX_EXACT, n, large)
    bias = jnp.zeros((2 * TQ, TQ), F32)
    for b in range(NUM_BUCKETS):
        bias = jnp.where(bucket == b, rb_ref[b, h], bias)
    bias_ref[...] = jnp.where(dist >= 0, (bias - far) * LOG2E, NEG_INF)
    s1 = jnp.sum(lq1_ref[...] * lk1_ref[...], axis=1, keepdims=True)
    s2 = jnp.sum(lq2_ref[...] * lk2_ref[...], axis=1, keepdims=True)
    lam = jnp.exp(s1) - jnp.exp(s2) + LAMBDA_INIT
    lam_ref[...] = jnp.broadcast_to(lam, lam_ref.shape)


def _attn_tables(rel_bias, lq1, lk1, lq2, lk2):
    vec = pl.BlockSpec((1, HEAD_DIM), lambda h: (0, 0))
    return pl.pallas_call(
        _attn_tables_kernel,
        grid=(ATTN_HEADS,),
        in_specs=[pl.BlockSpec(memory_space=pltpu.SMEM), vec, vec, vec, vec],
        out_specs=[pl.BlockSpec((None, 2 * TQ, TQ), lambda h: (h, 0, 0)),
                   pl.BlockSpec((8, TQ), lambda h: (0, 0))],
        out_shape=[jax.ShapeDtypeStruct((ATTN_HEADS, 2 * TQ, TQ), F32),
                   jax.ShapeDtypeStruct((8, TQ), F32)],
        compiler_params=_params("arbitrary"),
        name="attn_tables",
    )(rel_bias, lq1, lk1, lq2, lk2)


def _attn_scores(qs_ref, k_ref, j):
    kb = k_ref[pl.ds(pl.multiple_of(j * TQ, TQ), TQ), :]
    return lax.dot_general(kb, qs_ref[...], (((1,), (1,)), ((), ())), preferred_element_type=F32)


def _attn_update(vt_ref, m_ref, l_ref, acc_ref, st, j, bias):
    if bias is not None:
        st = st + jnp.concatenate([bias, bias], axis=1)
    m_prev = m_ref[...]
    m_new = jnp.maximum(m_prev, jnp.max(st, axis=0, keepdims=True))
    alpha = jnp.exp2(m_prev - m_new)
    p = jnp.exp2(st - m_new)
    l_ref[...] = alpha * l_ref[...] + jnp.sum(p, axis=0, keepdims=True)
    vt = vt_ref[:, pl.ds(pl.multiple_of(j * TQ, TQ), TQ)]
    acc_ref[...] = alpha * acc_ref[...] + jnp.dot(vt, p.astype(BF16), preferred_element_type=F32)
    m_ref[...] = m_new


def _diff_attn_kernel(q_ref, k_ref, v_ref, bias_ref, lam_ref, sg_ref, o_ref,
                      qs_ref, vt_ref, s_ref, m_ref, l_ref, acc_ref, *, seq):
    qi = pl.program_id(2)

    @pl.when(qi == 0)
    def _():
        step = 2 * TQ
        for c in range(seq // step):
            vt_ref[:, c * step:(c + 1) * step] = v_ref[c * step:(c + 1) * step, :].astype(F32).T.astype(BF16)

    q = q_ref[...]
    lane = lax.broadcasted_iota(jnp.int32, q.shape, 1)
    zero = jnp.zeros_like(q)
    qs_ref[0:TQ, :] = jnp.where(lane < HEAD_DIM, q, zero)
    qs_ref[TQ:2 * TQ, :] = jnp.where(lane >= HEAD_DIM, q, zero)
    m_ref[...] = jnp.full(m_ref.shape, NEG_INF, F32)
    l_ref[...] = jnp.zeros(l_ref.shape, F32)
    acc_ref[...] = jnp.zeros(acc_ref.shape, F32)
    scores = functools.partial(_attn_scores, qs_ref, k_ref)
    update = functools.partial(_attn_update, vt_ref, m_ref, l_ref, acc_ref)

    sa_ref, sb_ref = s_ref.at[0], s_ref.at[1]
    n_far = jnp.maximum(qi - 1, 0)
    sa_ref[...] = scores(0)

    def far_pair(p, carry):
        j = 2 * p
        sb_ref[...] = scores(j + 1)
        update(sa_ref[...], j, None)
        sa_ref[...] = scores(j + 2)
        update(sb_ref[...], j + 1, None)
        return carry

    lax.fori_loop(0, n_far // 2, far_pair, 0)

    @pl.when(n_far % 2 == 1)
    def _():
        update(sa_ref[...], n_far - 1, None)
        sa_ref[...] = scores(n_far)

    @pl.when(qi >= 1)
    def _():
        sb_ref[...] = scores(qi)
        update(sa_ref[...], qi - 1, bias_ref[0:TQ, :])
        update(sb_ref[...], qi, bias_ref[TQ:2 * TQ, :])

    @pl.when(qi == 0)
    def _():
        update(sa_ref[...], qi, bias_ref[TQ:2 * TQ, :])

    ot = acc_ref[...] / l_ref[...]
    dt = ot[:, 0:TQ] - lam_ref[0:1, :] * ot[:, TQ:2 * TQ]
    yt = dt * lax.rsqrt(jnp.mean(dt * dt, axis=0, keepdims=True) + NORM_EPS)
    o_ref[...] = (yt.T * sg_ref[...] * (1.0 - LAMBDA_INIT)).astype(o_ref.dtype)


def _diff_attn(qkv, bias_tiles, lam, subln_g, bsz, seq):
    nq = seq // TQ
    h_blocks = D_ATTN // ATTN_V_DIM
    return pl.pallas_call(
        functools.partial(_diff_attn_kernel, seq=seq),
        grid=(bsz, ATTN_HEADS, nq),
        in_specs=[pl.BlockSpec((TQ, ATTN_V_DIM), lambda b, h, i: (b * nq + i, h)),
                  pl.BlockSpec((seq, ATTN_V_DIM), lambda b, h, i: (b, h_blocks + h)),
                  pl.BlockSpec((seq, ATTN_V_DIM), lambda b, h, i: (b, 2 * h_blocks + h)),
                  pl.BlockSpec((None, 2 * TQ, TQ), lambda b, h, i: (h, 0, 0)),
                  pl.BlockSpec((8, TQ), lambda b, h, i: (0, 0)),
                  pl.BlockSpec((1, ATTN_V_DIM), lambda b, h, i: (0, 0))],
        out_specs=pl.BlockSpec((TQ, ATTN_V_DIM), lambda b, h, i: (b * nq + i, h)),
        out_shape=jax.ShapeDtypeStruct((bsz * seq, D_ATTN), BF16),
        scratch_shapes=[pltpu.VMEM((2 * TQ, ATTN_V_DIM), BF16),
                        pltpu.VMEM((ATTN_V_DIM, seq), BF16),
                        pltpu.VMEM((2, TQ, 2 * TQ), F32),
                        pltpu.VMEM((1, 2 * TQ), F32),
                        pltpu.VMEM((1, 2 * TQ), F32),
                        pltpu.VMEM((ATTN_V_DIM, 2 * TQ), F32)],
        compiler_params=_params("arbitrary", "arbitrary", "arbitrary"),
        name="diff_attn",
    )(qkv, qkv, qkv, bias_tiles, lam, subln_g)


P_MU_R, P_MU_K, P_MU_V, P_W0, P_A0, P_KK, P_KA, P_RK, P_LNG, P_LNB = range(10)
P_ROWS = 16


def _rwkv_chunk(r, wlog, cum, k, v, an, b, st_ref):
    groups = range(N_GROUPS)
    heads = GROUP // HEAD_DIM
    grp = lambda x, g: x[:, g * GROUP:(g + 1) * GROUP]

    rows_bd = lax.broadcasted_iota(jnp.int32, (GROUP, GROUP), 0) // HEAD_DIM
    lanes_bd = lax.broadcasted_iota(jnp.int32, (GROUP, GROUP), 1) // HEAD_DIM
    bd_mask = rows_bd == lanes_bd
    t_idx = lax.broadcasted_iota(jnp.int32, (CHUNK, GROUP), 0)
    s_idx = lax.broadcasted_iota(jnp.int32, (CHUNK, GROUP), 1) % HEAD_DIM
    m_strict = s_idx < t_idx
    m_incl = s_idx <= t_idx
    eye = jnp.where(s_idx == t_idx, 1.0, 0.0).astype(F32)

    def bd(x):
        return jnp.where(bd_mask, jnp.concatenate([x] * heads, axis=0), 0.0)

    tot = cum[CHUNK - 1:CHUNK, :]
    e_neg = jnp.exp(-cum)
    e_rem = jnp.exp(tot - cum)
    decay = jnp.exp(tot)
    at = jnp.exp(cum - wlog) * an
    rt = jnp.exp(cum) * r
    bt, kt = e_neg * b, e_neg * k
    b_rem, k_rem = e_rem * b, e_rem * k

    aa = [_mm_nt(jnp.concatenate([grp(at, g), grp(rt, g)], axis=0),
                 jnp.concatenate([bd(grp(bt, g)), bd(grp(kt, g))], axis=0)) for g in groups]
    a_ab = [jnp.where(m_strict, aa[g][:CHUNK, :GROUP], 0.0) for g in groups]
    a_ak = [jnp.where(m_strict, aa[g][:CHUNK, GROUP:], 0.0) for g in groups]
    a_rb = [jnp.where(m_incl, aa[g][CHUNK:, :GROUP], 0.0) for g in groups]
    a_rk = [jnp.where(m_incl, aa[g][CHUNK:, GROUP:], 0.0) for g in groups]

    minv = [eye + a_ab[g] for g in groups]
    nk = [_mm(a_ab[g], bd(a_ab[g])) for g in groups]
    for _ in range(int(math.log2(CHUNK)) - 2):
        res = [_mm(jnp.concatenate([nk[g], minv[g]], axis=0), bd(nk[g])) for g in groups]
        nk = [res[g][:CHUNK] for g in groups]
        minv = [minv[g] + res[g][CHUNK:] for g in groups]
    minv = [minv[g] + _mm(minv[g], bd(nk[g])) for g in groups]

    bd_v = [bd(grp(v, g)) for g in groups]
    x1 = [_mm(a_ak[g], bd_v[g]) for g in groups]
    ma = [_mm(minv[g], jnp.concatenate([bd(grp(at, g)), bd(x1[g])], axis=1)) for g in groups]

    st = [st_ref[g] for g in groups]
    c1 = [_mm_nt(jnp.concatenate([ma[g][:, :GROUP], grp(rt, g)], axis=0), bd(st[g])) for g in groups]
    u = [c1[g][:CHUNK] + ma[g][:, GROUP:] for g in groups]
    uv = [jnp.concatenate([bd(u[g]), bd_v[g]], axis=0) for g in groups]
    y = [c1[g][CHUNK:] + _mm(jnp.concatenate([a_rb[g], a_rk[g]], axis=1), uv[g]) for g in groups]
    upd = [_mm_tn(uv[g], jnp.concatenate([bd(grp(b_rem, g)), bd(grp(k_rem, g))], axis=0))
           for g in groups]
    for g in groups:
        folded = sum(upd[g][i * HEAD_DIM:(i + 1) * HEAD_DIM] for i in range(heads))
        st_ref[g] = st[g] * grp(decay, g) + folded
    return jnp.concatenate(y, axis=1)


def _rwkv_kernel(rp_ref, kp_ref, vp_ref, lo_ref, pv_ref, w2_ref, a2_ref, g2_ref, e_ref, tri_ref,
                 o_ref, st_ref, prev_ref):
    c = pl.program_id(1)

    @pl.when(c == 0)
    def _():
        st_ref[...] = jnp.zeros(st_ref.shape, F32)
        prev_ref[...] = jnp.zeros(prev_ref.shape, F32)

    row0 = lax.broadcasted_iota(jnp.int32, (CHUNK, D_RWKV), 0) == 0

    def shifted(x, col):
        prev = prev_ref[0:1, col * D_RWKV:(col + 1) * D_RWKV]
        return jnp.where(row0, prev, pltpu.roll(x, 1, axis=0))

    pv = lambda i: pv_ref[i:i + 1, :]
    rp, kp, vp, lo = rp_ref[...], kp_ref[...], vp_ref[...], lo_ref[...]
    lo_prev = shifted(lo, 3)
    r = rp + (shifted(rp, 0) - rp) * pv(P_MU_R)
    k = kp + (shifted(kp, 1) - kp) * pv(P_MU_K)
    v = vp + (shifted(vp, 2) - vp) * pv(P_MU_V)
    for col, ref in enumerate((rp_ref, kp_ref, vp_ref, lo_ref)):
        prev_ref[0:1, col * D_RWKV:(col + 1) * D_RWKV] = ref[CHUNK - 1:CHUNK, :]

    lora = lo[:, :LORA_COLS] + lo_prev[:, LORA_COLS:]
    w_in = pv(P_W0) + _mm(jnp.tanh(lora[:, :LORA_W]), w2_ref[...])
    w = -(jnp.maximum(-w_in, 0.0) + jnp.log(1.0 + jnp.exp(-jnp.abs(w_in)))) - 0.5
    wlog = -jnp.exp(w)
    a = 1.0 / (1.0 + jnp.exp(-(pv(P_A0) + _mm(lora[:, LORA_W:2 * LORA_W], a2_ref[...]))))
    g = _mm(1.0 / (1.0 + jnp.exp(-lora[:, 2 * LORA_W:])), g2_ref[...])

    def head_sum(x):
        return jnp.concatenate([_mm(x[:, i * GROUP:(i + 1) * GROUP], e_ref[...])
                                for i in range(N_GROUPS)], axis=1)

    kk = k * pv(P_KK)
    kk = kk / jnp.maximum(jnp.sqrt(head_sum(kk * kk)), 1e-12)
    k2 = k * (1.0 + (a - 1.0) * pv(P_KA))
    cum = jnp.dot(tri_ref[...], wlog, preferred_element_type=F32, precision=lax.Precision.HIGHEST)

    y = _rwkv_chunk(r, wlog, cum, k2, v, -kk, kk * a, st_ref)

    mu = head_sum(y) * (1.0 / HEAD_DIM)
    d = y - mu
    var = head_sum(d * d) * (1.0 / HEAD_DIM)
    yn = d * lax.rsqrt(var + RWKV_GN_EPS) * pv(P_LNG) + pv(P_LNB)
    bonus = head_sum(r * k2 * pv(P_RK)) * v
    o_ref[...] = ((yn + bonus) * g).astype(o_ref.dtype)


def _rwkv(pr, pvec, w2p, a2p, g2p, e256, tri, bsz, seq):
    nc = seq // CHUNK
    col = lambda cb: pl.BlockSpec((CHUNK, D_RWKV), lambda b, c: (b * nc + c, cb))
    full = lambda shape: pl.BlockSpec(shape, lambda b, c: (0, 0))
    return pl.pallas_call(
        _rwkv_kernel,
        grid=(bsz, nc),
        in_specs=[col(0), col(1), col(2), col(3),
                  full((P_ROWS, D_RWKV)), full((LORA_W, D_RWKV)), full((LORA_W, D_RWKV)),
                  full((LORA_G, D_RWKV)), full((GROUP, GROUP)), full((CHUNK, CHUNK))],
        out_specs=pl.BlockSpec((CHUNK, D_RWKV), lambda b, c: (b * nc + c, 0)),
        out_shape=jax.ShapeDtypeStruct((bsz * seq, D_RWKV), BF16),
        scratch_shapes=[pltpu.VMEM((N_GROUPS, HEAD_DIM, GROUP), F32),
                        pltpu.VMEM((8, 4 * D_RWKV), F32)],
        compiler_params=_params("arbitrary", "arbitrary"),
        name="rwkv7",
    )(pr, pr, pr, pr, pvec, w2p, a2p, g2p, e256, tri)


def _out_proj_kernel(oa_ref, or_ref, wa_ref, wr_ref, x_ref, gate_ref, o_ref):
    acc = jnp.dot(oa_ref[...], wa_ref[...], preferred_element_type=F32)
    acc = acc + jnp.dot(or_ref[...], wr_ref[...], preferred_element_type=F32)
    o_ref[...] = x_ref[...] + gate_ref[...] * acc


def _out_proj(o_attn, o_rwkv, w_bf, x2, mod3, seq):
    n = x2.shape[0]
    tiles_per_batch = seq // TM
    gate_col = 2 * (D_MODEL // TN)
    return pl.pallas_call(
        _out_proj_kernel,
        grid=(n // TM, D_MODEL // TN),
        in_specs=[pl.BlockSpec((TM, D_ATTN), lambda i, j: (i, 0)),
                  pl.BlockSpec((TM, D_RWKV), lambda i, j: (i, 0)),
                  pl.BlockSpec((D_ATTN, TN), lambda i, j: (0, j)),
                  pl.BlockSpec((D_RWKV, TN), lambda i, j: (1, j)),
                  pl.BlockSpec((TM, TN), lambda i, j: (i, j)),
                  pl.BlockSpec((None, 1, TN), lambda i, j: (i // tiles_per_batch, 0, gate_col + j))],
        out_specs=pl.BlockSpec((TM, TN), lambda i, j: (i, j)),
        out_shape=jax.ShapeDtypeStruct((n, D_MODEL), F32),
        compiler_params=_params("arbitrary", "arbitrary"),
        name="out_proj",
    )(o_attn, o_rwkv, w_bf, w_bf, x2, mod3)


def _ffn_up_kernel(x_ref, xh_ref, g_ref, sc_ref, sh_ref, wg_ref, wv_ref, cwg_ref, cwv_ref,
                   cbg_ref, cbv_ref, o_ref, h_ref, *, tiles_per_batch):
    i = pl.program_id(0)
    j = pl.program_id(1)

    @pl.when(j == 0)
    def _():
        g, sc, sh = g_ref[...], sc_ref[...], sh_ref[...]
        h_ref[0:HALO, :] = _modulated_norm(xh_ref[...], g, sc, sh).astype(BF16)
        h_ref[HALO:, :] = _modulated_norm(x_ref[...], g, sc, sh).astype(BF16)

    first = (i % tiles_per_batch) == 0
    halo_rows = lax.broadcasted_iota(jnp.int32, (TM + HALO, TN), 0) < HALO
    kill = jnp.logical_and(first, halo_rows)

    def conv(w_ref, cw_ref, cb_ref):
        up = jnp.dot(h_ref[...], w_ref[...], preferred_element_type=F32)
        up = jnp.where(kill, 0.0, up)
        y = (cb_ref[...] + cw_ref[0:1, :] * pltpu.roll(up, 2, axis=0)
             + cw_ref[1:2, :] * pltpu.roll(up, 1, axis=0) + cw_ref[2:3, :] * up)
        return y[HALO:, :]

    gate = conv(wg_ref, cwg_ref, cbg_ref)
    val = conv(wv_ref, cwv_ref, cbv_ref)
    o_ref[...] = (gate * (1.0 / (1.0 + jnp.exp(-gate))) * val).astype(o_ref.dtype)


def _ffn_up(x1, norm_g, mod3, w_bf, conv_w8, conv_b, seq):
    n = x1.shape[0]
    tiles_per_batch = seq // TM
    nf = D_FF // TN
    batch = lambda i, j: i // tiles_per_batch
    return pl.pallas_call(
        functools.partial(_ffn_up_kernel, tiles_per_batch=tiles_per_batch),
        grid=(n // TM, nf),
        in_specs=[pl.BlockSpec((TM, D_MODEL), lambda i, j: (i, 0)),
                  pl.BlockSpec((HALO, D_MODEL), lambda i, j: (jnp.maximum(i * (TM // HALO) - 1, 0), 0)),
                  pl.BlockSpec((1, D_MODEL), lambda i, j: (0, 0)),
                  pl.BlockSpec((None, 1, D_MODEL), lambda i, j: (batch(i, j), 0, 4)),
                  pl.BlockSpec((None, 1, D_MODEL), lambda i, j: (batch(i, j), 0, 3)),
                  pl.BlockSpec((D_MODEL, TN), lambda i, j: (0, j)),
                  pl.BlockSpec((D_MODEL, TN), lambda i, j: (0, nf + j)),
                  pl.BlockSpec((8, TN), lambda i, j: (0, j)),
                  pl.BlockSpec((8, TN), lambda i, j: (0, nf + j)),
                  pl.BlockSpec((1, TN), lambda i, j: (0, j)),
                  pl.BlockSpec((1, TN), lambda i, j: (0, nf + j))],
        out_specs=pl.BlockSpec((TM, TN), lambda i, j: (i, j)),
        out_shape=jax.ShapeDtypeStruct((n, D_FF), BF16),
        scratch_shapes=[pltpu.VMEM((TM + HALO, D_MODEL), BF16)],
        compiler_params=_params("arbitrary", "arbitrary"),
        name="ffn_up",
    )(x1, x1, norm_g, mod3, mod3, w_bf, w_bf, conv_w8, conv_w8, conv_b, conv_b)


def _ffn_down_kernel(a_ref, w_ref, x_ref, gate_ref, o_ref):
    acc = jnp.dot(a_ref[...], w_ref[...], preferred_element_type=F32)
    o_ref[...] = x_ref[...] + gate_ref[...] * acc


def _ffn_down(act, w_bf, x1, mod3, seq):
    n = x1.shape[0]
    tiles_per_batch = seq // TM_DOWN
    gate_col = 5 * (D_MODEL // TN)
    return pl.pallas_call(
        _ffn_down_kernel,
        grid=(n // TM_DOWN, D_MODEL // TN),
        in_specs=[pl.BlockSpec((TM_DOWN, D_FF), lambda i, j: (i, 0)),
                  pl.BlockSpec((D_FF, TN), lambda i, j: (0, j)),
                  pl.BlockSpec((TM_DOWN, TN), lambda i, j: (i, j)),
                  pl.BlockSpec((None, 1, TN), lambda i, j: (i // tiles_per_batch, 0, gate_col + j))],
        out_specs=pl.BlockSpec((TM_DOWN, TN), lambda i, j: (i, j)),
        out_shape=jax.ShapeDtypeStruct((n, D_MODEL), F32),
        compiler_params=_params("arbitrary", "arbitrary"),
        name="ffn_down",
    )(act, w_bf, x1, mod3)


def _pad_cols(w, width):
    return jnp.pad(w, ((0, 0), (0, width - w.shape[1])))


def _pad_rows(w, height):
    return jnp.pad(w, ((0, height - w.shape[0]), (0, 0)))


def kernel(x, c, rel_bias, w_ada, b_ada, norm_mix_g, w_in, q_norm_g, k_norm_g, lambda_q1, lambda_k1,
           lambda_q2, lambda_k2, attn_subln_g, mu_rkv, mu_wag, w0, w1, w2, a0, a1, a2, g1, g2, k_k,
           k_a, r_k, ln_x_g, ln_x_b, w_out, norm_ffn_g, w_up, conv_w, conv_b, w_down):
    bsz, seq, _ = x.shape
    n = bsz * seq
    x2 = x.reshape(n, D_MODEL)

    mod = _adaln(c, w_ada[0], b_ada[0])
    mod3 = mod.reshape(bsz, 1, 6 * D_MODEL)

    idx = jnp.arange(GROUP) // HEAD_DIM
    e256 = (idx[:, None] == idx[None, :]).astype(BF16)
    tri = (jnp.arange(CHUNK)[:, None] >= jnp.arange(CHUNK)[None, :]).astype(F32)

    w_in0 = w_in[0]
    gain = jnp.concatenate([jnp.tile(q_norm_g[0], D_ATTN // HEAD_DIM) * (HEAD_DIM ** -0.5 * LOG2E),
                            jnp.tile(k_norm_g[0], D_ATTN // HEAD_DIM),
                            jnp.ones((D_ATTN,), F32)]).reshape(1, 3 * D_ATTN)
    qkv = _proj_attn(x2, norm_mix_g, mod3, w_in0[:, :3 * D_ATTN].astype(BF16), gain, e256, seq)

    mu = mu_wag[0]
    lora_w = (_pad_cols(w1[0], LORA_W), _pad_cols(a1[0], LORA_W), _pad_cols(g1[0], LORA_G))
    lora_cur = jnp.concatenate([(1.0 - mu[i])[:, None] * w for i, w in enumerate(lora_w)], axis=1)
    lora_prev = jnp.concatenate([mu[i][:, None] * w for i, w in enumerate(lora_w)], axis=1)
    w_rwkv = jnp.concatenate([w_in0[:, 3 * D_ATTN:], lora_cur, lora_prev], axis=1).astype(BF16)
    pr = _proj_rwkv(x2, norm_mix_g, mod3, w_rwkv, seq)

    bias_tiles, lam = _attn_tables(rel_bias, lambda_q1, lambda_k1, lambda_q2, lambda_k2)
    o_attn = _diff_attn(qkv, bias_tiles, lam, attn_subln_g, bsz, seq)

    pvec = jnp.concatenate([mu_rkv[0], w0, a0, k_k, k_a, r_k.reshape(1, D_RWKV), ln_x_g, ln_x_b,
                            jnp.zeros((P_ROWS - 10, D_RWKV), F32)], axis=0)
    o_rwkv = _rwkv(pr, pvec, _pad_rows(w2[0], LORA_W).astype(BF16), _pad_rows(a2[0], LORA_W).astype(BF16),
                   _pad_rows(g2[0], LORA_G).astype(BF16), e256, tri, bsz, seq)

    x1 = _out_proj(o_attn, o_rwkv, w_out[0].astype(BF16), x2, mod3, seq)

    act = _ffn_up(x1, norm_ffn_g, mod3, w_up[0].astype(BF16), _pad_rows(conv_w[0], 8), conv_b, seq)
    out = _ffn_down(act, w_down[0].astype(BF16), x1, mod3, seq)
    return out.reshape(bsz, seq, D_MODEL)
```

```python
import functools
import math

import jax
import jax.numpy as jnp
from jax import lax
from jax.experimental import pallas as pl
from jax.experimental.pallas import tpu as pltpu

F32 = jnp.float32
BF16 = jnp.bfloat16

D_MODEL = 2048
D_ATTN = 1024
D_RWKV = 1024
HEAD_DIM = 64
ATTN_HEADS = 8
ATTN_V_DIM = 128
RWKV_HEADS = 16
D_FF = 5632
NUM_BUCKETS = 32
MAX_EXACT = 16
MAX_DISTANCE = 128
NORM_EPS = 1e-6
RWKV_GN_EPS = 64e-5
NEG_INF = -1e30
LOG2E = math.log2(math.e)
LAMBDA_INIT = 0.8 - 0.6 * math.exp(0.0)

LANES = 128
LORA_W = 128
LORA_G = 256
LORA_COLS = 2 * LORA_W + LORA_G

TM = 1024
TN = 512
TM_DOWN = 512
TM_OUT = 512
FFN_PARTS = 1
TQ = 256
HEAD_PAIR = 2
V_ROWS = ATTN_V_DIM + 16
CHUNK = 64
GROUP = 256
N_GROUPS = D_RWKV // GROUP

VMEM_LIMIT = 56 * 1024 * 1024


def _mm(a, b):
    return jnp.dot(a.astype(BF16), b.astype(BF16), preferred_element_type=F32)


def _mm_nt(a, b):
    return lax.dot_general(a.astype(BF16), b.astype(BF16), (((1,), (1,)), ((), ())),
                           preferred_element_type=F32)


def _mm_tn(a, b):
    return lax.dot_general(a.astype(BF16), b.astype(BF16), (((0,), (0,)), ((), ())),
                           preferred_element_type=F32)


def _params(*sem):
    return pltpu.CompilerParams(dimension_semantics=sem, vmem_limit_bytes=VMEM_LIMIT)


def _adaln_kernel(c_ref, w_ref, b_ref, o_ref):
    c = c_ref[...]
    c_act = c * (1.0 / (1.0 + jnp.exp(-c)))
    acc = jnp.dot(c_act, w_ref[...], preferred_element_type=F32, precision=lax.Precision.HIGHEST)
    o_ref[...] = acc + b_ref[...]


def _adaln(c, w_ada, b_ada):
    bsz = c.shape[0]
    n_out = w_ada.shape[1]
    tn = 1024
    return pl.pallas_call(
        _adaln_kernel,
        grid=(n_out // tn,),
        in_specs=[pl.BlockSpec((bsz, D_MODEL), lambda j: (0, 0)),
                  pl.BlockSpec((D_MODEL, tn), lambda j: (0, j)),
                  pl.BlockSpec((1, tn), lambda j: (0, j))],
        out_specs=pl.BlockSpec((bsz, tn), lambda j: (0, j)),
        out_shape=jax.ShapeDtypeStruct((bsz, n_out), F32),
        compiler_params=_params("arbitrary"),
        name="adaln",
    )(c, w_ada, b_ada.reshape(1, n_out))


def _modulated_norm(x, g, scale, shift):
    y = x * lax.rsqrt(jnp.mean(x * x, axis=-1, keepdims=True) + NORM_EPS)
    return (y * g) * (1.0 + scale) + shift


QK_TILES = 2 * D_ATTN // TN
ATTN_TILES = 3 * D_ATTN // TN
W_IN_TILES = (3 * D_ATTN + 3 * D_RWKV) // TN
LORA_TILES = 2 * LORA_COLS // TN


def _proj_kernel(x_ref, g_ref, sc_ref, sh_ref, w_ref, wl_ref, gain_ref, e_ref, qkv_ref, pr_ref, h_ref):
    j = pl.program_id(1)

    @pl.when(j == 0)
    def _():
        h_ref[...] = _modulated_norm(x_ref[...], g_ref[...], sc_ref[...], sh_ref[...]).astype(BF16)

    def project(weights_ref):
        return jnp.dot(h_ref[...], weights_ref[...].astype(BF16), preferred_element_type=F32)

    @pl.when(j < QK_TILES)
    def _():
        acc = project(w_ref)
        half = TN // 2
        for s in range(2):
            a = acc[:, s * half:(s + 1) * half]
            ss = _mm(a * a, e_ref[...])
            y = a * lax.rsqrt(ss * (1.0 / HEAD_DIM) + NORM_EPS) * gain_ref[:, s * half:(s + 1) * half]
            qkv_ref[:, s * half:(s + 1) * half] = y.astype(qkv_ref.dtype)

    @pl.when(jnp.logical_and(j >= QK_TILES, j < ATTN_TILES))
    def _():
        qkv_ref[...] = project(w_ref).astype(qkv_ref.dtype)

    @pl.when(jnp.logical_and(j >= ATTN_TILES, j < W_IN_TILES))
    def _():
        pr_ref[...] = project(w_ref)

    @pl.when(j >= W_IN_TILES)
    def _():
        pr_ref[...] = project(wl_ref)


def _proj(x2, norm_g, mod3, w_in, w_lora, gain, e256, seq):
    n = x2.shape[0]
    tiles_per_batch = seq // TM
    batch = lambda i, j: i // tiles_per_batch
    return pl.pallas_call(
        _proj_kernel,
        grid=(n // TM, W_IN_TILES + LORA_TILES),
        in_specs=[pl.BlockSpec((TM, D_MODEL), lambda i, j: (i, 0)),
                  pl.BlockSpec((1, D_MODEL), lambda i, j: (0, 0)),
                  pl.BlockSpec((None, 1, D_MODEL), lambda i, j: (batch(i, j), 0, 1)),
                  pl.BlockSpec((None, 1, D_MODEL), lambda i, j: (batch(i, j), 0, 0)),
                  pl.BlockSpec((D_MODEL, TN), lambda i, j: (0, jnp.minimum(j, W_IN_TILES - 1))),
                  pl.BlockSpec((D_MODEL, TN), lambda i, j: (0, jnp.maximum(j - W_IN_TILES, 0))),
                  pl.BlockSpec((1, TN), lambda i, j: (0, jnp.minimum(j, QK_TILES - 1))),
                  pl.BlockSpec((TN // 2, TN // 2), lambda i, j: (0, 0))],
        out_specs=[pl.BlockSpec((TM, TN), lambda i, j: (i, jnp.minimum(j, ATTN_TILES - 1))),
                   pl.BlockSpec((TM, TN), lambda i, j: (i, jnp.maximum(j - ATTN_TILES, 0)))],
        out_shape=[jax.ShapeDtypeStruct((n, 3 * D_ATTN), BF16),
                   jax.ShapeDtypeStruct((n, 3 * D_RWKV + 2 * LORA_COLS), F32)],
        scratch_shapes=[pltpu.VMEM((TM, D_MODEL), BF16)],
        compiler_params=_params("arbitrary", "arbitrary"),
        name="proj",
    )(x2, norm_g, mod3, mod3, w_in, w_lora, gain, e256)


def _attn_tables_kernel(rb_ref, lq1_ref, lk1_ref, lq2_ref, lk2_ref, bias_ref, lam_ref):
    h = pl.program_id(0)
    rows = lax.broadcasted_iota(jnp.int32, (2 * TQ, TQ), 0)
    cols = lax.broadcasted_iota(jnp.int32, (2 * TQ, TQ), 1)
    far = rb_ref[NUM_BUCKETS - 1, h]
    dist = cols - rows + TQ
    n = jnp.maximum(dist, 0)
    nf = jnp.maximum(n, 1).astype(F32)
    large = MAX_EXACT + (jnp.log(nf / MAX_EXACT) / math.log(MAX_DISTANCE / MAX_EXACT)
                         * (NUM_BUCKETS - MAX_EXACT)).astype(jnp.int32)
    large = jnp.minimum(large, NUM_BUCKETS - 1)
    bucket = jnp.where(n < MAX_EXACT, n, large)
    bias = jnp.zeros((2 * TQ, TQ), F32)
    for b in range(NUM_BUCKETS):
        bias = jnp.where(bucket == b, rb_ref[b, h], bias)
    bias_ref[...] = jnp.where(dist >= 0, (bias - far) * LOG2E, NEG_INF)
    s1 = jnp.sum(lq1_ref[...] * lk1_ref[...], axis=1, keepdims=True)
    s2 = jnp.sum(lq2_ref[...] * lk2_ref[...], axis=1, keepdims=True)
    lam = jnp.exp(s1) - jnp.exp(s2) + LAMBDA_INIT
    lam_ref[...] = jnp.broadcast_to(lam, lam_ref.shape)


def _attn_tables(rel_bias, lq1, lk1, lq2, lk2):
    vec = pl.BlockSpec((1, HEAD_DIM), lambda h: (0, 0))
    return pl.pallas_call(
        _attn_tables_kernel,
        grid=(ATTN_HEADS,),
        in_specs=[pl.BlockSpec(memory_space=pltpu.SMEM), vec, vec, vec, vec],
        out_specs=[pl.BlockSpec((None, 2 * TQ, TQ), lambda h: (h, 0, 0)),
                   pl.BlockSpec((8, TQ), lambda h: (0, 0))],
        out_shape=[jax.ShapeDtypeStruct((ATTN_HEADS, 2 * TQ, TQ), F32),
                   jax.ShapeDtypeStruct((8, TQ), F32)],
        compiler_params=_params("arbitrary"),
        name="attn_tables",
    )(rel_bias, lq1, lk1, lq2, lk2)


def _attn_scores(qs_ref, k_ref, j):
    start = pl.multiple_of(j * TQ, TQ)
    return [lax.dot_general(k_ref[pl.ds(start, TQ), hh * ATTN_V_DIM:(hh + 1) * ATTN_V_DIM], qs_ref[hh],
                            (((1,), (1,)), ((), ())), preferred_element_type=F32)
            for hh in range(HEAD_PAIR)]


def _attn_update(vt_ref, m_ref, acc_ref, sts, j, biases):
    heads = range(HEAD_PAIR)
    if biases is not None:
        sts = [sts[hh] + jnp.concatenate([biases[hh], biases[hh]], axis=1) for hh in heads]
    m_prev = [m_ref[hh] for hh in heads]
    m_new = [jnp.maximum(m_prev[hh], jnp.max(sts[hh], axis=0, keepdims=True)) for hh in heads]
    alpha = [jnp.exp2(m_prev[hh] - m_new[hh]) for hh in heads]
    p = [jnp.exp2(sts[hh] - m_new[hh]) for hh in heads]
    start = pl.multiple_of(j * TQ, TQ)
    pv = [jnp.dot(vt_ref[hh, :, pl.ds(start, TQ)], p[hh].astype(BF16), preferred_element_type=F32)
          for hh in heads]
    for hh in heads:
        acc_ref[hh] = alpha[hh] * acc_ref[hh] + pv[hh]
        m_ref[hh] = m_new[hh]


def _diff_attn_kernel(q_ref, k_ref, v_ref, bias_ref, lam_ref, sg_ref, o_ref,
                      qs_ref, vt_ref, s_ref, m_ref, acc_ref, *, seq):
    qi = pl.program_id(2)
    heads = range(HEAD_PAIR)

    @pl.when(qi == 0)
    def _():
        step = 2 * TQ
        for hh in heads:
            for c in range(seq // step):
                v = v_ref[c * step:(c + 1) * step, hh * ATTN_V_DIM:(hh + 1) * ATTN_V_DIM]
                vt_ref[hh, 0:ATTN_V_DIM, c * step:(c + 1) * step] = v.astype(F32).T.astype(BF16)
            ones_row = lax.broadcasted_iota(jnp.int32, (V_ROWS - ATTN_V_DIM, seq), 0) == 0
            vt_ref[hh, ATTN_V_DIM:V_ROWS, :] = jnp.where(ones_row, 1.0, 0.0).astype(BF16)

    lane = lax.broadcasted_iota(jnp.int32, (TQ, ATTN_V_DIM), 1)
    for hh in heads:
        q = q_ref[:, hh * ATTN_V_DIM:(hh + 1) * ATTN_V_DIM]
        zero = jnp.zeros_like(q)
        qs_ref[hh, 0:TQ, :] = jnp.where(lane < HEAD_DIM, q, zero)
        qs_ref[hh, TQ:2 * TQ, :] = jnp.where(lane >= HEAD_DIM, q, zero)
    m_ref[...] = jnp.full(m_ref.shape, NEG_INF, F32)
    acc_ref[...] = jnp.zeros(acc_ref.shape, F32)
    scores = functools.partial(_attn_scores, qs_ref, k_ref)
    update = functools.partial(_attn_update, vt_ref, m_ref, acc_ref)

    def put(buf, sts):
        for hh in heads:
            s_ref[buf, hh] = sts[hh]

    get = lambda buf: [s_ref[buf, hh] for hh in heads]
    bias_prev = lambda: [bias_ref[hh, 0:TQ, :] for hh in heads]
    bias_diag = lambda: [bias_ref[hh, TQ:2 * TQ, :] for hh in heads]

    n_far = jnp.maximum(qi - 1, 0)
    put(0, scores(0))

    def far_pair(p, carry):
        j = 2 * p
        put(1, scores(j + 1))
        update(get(0), j, None)
        put(0, scores(j + 2))
        update(get(1), j + 1, None)
        return carry

    lax.fori_loop(0, n_far // 2, far_pair, 0)

    @pl.when(n_far % 2 == 1)
    def _():
        update(get(0), n_far - 1, None)
        put(0, scores(n_far))

    @pl.when(qi >= 1)
    def _():
        put(1, scores(qi))
        update(get(0), qi - 1, bias_prev())
        update(get(1), qi, bias_diag())

    @pl.when(qi == 0)
    def _():
        update(get(0), qi, bias_diag())

    for hh in heads:
        acc = acc_ref[hh]
        ot = acc[0:ATTN_V_DIM] / acc[ATTN_V_DIM:ATTN_V_DIM + 1]
        dt = ot[:, 0:TQ] - lam_ref[0:1, :] * ot[:, TQ:2 * TQ]
        yt = dt * lax.rsqrt(jnp.mean(dt * dt, axis=0, keepdims=True) + NORM_EPS)
        o_ref[:, hh * ATTN_V_DIM:(hh + 1) * ATTN_V_DIM] = (
            yt.T * sg_ref[...] * (1.0 - LAMBDA_INIT)).astype(o_ref.dtype)


def _diff_attn(qkv, bias_tiles, lam, subln_g, bsz, seq):
    nq = seq // TQ
    width = HEAD_PAIR * ATTN_V_DIM
    h_blocks = D_ATTN // width
    return pl.pallas_call(
        functools.partial(_diff_attn_kernel, seq=seq),
        grid=(bsz, ATTN_HEADS // HEAD_PAIR, nq),
        in_specs=[pl.BlockSpec((TQ, width), lambda b, h, i: (b * nq + i, h)),
                  pl.BlockSpec((seq, width), lambda b, h, i: (b, h_blocks + h)),
                  pl.BlockSpec((seq, width), lambda b, h, i: (b, 2 * h_blocks + h)),
                  pl.BlockSpec((HEAD_PAIR, 2 * TQ, TQ), lambda b, h, i: (h, 0, 0)),
                  pl.BlockSpec((8, TQ), lambda b, h, i: (0, 0)),
                  pl.BlockSpec((1, ATTN_V_DIM), lambda b, h, i: (0, 0))],
        out_specs=pl.BlockSpec((TQ, width), lambda b, h, i: (b * nq + i, h)),
        out_shape=jax.ShapeDtypeStruct((bsz * seq, D_ATTN), BF16),
        scratch_shapes=[pltpu.VMEM((HEAD_PAIR, 2 * TQ, ATTN_V_DIM), BF16),
                        pltpu.VMEM((HEAD_PAIR, V_ROWS, seq), BF16),
                        pltpu.VMEM((2, HEAD_PAIR, TQ, 2 * TQ), F32),
                        pltpu.VMEM((HEAD_PAIR, 1, 2 * TQ), F32),
                        pltpu.VMEM((HEAD_PAIR, V_ROWS, 2 * TQ), F32)],
        compiler_params=_params("arbitrary", "arbitrary", "arbitrary"),
        name="diff_attn",
    )(qkv, qkv, qkv, bias_tiles, lam, subln_g)


P_MU_R, P_MU_K, P_MU_V, P_W0, P_A0, P_KK, P_KA, P_RK, P_LNG, P_LNB = range(10)
P_ROWS = 16


def _rwkv_chunk(r, wlog, cum, k, v, an, b, st_ref):
    groups = range(N_GROUPS)
    heads = GROUP // HEAD_DIM
    grp = lambda x, g: x[:, g * GROUP:(g + 1) * GROUP]

    rows_bd = lax.broadcasted_iota(jnp.int32, (GROUP, GROUP), 0) // HEAD_DIM
    lanes_bd = lax.broadcasted_iota(jnp.int32, (GROUP, GROUP), 1) // HEAD_DIM
    bd_mask = rows_bd == lanes_bd
    t_idx = lax.broadcasted_iota(jnp.int32, (CHUNK, GROUP), 0)
    s_idx = lax.broadcasted_iota(jnp.int32, (CHUNK, GROUP), 1) % HEAD_DIM
    m_strict = s_idx < t_idx
    m_incl = s_idx <= t_idx
    eye = jnp.where(s_idx == t_idx, 1.0, 0.0).astype(F32)

    def bd(x):
        return jnp.where(bd_mask, jnp.concatenate([x] * heads, axis=0), 0.0)

    tot = cum[CHUNK - 1:CHUNK, :]
    e_neg = jnp.exp(-cum)
    e_rem = jnp.exp(tot - cum)
    decay = jnp.exp(tot)
    at = jnp.exp(cum - wlog) * an
    rt = jnp.exp(cum) * r
    bt, kt = e_neg * b, e_neg * k
    b_rem, k_rem = e_rem * b, e_rem * k

    aa = [_mm_nt(jnp.concatenate([grp(at, g), grp(rt, g)], axis=0),
                 jnp.concatenate([bd(grp(bt, g)), bd(grp(kt, g))], axis=0)) for g in groups]
    a_ab = [jnp.where(m_strict, aa[g][:CHUNK, :GROUP], 0.0) for g in groups]
    a_ak = [jnp.where(m_strict, aa[g][:CHUNK, GROUP:], 0.0) for g in groups]
    a_rb = [jnp.where(m_incl, aa[g][CHUNK:, :GROUP], 0.0) for g in groups]
    a_rk = [jnp.where(m_incl, aa[g][CHUNK:, GROUP:], 0.0) for g in groups]

    minv = [eye + a_ab[g] for g in groups]
    nk = [_mm(a_ab[g], bd(a_ab[g])) for g in groups]
    for _ in range(int(math.log2(CHUNK)) - 2):
        res = [_mm(jnp.concatenate([nk[g], minv[g]], axis=0), bd(nk[g])) for g in groups]
        nk = [res[g][:CHUNK] for g in groups]
        minv = [minv[g] + res[g][CHUNK:] for g in groups]
    minv = [minv[g] + _mm(minv[g], bd(nk[g])) for g in groups]

    bd_v = [bd(grp(v, g)) for g in groups]
    x1 = [_mm(a_ak[g], bd_v[g]) for g in groups]
    ma = [_mm(minv[g], jnp.concatenate([bd(grp(at, g)), bd(x1[g])], axis=1)) for g in groups]

    st = [st_ref[g] for g in groups]
    c1 = [_mm_nt(jnp.concatenate([ma[g][:, :GROUP], grp(rt, g)], axis=0), bd(st[g])) for g in groups]
    u = [c1[g][:CHUNK] + ma[g][:, GROUP:] for g in groups]
    uv = [jnp.concatenate([bd(u[g]), bd_v[g]], axis=0) for g in groups]
    y = [c1[g][CHUNK:] + _mm(jnp.concatenate([a_rb[g], a_rk[g]], axis=1), uv[g]) for g in groups]
    upd = [_mm_tn(uv[g], jnp.concatenate([bd(grp(b_rem, g)), bd(grp(k_rem, g))], axis=0))
           for g in groups]
    for g in groups:
        folded = sum(upd[g][i * HEAD_DIM:(i + 1) * HEAD_DIM] for i in range(heads))
        st_ref[g] = st[g] * grp(decay, g) + folded
    return jnp.concatenate(y, axis=1)


def _rwkv_kernel(rp_ref, kp_ref, vp_ref, lo_ref, pv_ref, w2_ref, a2_ref, g2_ref, e_ref, tri_ref,
                 o_ref, st_ref, prev_ref):
    c = pl.program_id(1)

    @pl.when(c == 0)
    def _():
        st_ref[...] = jnp.zeros(st_ref.shape, F32)
        prev_ref[...] = jnp.zeros(prev_ref.shape, F32)

    row0 = lax.broadcasted_iota(jnp.int32, (CHUNK, D_RWKV), 0) == 0

    def shifted(x, col):
        prev = prev_ref[0:1, col * D_RWKV:(col + 1) * D_RWKV]
        return jnp.where(row0, prev, pltpu.roll(x, 1, axis=0))

    pv = lambda i: pv_ref[i:i + 1, :]
    rp, kp, vp, lo = rp_ref[...], kp_ref[...], vp_ref[...], lo_ref[...]
    lo_prev = shifted(lo, 3)
    r = rp + (shifted(rp, 0) - rp) * pv(P_MU_R)
    k = kp + (shifted(kp, 1) - kp) * pv(P_MU_K)
    v = vp + (shifted(vp, 2) - vp) * pv(P_MU_V)
    for col, ref in enumerate((rp_ref, kp_ref, vp_ref, lo_ref)):
        prev_ref[0:1, col * D_RWKV:(col + 1) * D_RWKV] = ref[CHUNK - 1:CHUNK, :]

    lora = lo[:, :LORA_COLS] + lo_prev[:, LORA_COLS:]
    w_in = pv(P_W0) + _mm(jnp.tanh(lora[:, :LORA_W]), w2_ref[...])
    w = -(jnp.maximum(-w_in, 0.0) + jnp.log(1.0 + jnp.exp(-jnp.abs(w_in)))) - 0.5
    wlog = -jnp.exp(w)
    a = 1.0 / (1.0 + jnp.exp(-(pv(P_A0) + _mm(lora[:, LORA_W:2 * LORA_W], a2_ref[...]))))
    g = _mm(1.0 / (1.0 + jnp.exp(-lora[:, 2 * LORA_W:])), g2_ref[...])

    def head_sum(x):
        return jnp.concatenate([_mm(x[:, i * GROUP:(i + 1) * GROUP], e_ref[...])
                                for i in range(N_GROUPS)], axis=1)

    kk = k * pv(P_KK)
    kk = kk / jnp.maximum(jnp.sqrt(head_sum(kk * kk)), 1e-12)
    k2 = k * (1.0 + (a - 1.0) * pv(P_KA))
    cum = jnp.dot(tri_ref[...], wlog, preferred_element_type=F32, precision=lax.Precision.HIGHEST)

    y = _rwkv_chunk(r, wlog, cum, k2, v, -kk, kk * a, st_ref)

    mu = head_sum(y) * (1.0 / HEAD_DIM)
    d = y - mu
    var = head_sum(d * d) * (1.0 / HEAD_DIM)
    yn = d * lax.rsqrt(var + RWKV_GN_EPS) * pv(P_LNG) + pv(P_LNB)
    bonus = head_sum(r * k2 * pv(P_RK)) * v
    o_ref[...] = ((yn + bonus) * g).astype(o_ref.dtype)


def _rwkv(pr, pvec, w2p, a2p, g2p, e256, tri, bsz, seq):
    nc = seq // CHUNK
    col = lambda cb: pl.BlockSpec((CHUNK, D_RWKV), lambda b, c: (b * nc + c, cb))
    full = lambda shape: pl.BlockSpec(shape, lambda b, c: (0, 0))
    return pl.pallas_call(
        _rwkv_kernel,
        grid=(bsz, nc),
        in_specs=[col(0), col(1), col(2), col(3),
                  full((P_ROWS, D_RWKV)), full((LORA_W, D_RWKV)), full((LORA_W, D_RWKV)),
                  full((LORA_G, D_RWKV)), full((GROUP, GROUP)), full((CHUNK, CHUNK))],
        out_specs=pl.BlockSpec((CHUNK, D_RWKV), lambda b, c: (b * nc + c, 0)),
        out_shape=jax.ShapeDtypeStruct((bsz * seq, D_RWKV), BF16),
        scratch_shapes=[pltpu.VMEM((N_GROUPS, HEAD_DIM, GROUP), F32),
                        pltpu.VMEM((8, 4 * D_RWKV), F32)],
        compiler_params=_params("arbitrary", "arbitrary"),
        name="rwkv7",
    )(pr, pr, pr, pr, pvec, w2p, a2p, g2p, e256, tri)


def _out_proj_kernel(oa_ref, or_ref, w_ref, x_ref, gate_ref, g_ref, sc_ref, sh_ref, x1_ref, h2_ref):
    acc = jnp.dot(oa_ref[...], w_ref[0:D_ATTN, :], preferred_element_type=F32)
    acc = acc + jnp.dot(or_ref[...], w_ref[D_ATTN:, :], preferred_element_type=F32)
    x1 = x_ref[...] + gate_ref[...] * acc
    x1_ref[...] = x1
    h2_ref[...] = _modulated_norm(x1, g_ref[...], sc_ref[...], sh_ref[...]).astype(BF16)


def _out_proj(o_attn, o_rwkv, w_bf, x2, mod3, norm_g, seq):
    n = x2.shape[0]
    tiles_per_batch = seq // TM_OUT
    mod = lambda col: pl.BlockSpec((None, 1, D_MODEL), lambda i: (i // tiles_per_batch, 0, col))
    return pl.pallas_call(
        _out_proj_kernel,
        grid=(n // TM_OUT,),
        in_specs=[pl.BlockSpec((TM_OUT, D_ATTN), lambda i: (i, 0)),
                  pl.BlockSpec((TM_OUT, D_RWKV), lambda i: (i, 0)),
                  pl.BlockSpec((D_MODEL, D_MODEL), lambda i: (0, 0), pipeline_mode=pl.Buffered(1)),
                  pl.BlockSpec((TM_OUT, D_MODEL), lambda i: (i, 0)),
                  mod(2),
                  pl.BlockSpec((1, D_MODEL), lambda i: (0, 0)),
                  mod(4), mod(3)],
        out_specs=[pl.BlockSpec((TM_OUT, D_MODEL), lambda i: (i, 0)),
                   pl.BlockSpec((TM_OUT, D_MODEL), lambda i: (i, 0))],
        out_shape=[jax.ShapeDtypeStruct((n, D_MODEL), F32),
                   jax.ShapeDtypeStruct((n, D_MODEL), BF16)],
        compiler_params=_params("arbitrary"),
        name="out_proj",
    )(o_attn, o_rwkv, w_bf, x2, mod3, norm_g, mod3, mod3)


def _ffn_up_kernel(h_ref, wg_ref, wv_ref, cwg_ref, cwv_ref, cbg_ref, cbv_ref, o_ref, carry_ref, *,
                   tiles_per_batch):
    i = pl.program_id(0)
    j = pl.program_id(1)
    first = (i % tiles_per_batch) == 0

    @pl.when(i == 0)
    def _():
        carry_ref[2 * j] = jnp.zeros((8, TN), F32)
        carry_ref[2 * j + 1] = jnp.zeros((8, TN), F32)

    def conv(up, prev, cw_ref, cb_ref):
        def taps(x, x1, x2):
            return cb_ref[...] + cw_ref[0:1, :] * x2 + cw_ref[1:2, :] * x1 + cw_ref[2:3, :] * x

        y = taps(up, pltpu.roll(up, 1, axis=0), pltpu.roll(up, 2, axis=0))
        ext = jnp.concatenate([prev, up[0:8, :]], axis=0)
        head = taps(ext, pltpu.roll(ext, 1, axis=0), pltpu.roll(ext, 2, axis=0))[8:16, :]
        return jnp.concatenate([head, y[8:, :]], axis=0)

    part = TM // FFN_PARTS
    wg = wg_ref[...].astype(BF16)
    wv = wv_ref[...].astype(BF16)
    prev_g = jnp.where(first, 0.0, carry_ref[2 * j])
    prev_v = jnp.where(first, 0.0, carry_ref[2 * j + 1])
    for r in range(FFN_PARTS):
        h = h_ref[r * part:(r + 1) * part, :]
        up_g = jnp.dot(h, wg, preferred_element_type=F32)
        up_v = jnp.dot(h, wv, preferred_element_type=F32)
        gate = conv(up_g, prev_g, cwg_ref, cbg_ref)
        val = conv(up_v, prev_v, cwv_ref, cbv_ref)
        o_ref[r * part:(r + 1) * part, :] = (gate * (1.0 / (1.0 + jnp.exp(-gate))) * val).astype(o_ref.dtype)
        prev_g, prev_v = up_g[part - 8:, :], up_v[part - 8:, :]
    carry_ref[2 * j] = prev_g
    carry_ref[2 * j + 1] = prev_v


def _ffn_up(h2, w_up, conv_w8, conv_b, seq):
    n = h2.shape[0]
    tiles_per_batch = seq // TM
    nf = D_FF // TN
    return pl.pallas_call(
        functools.partial(_ffn_up_kernel, tiles_per_batch=tiles_per_batch),
        grid=(n // TM, nf),
        in_specs=[pl.BlockSpec((TM, D_MODEL), lambda i, j: (i, 0)),
                  pl.BlockSpec((D_MODEL, TN), lambda i, j: (0, j)),
                  pl.BlockSpec((D_MODEL, TN), lambda i, j: (0, nf + j)),
                  pl.BlockSpec((8, TN), lambda i, j: (0, j)),
                  pl.BlockSpec((8, TN), lambda i, j: (0, nf + j)),
                  pl.BlockSpec((1, TN), lambda i, j: (0, j)),
                  pl.BlockSpec((1, TN), lambda i, j: (0, nf + j))],
        out_specs=pl.BlockSpec((TM, TN), lambda i, j: (i, j)),
        out_shape=jax.ShapeDtypeStruct((n, D_FF), BF16),
        scratch_shapes=[pltpu.VMEM((2 * nf, 8, TN), F32)],
        compiler_params=_params("arbitrary", "arbitrary"),
        name="ffn_up",
    )(h2, w_up, w_up, conv_w8, conv_w8, conv_b, conv_b)


def _ffn_down_kernel(a_ref, w_ref, x_ref, gate_ref, o_ref):
    acc = jnp.dot(a_ref[...], w_ref[...].astype(BF16), preferred_element_type=F32)
    o_ref[...] = x_ref[...] + gate_ref[...] * acc


def _ffn_down(act, w_down, x1, mod3, seq):
    n = x1.shape[0]
    tiles_per_batch = seq // TM_DOWN
    gate_col = 5 * (D_MODEL // TN)
    return pl.pallas_call(
        _ffn_down_kernel,
        grid=(n // TM_DOWN, D_MODEL // TN),
        in_specs=[pl.BlockSpec((TM_DOWN, D_FF), lambda i, j: (i, 0)),
                  pl.BlockSpec((D_FF, TN), lambda i, j: (0, j)),
                  pl.BlockSpec((TM_DOWN, TN), lambda i, j: (i, j)),
                  pl.BlockSpec((None, 1, TN), lambda i, j: (i // tiles_per_batch, 0, gate_col + j))],
        out_specs=pl.BlockSpec((TM_DOWN, TN), lambda i, j: (i, j)),
        out_shape=jax.ShapeDtypeStruct((n, D_MODEL), F32),
        compiler_params=_params("arbitrary", "arbitrary"),
        name="ffn_down",
    )(act, w_down, x1, mod3)


def _pad_cols(w, width):
    return jnp.pad(w, ((0, 0), (0, width - w.shape[1])))


def _pad_rows(w, height):
    return jnp.pad(w, ((0, height - w.shape[0]), (0, 0)))


def kernel(x, c, rel_bias, w_ada, b_ada, norm_mix_g, w_in, q_norm_g, k_norm_g, lambda_q1, lambda_k1,
           lambda_q2, lambda_k2, attn_subln_g, mu_rkv, mu_wag, w0, w1, w2, a0, a1, a2, g1, g2, k_k,
           k_a, r_k, ln_x_g, ln_x_b, w_out, norm_ffn_g, w_up, conv_w, conv_b, w_down):
    bsz, seq, _ = x.shape
    n = bsz * seq
    x2 = x.reshape(n, D_MODEL)

    mod = _adaln(c, w_ada[0], b_ada[0])
    mod3 = mod.reshape(bsz, 1, 6 * D_MODEL)

    idx = jnp.arange(GROUP) // HEAD_DIM
    e256 = (idx[:, None] == idx[None, :]).astype(BF16)
    tri = (jnp.arange(CHUNK)[:, None] >= jnp.arange(CHUNK)[None, :]).astype(F32)

    gain = jnp.concatenate([jnp.tile(q_norm_g[0], D_ATTN // HEAD_DIM) * (HEAD_DIM ** -0.5 * LOG2E),
                            jnp.tile(k_norm_g[0], D_ATTN // HEAD_DIM)]).reshape(1, 2 * D_ATTN)
    mu = mu_wag[0]
    lora_w = (_pad_cols(w1[0], LORA_W), _pad_cols(a1[0], LORA_W), _pad_cols(g1[0], LORA_G))
    w_lora = jnp.concatenate([(1.0 - mu[i])[:, None] * w for i, w in enumerate(lora_w)]
                             + [mu[i][:, None] * w for i, w in enumerate(lora_w)], axis=1)
    qkv, pr = _proj(x2, norm_mix_g, mod3, w_in[0], w_lora, gain, e256, seq)

    bias_tiles, lam = _attn_tables(rel_bias, lambda_q1, lambda_k1, lambda_q2, lambda_k2)
    o_attn = _diff_attn(qkv, bias_tiles, lam, attn_subln_g, bsz, seq)

    pvec = jnp.concatenate([mu_rkv[0], w0, a0, k_k, k_a, r_k.reshape(1, D_RWKV), ln_x_g, ln_x_b,
                            jnp.zeros((P_ROWS - 10, D_RWKV), F32)], axis=0)
    o_rwkv = _rwkv(pr, pvec, _pad_rows(w2[0], LORA_W).astype(BF16), _pad_rows(a2[0], LORA_W).astype(BF16),
                   _pad_rows(g2[0], LORA_G).astype(BF16), e256, tri, bsz, seq)

    x1, h2 = _out_proj(o_attn, o_rwkv, w_out[0].astype(BF16), x2, mod3, norm_ffn_g, seq)

    act = _ffn_up(h2, w_up[0], _pad_rows(conv_w[0], 8), conv_b, seq)
    out = _ffn_down(act, w_down[0], x1, mod3, seq)
    return out.reshape(bsz, seq, D_MODEL)
```

```python
import functools
import math

import jax
import jax.numpy as jnp
from jax import lax
from jax.experimental import pallas as pl
from jax.experimental.pallas import tpu as pltpu

F32 = jnp.float32
BF16 = jnp.bfloat16

D_MODEL = 2048
D_ATTN = 1024
D_RWKV = 1024
HEAD_DIM = 64
ATTN_HEADS = 8
ATTN_V_DIM = 128
RWKV_HEADS = 16
D_FF = 5632
NUM_BUCKETS = 32
MAX_EXACT = 16
MAX_DISTANCE = 128
NORM_EPS = 1e-6
RWKV_GN_EPS = 64e-5
NEG_INF = -1e30
LOG2E = math.log2(math.e)
LAMBDA_INIT = 0.8 - 0.6 * math.exp(0.0)

LANES = 128
LORA_W = 128
LORA_G = 256
LORA_COLS = 2 * LORA_W + LORA_G

TM = 1024
TN = 512
TM_DOWN = 512
TM_OUT = 512
TQ = 256
HEAD_PAIR = 2
V_ROWS = ATTN_V_DIM + 16
CHUNK = 64
GROUP = 256
N_GROUPS = D_RWKV // GROUP

VMEM_LIMIT = 56 * 1024 * 1024


def _mm(a, b):
    return jnp.dot(a.astype(BF16), b.astype(BF16), preferred_element_type=F32)


def _mm_nt(a, b):
    return lax.dot_general(a.astype(BF16), b.astype(BF16), (((1,), (1,)), ((), ())),
                           preferred_element_type=F32)


def _mm_tn(a, b):
    return lax.dot_general(a.astype(BF16), b.astype(BF16), (((0,), (0,)), ((), ())),
                           preferred_element_type=F32)


def _params(*sem):
    return pltpu.CompilerParams(dimension_semantics=sem, vmem_limit_bytes=VMEM_LIMIT)


def _adaln_kernel(c_ref, w_ref, b_ref, o_ref):
    c = c_ref[...]
    c_act = c * (1.0 / (1.0 + jnp.exp(-c)))
    acc = jnp.dot(c_act, w_ref[...], preferred_element_type=F32, precision=lax.Precision.HIGHEST)
    o_ref[...] = acc + b_ref[...]


def _adaln(c, w_ada, b_ada):
    bsz = c.shape[0]
    n_out = w_ada.shape[1]
    tn = 1024
    return pl.pallas_call(
        _adaln_kernel,
        grid=(n_out // tn,),
        in_specs=[pl.BlockSpec((bsz, D_MODEL), lambda j: (0, 0)),
                  pl.BlockSpec((D_MODEL, tn), lambda j: (0, j)),
                  pl.BlockSpec((1, tn), lambda j: (0, j))],
        out_specs=pl.BlockSpec((bsz, tn), lambda j: (0, j)),
        out_shape=jax.ShapeDtypeStruct((bsz, n_out), F32),
        compiler_params=_params("arbitrary"),
        name="adaln",
    )(c, w_ada, b_ada.reshape(1, n_out))


def _modulated_norm(x, g, scale, shift):
    y = x * lax.rsqrt(jnp.mean(x * x, axis=-1, keepdims=True) + NORM_EPS)
    return (y * g) * (1.0 + scale) + shift


QK_TILES = 2 * D_ATTN // TN
ATTN_TILES = 3 * D_ATTN // TN
W_IN_TILES = (3 * D_ATTN + 3 * D_RWKV) // TN
LORA_TILES = 2 * LORA_COLS // TN


def _proj_kernel(x_ref, g_ref, sc_ref, sh_ref, w_ref, wl_ref, gain_ref, e_ref, qkv_ref, pr_ref, h_ref):
    j = pl.program_id(1)

    @pl.when(j == 0)
    def _():
        h_ref[...] = _modulated_norm(x_ref[...], g_ref[...], sc_ref[...], sh_ref[...]).astype(BF16)

    def project(weights_ref):
        return jnp.dot(h_ref[...], weights_ref[...].astype(BF16), preferred_element_type=F32)

    @pl.when(j < QK_TILES)
    def _():
        acc = project(w_ref)
        half = TN // 2
        for s in range(2):
            a = acc[:, s * half:(s + 1) * half]
            ss = _mm(a * a, e_ref[...])
            y = a * lax.rsqrt(ss * (1.0 / HEAD_DIM) + NORM_EPS) * gain_ref[:, s * half:(s + 1) * half]
            qkv_ref[:, s * half:(s + 1) * half] = y.astype(qkv_ref.dtype)

    @pl.when(jnp.logical_and(j >= QK_TILES, j < ATTN_TILES))
    def _():
        qkv_ref[...] = project(w_ref).astype(qkv_ref.dtype)

    @pl.when(jnp.logical_and(j >= ATTN_TILES, j < W_IN_TILES))
    def _():
        pr_ref[...] = project(w_ref)

    @pl.when(j >= W_IN_TILES)
    def _():
        pr_ref[...] = project(wl_ref)


def _proj(x2, norm_g, mod3, w_in, w_lora, gain, e256, seq):
    n = x2.shape[0]
    tiles_per_batch = seq // TM
    batch = lambda i, j: i // tiles_per_batch
    return pl.pallas_call(
        _proj_kernel,
        grid=(n // TM, W_IN_TILES + LORA_TILES),
        in_specs=[pl.BlockSpec((TM, D_MODEL), lambda i, j: (i, 0)),
                  pl.BlockSpec((1, D_MODEL), lambda i, j: (0, 0)),
                  pl.BlockSpec((None, 1, D_MODEL), lambda i, j: (batch(i, j), 0, 1)),
                  pl.BlockSpec((None, 1, D_MODEL), lambda i, j: (batch(i, j), 0, 0)),
                  pl.BlockSpec((D_MODEL, TN), lambda i, j: (0, jnp.minimum(j, W_IN_TILES - 1))),
                  pl.BlockSpec((D_MODEL, TN), lambda i, j: (0, jnp.maximum(j - W_IN_TILES, 0))),
                  pl.BlockSpec((1, TN), lambda i, j: (0, jnp.minimum(j, QK_TILES - 1))),
                  pl.BlockSpec((TN // 2, TN // 2), lambda i, j: (0, 0))],
        out_specs=[pl.BlockSpec((TM, TN), lambda i, j: (i, jnp.minimum(j, ATTN_TILES - 1))),
                   pl.BlockSpec((TM, TN), lambda i, j: (i, jnp.maximum(j - ATTN_TILES, 0)))],
        out_shape=[jax.ShapeDtypeStruct((n, 3 * D_ATTN), BF16),
                   jax.ShapeDtypeStruct((n, 3 * D_RWKV + 2 * LORA_COLS), F32)],
        scratch_shapes=[pltpu.VMEM((TM, D_MODEL), BF16)],
        compiler_params=_params("arbitrary", "arbitrary"),
        name="proj",
    )(x2, norm_g, mod3, mod3, w_in, w_lora, gain, e256)


def _attn_tables_kernel(rb_ref, lq1_ref, lk1_ref, lq2_ref, lk2_ref, bias_ref, lam_ref):
    h = pl.program_id(0)
    rows = lax.broadcasted_iota(jnp.int32, (2 * TQ, TQ), 0)
    cols = lax.broadcasted_iota(jnp.int32, (2 * TQ, TQ), 1)
    far = rb_ref[NUM_BUCKETS - 1, h]
    dist = cols - rows + TQ
    n = jnp.maximum(dist, 0)
    nf = jnp.maximum(n, 1).astype(F32)
    large = MAX_EXACT + (jnp.log(nf / MAX_EXACT) / math.log(MAX_DISTANCE / MAX_EXACT)
                         * (NUM_BUCKETS - MAX_EXACT)).astype(jnp.int32)
    large = jnp.minimum(large, NUM_BUCKETS - 1)
    bucket = jnp.where(n < MAX_EXACT, n, large)
    bias = jnp.zeros((2 * TQ, TQ), F32)
    for b in range(NUM_BUCKETS):
        bias = jnp.where(bucket == b, rb_ref[b, h], bias)
    bias_ref[...] = jnp.where(dist >= 0, (bias - far) * LOG2E, NEG_INF)
    s1 = jnp.sum(lq1_ref[...] * lk1_ref[...], axis=1, keepdims=True)
    s2 = jnp.sum(lq2_ref[...] * lk2_ref[...], axis=1, keepdims=True)
    lam = jnp.exp(s1) - jnp.exp(s2) + LAMBDA_INIT
    lam_ref[...] = jnp.broadcast_to(lam, lam_ref.shape)


def _attn_tables(rel_bias, lq1, lk1, lq2, lk2):
    vec = pl.BlockSpec((1, HEAD_DIM), lambda h: (0, 0))
    return pl.pallas_call(
        _attn_tables_kernel,
        grid=(ATTN_HEADS,),
        in_specs=[pl.BlockSpec(memory_space=pltpu.SMEM), vec, vec, vec, vec],
        out_specs=[pl.BlockSpec((None, 2 * TQ, TQ), lambda h: (h, 0, 0)),
                   pl.BlockSpec((8, TQ), lambda h: (0, 0))],
        out_shape=[jax.ShapeDtypeStruct((ATTN_HEADS, 2 * TQ, TQ), F32),
                   jax.ShapeDtypeStruct((8, TQ), F32)],
        compiler_params=_params("arbitrary"),
        name="attn_tables",
    )(rel_bias, lq1, lk1, lq2, lk2)


def _attn_scores(qs_ref, k_ref, j):
    start = pl.multiple_of(j * TQ, TQ)
    return [lax.dot_general(k_ref[pl.ds(start, TQ), hh * ATTN_V_DIM:(hh + 1) * ATTN_V_DIM], qs_ref[hh],
                            (((1,), (1,)), ((), ())), preferred_element_type=F32)
            for hh in range(HEAD_PAIR)]


def _attn_update(vt_ref, m_ref, acc_ref, sts, j, biases):
    heads = range(HEAD_PAIR)
    if biases is not None:
        sts = [sts[hh] + jnp.concatenate([biases[hh], biases[hh]], axis=1) for hh in heads]
    m_prev = [m_ref[hh] for hh in heads]
    m_new = [jnp.maximum(m_prev[hh], jnp.max(sts[hh], axis=0, keepdims=True)) for hh in heads]
    alpha = [jnp.exp2(m_prev[hh] - m_new[hh]) for hh in heads]
    p = [jnp.exp2(sts[hh] - m_new[hh]) for hh in heads]
    start = pl.multiple_of(j * TQ, TQ)
    pv = [jnp.dot(vt_ref[hh, :, pl.ds(start, TQ)], p[hh].astype(BF16), preferred_element_type=F32)
          for hh in heads]
    for hh in heads:
        acc_ref[hh] = alpha[hh] * acc_ref[hh] + pv[hh]
        m_ref[hh] = m_new[hh]


def _diff_attn_kernel(q_ref, k_ref, v_ref, bias_ref, lam_ref, sg_ref, o_ref,
                      qs_ref, vt_ref, s_ref, m_ref, acc_ref, *, seq):
    qi = pl.program_id(2)
    heads = range(HEAD_PAIR)

    @pl.when(qi == 0)
    def _():
        step = 2 * TQ
        for hh in heads:
            for c in range(seq // step):
                v = v_ref[c * step:(c + 1) * step, hh * ATTN_V_DIM:(hh + 1) * ATTN_V_DIM]
                vt_ref[hh, 0:ATTN_V_DIM, c * step:(c + 1) * step] = v.astype(F32).T.astype(BF16)
            ones_row = lax.broadcasted_iota(jnp.int32, (V_ROWS - ATTN_V_DIM, seq), 0) == 0
            vt_ref[hh, ATTN_V_DIM:V_ROWS, :] = jnp.where(ones_row, 1.0, 0.0).astype(BF16)

    lane = lax.broadcasted_iota(jnp.int32, (TQ, ATTN_V_DIM), 1)
    for hh in heads:
        q = q_ref[:, hh * ATTN_V_DIM:(hh + 1) * ATTN_V_DIM]
        zero = jnp.zeros_like(q)
        qs_ref[hh, 0:TQ, :] = jnp.where(lane < HEAD_DIM, q, zero)
        qs_ref[hh, TQ:2 * TQ, :] = jnp.where(lane >= HEAD_DIM, q, zero)
    m_ref[...] = jnp.full(m_ref.shape, NEG_INF, F32)
    acc_ref[...] = jnp.zeros(acc_ref.shape, F32)
    scores = functools.partial(_attn_scores, qs_ref, k_ref)
    update = functools.partial(_attn_update, vt_ref, m_ref, acc_ref)

    def put(buf, sts):
        for hh in heads:
            s_ref[buf, hh] = sts[hh]

    get = lambda buf: [s_ref[buf, hh] for hh in heads]
    bias_prev = lambda: [bias_ref[hh, 0:TQ, :] for hh in heads]
    bias_diag = lambda: [bias_ref[hh, TQ:2 * TQ, :] for hh in heads]

    n_far = jnp.maximum(qi - 1, 0)
    put(0, scores(0))

    def far_pair(p, carry):
        j = 2 * p
        put(1, scores(j + 1))
        update(get(0), j, None)
        put(0, scores(j + 2))
        update(get(1), j + 1, None)
        return carry

    lax.fori_loop(0, n_far // 2, far_pair, 0)

    @pl.when(n_far % 2 == 1)
    def _():
        update(get(0), n_far - 1, None)
        put(0, scores(n_far))

    @pl.when(qi >= 1)
    def _():
        put(1, scores(qi))
        update(get(0), qi - 1, bias_prev())
        update(get(1), qi, bias_diag())

    @pl.when(qi == 0)
    def _():
        update(get(0), qi, bias_diag())

    for hh in heads:
        acc = acc_ref[hh]
        ot = acc[0:ATTN_V_DIM] / acc[ATTN_V_DIM:ATTN_V_DIM + 1]
        dt = ot[:, 0:TQ] - lam_ref[0:1, :] * ot[:, TQ:2 * TQ]
        yt = dt * lax.rsqrt(jnp.mean(dt * dt, axis=0, keepdims=True) + NORM_EPS)
        o_ref[:, hh * ATTN_V_DIM:(hh + 1) * ATTN_V_DIM] = (
            yt.T * sg_ref[...] * (1.0 - LAMBDA_INIT)).astype(o_ref.dtype)


def _diff_attn(qkv, bias_tiles, lam, subln_g, bsz, seq):
    nq = seq // TQ
    width = HEAD_PAIR * ATTN_V_DIM
    h_blocks = D_ATTN // width
    return pl.pallas_call(
        functools.partial(_diff_attn_kernel, seq=seq),
        grid=(bsz, ATTN_HEADS // HEAD_PAIR, nq),
        in_specs=[pl.BlockSpec((TQ, width), lambda b, h, i: (b * nq + i, h)),
                  pl.BlockSpec((seq, width), lambda b, h, i: (b, h_blocks + h)),
                  pl.BlockSpec((seq, width), lambda b, h, i: (b, 2 * h_blocks + h)),
                  pl.BlockSpec((HEAD_PAIR, 2 * TQ, TQ), lambda b, h, i: (h, 0, 0)),
                  pl.BlockSpec((8, TQ), lambda b, h, i: (0, 0)),
                  pl.BlockSpec((1, ATTN_V_DIM), lambda b, h, i: (0, 0))],
        out_specs=pl.BlockSpec((TQ, width), lambda b, h, i: (b * nq + i, h)),
        out_shape=jax.ShapeDtypeStruct((bsz * seq, D_ATTN), BF16),
        scratch_shapes=[pltpu.VMEM((HEAD_PAIR, 2 * TQ, ATTN_V_DIM), BF16),
                        pltpu.VMEM((HEAD_PAIR, V_ROWS, seq), BF16),
                        pltpu.VMEM((2, HEAD_PAIR, TQ, 2 * TQ), F32),
                        pltpu.VMEM((HEAD_PAIR, 1, 2 * TQ), F32),
                        pltpu.VMEM((HEAD_PAIR, V_ROWS, 2 * TQ), F32)],
        compiler_params=_params("arbitrary", "arbitrary", "arbitrary"),
        name="diff_attn",
    )(qkv, qkv, qkv, bias_tiles, lam, subln_g)


P_MU_R, P_MU_K, P_MU_V, P_W0, P_A0, P_KK, P_KA, P_RK, P_LNG, P_LNB = range(10)
P_ROWS = 16


S_AT, S_RT, S_BT, S_KT, S_BREM, S_KREM, S_V = range(7)
S_BONUS, S_GATE = range(2)


def _rwkv_prep(g, rp_ref, kp_ref, vp_ref, lo_ref, pv_ref, w2_ref, a2_ref, g2_ref, e_ref, tri_ref, prev_ref,
               ops_ref, epi_ref, dec_ref):
    cols = slice(g * GROUP, (g + 1) * GROUP)

    def shifted(x, prev):
        row0 = lax.broadcasted_iota(jnp.int32, x.shape, 0) == 0
        return jnp.where(row0, prev, pltpu.roll(x, 1, axis=0))

    def mixed(ref, col, mu_row):
        x = ref[:, cols]
        prev = prev_ref[0:1, col * D_RWKV + g * GROUP:col * D_RWKV + (g + 1) * GROUP]
        return x + (shifted(x, prev) - x) * pv_ref[mu_row:mu_row + 1, cols]

    pv = lambda i: pv_ref[i:i + 1, cols]
    r = mixed(rp_ref, 0, P_MU_R)
    k = mixed(kp_ref, 1, P_MU_K)
    v = mixed(vp_ref, 2, P_MU_V)

    lora = lo_ref[:, :LORA_COLS] + shifted(lo_ref[:, LORA_COLS:], prev_ref[0:1, 3 * D_RWKV + LORA_COLS:])
    w_in = pv(P_W0) + _mm(jnp.tanh(lora[:, :LORA_W]), w2_ref[:, cols])
    w = -(jnp.maximum(-w_in, 0.0) + jnp.log(1.0 + jnp.exp(-jnp.abs(w_in)))) - 0.5
    wlog = -jnp.exp(w)
    a = 1.0 / (1.0 + jnp.exp(-(pv(P_A0) + _mm(lora[:, LORA_W:2 * LORA_W], a2_ref[:, cols]))))
    gate = _mm(1.0 / (1.0 + jnp.exp(-lora[:, 2 * LORA_W:])), g2_ref[:, cols])

    head_sum = lambda x: _mm(x, e_ref[...])
    kk = k * pv(P_KK)
    kk = kk / jnp.maximum(jnp.sqrt(head_sum(kk * kk)), 1e-12)
    k2 = k * (1.0 + (a - 1.0) * pv(P_KA))
    b = kk * a
    cum = jnp.dot(tri_ref[...], wlog, preferred_element_type=F32, precision=lax.Precision.HIGHEST)

    tot = cum[CHUNK - 1:CHUNK, :]
    e_neg = jnp.exp(-cum)
    e_rem = jnp.exp(tot - cum)
    staged = {S_AT: jnp.exp(cum - wlog) * (-kk), S_RT: jnp.exp(cum) * r, S_BT: e_neg * b, S_KT: e_neg * k2,
              S_BREM: e_rem * b, S_KREM: e_rem * k2, S_V: v}
    for slot, val in staged.items():
        ops_ref[slot, :, cols] = val.astype(BF16)
    epi_ref[S_BONUS, :, cols] = head_sum(r * k2 * pv(P_RK)) * v
    epi_ref[S_GATE, :, cols] = gate
    dec_ref[:, cols] = jnp.broadcast_to(jnp.exp(tot), (dec_ref.shape[0], GROUP))


def _head_sum(x, e_ref):
    return jnp.concatenate([_mm(x[:, i * GROUP:(i + 1) * GROUP], e_ref[...]) for i in range(N_GROUPS)],
                           axis=1)


def _rwkv_recur(ops_ref, epi_ref, dec_ref, pv_ref, e_ref, st_ref, o_ref, fillers):
    groups = range(N_GROUPS)
    heads = GROUP // HEAD_DIM
    grp = lambda slot, g: ops_ref[slot, :, g * GROUP:(g + 1) * GROUP]
    fillers = list(fillers)

    rows_bd = lax.broadcasted_iota(jnp.int32, (GROUP, GROUP), 0) // HEAD_DIM
    lanes_bd = lax.broadcasted_iota(jnp.int32, (GROUP, GROUP), 1) // HEAD_DIM
    bd_mask = rows_bd == lanes_bd
    t_idx = lax.broadcasted_iota(jnp.int32, (CHUNK, GROUP), 0)
    s_idx = lax.broadcasted_iota(jnp.int32, (CHUNK, GROUP), 1) % HEAD_DIM
    m_strict = s_idx < t_idx
    m_incl = s_idx <= t_idx
    eye = jnp.where(s_idx == t_idx, 1.0, 0.0).astype(F32)

    def bd(x):
        return jnp.where(bd_mask, jnp.concatenate([x] * heads, axis=0), jnp.zeros((), x.dtype))

    at = [grp(S_AT, g) for g in groups]
    rt = [grp(S_RT, g) for g in groups]
    aa = [_mm_nt(jnp.concatenate([at[g], rt[g]], axis=0),
                 jnp.concatenate([bd(grp(S_BT, g)), bd(grp(S_KT, g))], axis=0)) for g in groups]
    a_ab = [jnp.where(m_strict, aa[g][:CHUNK, :GROUP], 0.0) for g in groups]
    a_ak = [jnp.where(m_strict, aa[g][:CHUNK, GROUP:], 0.0) for g in groups]
    a_rb = [jnp.where(m_incl, aa[g][CHUNK:, :GROUP], 0.0) for g in groups]
    a_rk = [jnp.where(m_incl, aa[g][CHUNK:, GROUP:], 0.0) for g in groups]

    minv = [eye + a_ab[g] for g in groups]
    nk = [_mm(a_ab[g], bd(a_ab[g])) for g in groups]
    for _ in range(int(math.log2(CHUNK)) - 2):
        if fillers:
            fillers.pop(0)()
        res = [_mm(jnp.concatenate([nk[g], minv[g]], axis=0), bd(nk[g])) for g in groups]
        nk = [res[g][:CHUNK] for g in groups]
        minv = [minv[g] + res[g][CHUNK:] for g in groups]
    minv = [minv[g] + _mm(minv[g], bd(nk[g])) for g in groups]
    while fillers:
        fillers.pop(0)()

    bd_v = [bd(grp(S_V, g)) for g in groups]
    x1 = [_mm(a_ak[g], bd_v[g]) for g in groups]
    ma = [_mm(minv[g], jnp.concatenate([bd(at[g]), bd(x1[g].astype(BF16))], axis=1)) for g in groups]

    st = [st_ref[g] for g in groups]
    c1 = [_mm_nt(jnp.concatenate([ma[g][:, :GROUP].astype(BF16), rt[g]], axis=0), bd(st[g])) for g in groups]
    u = [c1[g][:CHUNK] + ma[g][:, GROUP:] for g in groups]
    uv = [jnp.concatenate([bd(u[g].astype(BF16)), bd_v[g]], axis=0) for g in groups]
    y = [c1[g][CHUNK:] + _mm(jnp.concatenate([a_rb[g], a_rk[g]], axis=1), uv[g]) for g in groups]
    upd = [_mm_tn(uv[g], jnp.concatenate([bd(grp(S_BREM, g)), bd(grp(S_KREM, g))], axis=0))
           for g in groups]
    for g in groups:
        folded = sum(upd[g][i * HEAD_DIM:(i + 1) * HEAD_DIM] for i in range(heads))
        st_ref[g] = st[g] * dec_ref[0:1, g * GROUP:(g + 1) * GROUP] + folded
    y = jnp.concatenate(y, axis=1)

    pv = lambda i: pv_ref[i:i + 1, :]
    mu = _head_sum(y, e_ref) * (1.0 / HEAD_DIM)
    d = y - mu
    var = _head_sum(d * d, e_ref) * (1.0 / HEAD_DIM)
    yn = d * lax.rsqrt(var + RWKV_GN_EPS) * pv(P_LNG) + pv(P_LNB)
    o_ref[...] = ((yn + epi_ref[S_BONUS]) * epi_ref[S_GATE]).astype(o_ref.dtype)


def _rwkv_kernel(rp_ref, kp_ref, vp_ref, lo_ref, pv_ref, w2_ref, a2_ref, g2_ref, e_ref, tri_ref,
                 o_ref, st_ref, prev_ref, ops_ref, epi_ref, dec_ref):
    c = pl.program_id(1)

    @pl.when(c == 0)
    def _():
        st_ref[...] = jnp.zeros(st_ref.shape, F32)
        prev_ref[...] = jnp.zeros(prev_ref.shape, F32)
        ops_ref[...] = jnp.zeros(ops_ref.shape, BF16)
        epi_ref[...] = jnp.zeros(epi_ref.shape, F32)
        dec_ref[...] = jnp.zeros(dec_ref.shape, F32)

    def step(wr, rd):
        prep = [functools.partial(_rwkv_prep, g, rp_ref, kp_ref, vp_ref, lo_ref, pv_ref, w2_ref, a2_ref,
                                  g2_ref, e_ref, tri_ref, prev_ref, ops_ref.at[wr], epi_ref.at[wr],
                                  dec_ref.at[wr]) for g in range(N_GROUPS)]
        _rwkv_recur(ops_ref.at[rd], epi_ref.at[rd], dec_ref.at[rd], pv_ref, e_ref, st_ref, o_ref, prep)
        for col, ref in enumerate((rp_ref, kp_ref, vp_ref, lo_ref)):
            prev_ref[0:1, col * D_RWKV:(col + 1) * D_RWKV] = ref[CHUNK - 1:CHUNK, :]

    @pl.when(c % 2 == 0)
    def _():
        step(0, 1)

    @pl.when(c % 2 == 1)
    def _():
        step(1, 0)


def _rwkv(pr, pvec, w2p, a2p, g2p, e256, tri, bsz, seq):
    nc = seq // CHUNK
    col = lambda cb: pl.BlockSpec((CHUNK, D_RWKV), lambda b, c: (b * nc + jnp.minimum(c, nc - 1), cb))
    full = lambda shape: pl.BlockSpec(shape, lambda b, c: (0, 0))
    return pl.pallas_call(
        _rwkv_kernel,
        grid=(bsz, nc + 1),
        in_specs=[col(0), col(1), col(2), col(3),
                  full((P_ROWS, D_RWKV)), full((LORA_W, D_RWKV)), full((LORA_W, D_RWKV)),
                  full((LORA_G, D_RWKV)), full((GROUP, GROUP)), full((CHUNK, CHUNK))],
        out_specs=pl.BlockSpec((CHUNK, D_RWKV), lambda b, c: (b * nc + jnp.maximum(c - 1, 0), 0)),
        out_shape=jax.ShapeDtypeStruct((bsz * seq, D_RWKV), BF16),
        scratch_shapes=[pltpu.VMEM((N_GROUPS, HEAD_DIM, GROUP), F32),
                        pltpu.VMEM((8, 4 * D_RWKV), F32),
                        pltpu.VMEM((2, 7, CHUNK, D_RWKV), BF16),
                        pltpu.VMEM((2, 2, CHUNK, D_RWKV), F32),
                        pltpu.VMEM((2, 8, D_RWKV), F32)],
        compiler_params=_params("arbitrary", "arbitrary"),
        name="rwkv7",
    )(pr, pr, pr, pr, pvec, w2p, a2p, g2p, e256, tri)


def _out_proj_kernel(oa_ref, or_ref, w_ref, x_ref, gate_ref, g_ref, sc_ref, sh_ref, x1_ref, h2_ref):
    acc = jnp.dot(oa_ref[...], w_ref[0:D_ATTN, :], preferred_element_type=F32)
    acc = acc + jnp.dot(or_ref[...], w_ref[D_ATTN:, :], preferred_element_type=F32)
    x1 = x_ref[...] + gate_ref[...] * acc
    x1_ref[...] = x1
    h2_ref[...] = _modulated_norm(x1, g_ref[...], sc_ref[...], sh_ref[...]).astype(BF16)


def _out_proj(o_attn, o_rwkv, w_bf, x2, mod3, norm_g, seq):
    n = x2.shape[0]
    tiles_per_batch = seq // TM_OUT
    mod = lambda col: pl.BlockSpec((None, 1, D_MODEL), lambda i: (i // tiles_per_batch, 0, col))
    return pl.pallas_call(
        _out_proj_kernel,
        grid=(n // TM_OUT,),
        in_specs=[pl.BlockSpec((TM_OUT, D_ATTN), lambda i: (i, 0)),
                  pl.BlockSpec((TM_OUT, D_RWKV), lambda i: (i, 0)),
                  pl.BlockSpec((D_MODEL, D_MODEL), lambda i: (0, 0), pipeline_mode=pl.Buffered(1)),
                  pl.BlockSpec((TM_OUT, D_MODEL), lambda i: (i, 0)),
                  mod(2),
                  pl.BlockSpec((1, D_MODEL), lambda i: (0, 0)),
                  mod(4), mod(3)],
        out_specs=[pl.BlockSpec((TM_OUT, D_MODEL), lambda i: (i, 0)),
                   pl.BlockSpec((TM_OUT, D_MODEL), lambda i: (i, 0))],
        out_shape=[jax.ShapeDtypeStruct((n, D_MODEL), F32),
                   jax.ShapeDtypeStruct((n, D_MODEL), BF16)],
        compiler_params=_params("arbitrary"),
        name="out_proj",
    )(o_attn, o_rwkv, w_bf, x2, mod3, norm_g, mod3, mod3)


def _ffn_up_kernel(h_ref, wg_ref, wv_ref, cwg_ref, cwv_ref, cbg_ref, cbv_ref, o_ref,
                   wgb_ref, wvb_ref, carry_ref, *, tiles_per_batch):
    i = pl.program_id(1)

    @pl.when(i == 0)
    def _():
        wgb_ref[...] = wg_ref[...].astype(BF16)
        wvb_ref[...] = wv_ref[...].astype(BF16)
        carry_ref[...] = jnp.zeros(carry_ref.shape, F32)

    first = (i % tiles_per_batch) == 0

    def conv(up, prev, cw_ref, cb_ref):
        def taps(x, x1, x2):
            return cb_ref[...] + cw_ref[0:1, :] * x2 + cw_ref[1:2, :] * x1 + cw_ref[2:3, :] * x

        y = taps(up, pltpu.roll(up, 1, axis=0), pltpu.roll(up, 2, axis=0))
        ext = jnp.concatenate([prev, up[0:8, :]], axis=0)
        head = taps(ext, pltpu.roll(ext, 1, axis=0), pltpu.roll(ext, 2, axis=0))[8:16, :]
        return jnp.concatenate([head, y[8:, :]], axis=0)

    h = h_ref[...]
    up_g = jnp.dot(h, wgb_ref[...], preferred_element_type=F32)
    up_v = jnp.dot(h, wvb_ref[...], preferred_element_type=F32)
    gate = conv(up_g, jnp.where(first, 0.0, carry_ref[0]), cwg_ref, cbg_ref)
    val = conv(up_v, jnp.where(first, 0.0, carry_ref[1]), cwv_ref, cbv_ref)
    carry_ref[0] = up_g[TM - 8:, :]
    carry_ref[1] = up_v[TM - 8:, :]
    o_ref[...] = (gate * (1.0 / (1.0 + jnp.exp(-gate))) * val).astype(o_ref.dtype)


def _ffn_up(h2, w_up, conv_w8, conv_b, seq):
    n = h2.shape[0]
    tiles_per_batch = seq // TM
    nf = D_FF // TN
    return pl.pallas_call(
        functools.partial(_ffn_up_kernel, tiles_per_batch=tiles_per_batch),
        grid=(nf, n // TM),
        in_specs=[pl.BlockSpec((TM, D_MODEL), lambda j, i: (i, 0)),
                  pl.BlockSpec((D_MODEL, TN), lambda j, i: (0, j)),
                  pl.BlockSpec((D_MODEL, TN), lambda j, i: (0, nf + j)),
                  pl.BlockSpec((8, TN), lambda j, i: (0, j)),
                  pl.BlockSpec((8, TN), lambda j, i: (0, nf + j)),
                  pl.BlockSpec((1, TN), lambda j, i: (0, j)),
                  pl.BlockSpec((1, TN), lambda j, i: (0, nf + j))],
        out_specs=pl.BlockSpec((TM, TN), lambda j, i: (i, j)),
        out_shape=jax.ShapeDtypeStruct((n, D_FF), BF16),
        scratch_shapes=[pltpu.VMEM((D_MODEL, TN), BF16),
                        pltpu.VMEM((D_MODEL, TN), BF16),
                        pltpu.VMEM((2, 8, TN), F32)],
        compiler_params=_params("arbitrary", "arbitrary"),
        name="ffn_up",
    )(h2, w_up, w_up, conv_w8, conv_w8, conv_b, conv_b)


def _ffn_down_kernel(a_ref, w_ref, x_ref, gate_ref, o_ref, wb_ref):
    @pl.when(pl.program_id(1) == 0)
    def _():
        wb_ref[...] = w_ref[...].astype(BF16)

    acc = jnp.dot(a_ref[...], wb_ref[...], preferred_element_type=F32)
    o_ref[...] = x_ref[...] + gate_ref[...] * acc


def _ffn_down(act, w_down, x1, mod3, seq):
    n = x1.shape[0]
    tiles_per_batch = seq // TM_DOWN
    gate_col = 5 * (D_MODEL // TN)
    return pl.pallas_call(
        _ffn_down_kernel,
        grid=(D_MODEL // TN, n // TM_DOWN),
        in_specs=[pl.BlockSpec((TM_DOWN, D_FF), lambda j, i: (i, 0)),
                  pl.BlockSpec((D_FF, TN), lambda j, i: (0, j)),
                  pl.BlockSpec((TM_DOWN, TN), lambda j, i: (i, j)),
                  pl.BlockSpec((None, 1, TN), lambda j, i: (i // tiles_per_batch, 0, gate_col + j))],
        out_specs=pl.BlockSpec((TM_DOWN, TN), lambda j, i: (i, j)),
        out_shape=jax.ShapeDtypeStruct((n, D_MODEL), F32),
        scratch_shapes=[pltpu.VMEM((D_FF, TN), BF16)],
        compiler_params=_params("arbitrary", "arbitrary"),
        name="ffn_down",
    )(act, w_down, x1, mod3)


def _pad_cols(w, width):
    return jnp.pad(w, ((0, 0), (0, width - w.shape[1])))


def _pad_rows(w, height):
    return jnp.pad(w, ((0, height - w.shape[0]), (0, 0)))


def kernel(x, c, rel_bias, w_ada, b_ada, norm_mix_g, w_in, q_norm_g, k_norm_g, lambda_q1, lambda_k1,
           lambda_q2, lambda_k2, attn_subln_g, mu_rkv, mu_wag, w0, w1, w2, a0, a1, a2, g1, g2, k_k,
           k_a, r_k, ln_x_g, ln_x_b, w_out, norm_ffn_g, w_up, conv_w, conv_b, w_down):
    bsz, seq, _ = x.shape
    n = bsz * seq
    x2 = x.reshape(n, D_MODEL)

    mod = _adaln(c, w_ada[0], b_ada[0])
    mod3 = mod.reshape(bsz, 1, 6 * D_MODEL)

    idx = jnp.arange(GROUP) // HEAD_DIM
    e256 = (idx[:, None] == idx[None, :]).astype(BF16)
    tri = (jnp.arange(CHUNK)[:, None] >= jnp.arange(CHUNK)[None, :]).astype(F32)

    gain = jnp.concatenate([jnp.tile(q_norm_g[0], D_ATTN // HEAD_DIM) * (HEAD_DIM ** -0.5 * LOG2E),
                            jnp.tile(k_norm_g[0], D_ATTN // HEAD_DIM)]).reshape(1, 2 * D_ATTN)
    mu = mu_wag[0]
    lora_w = (_pad_cols(w1[0], LORA_W), _pad_cols(a1[0], LORA_W), _pad_cols(g1[0], LORA_G))
    w_lora = jnp.concatenate([(1.0 - mu[i])[:, None] * w for i, w in enumerate(lora_w)]
                             + [mu[i][:, None] * w for i, w in enumerate(lora_w)], axis=1)
    qkv, pr = _proj(x2, norm_mix_g, mod3, w_in[0], w_lora, gain, e256, seq)

    bias_tiles, lam = _attn_tables(rel_bias, lambda_q1, lambda_k1, lambda_q2, lambda_k2)
    o_attn = _diff_attn(qkv, bias_tiles, lam, attn_subln_g, bsz, seq)

    pvec = jnp.concatenate([mu_rkv[0], w0, a0, k_k, k_a, r_k.reshape(1, D_RWKV), ln_x_g, ln_x_b,
                            jnp.zeros((P_ROWS - 10, D_RWKV), F32)], axis=0)
    o_rwkv = _rwkv(pr, pvec, _pad_rows(w2[0], LORA_W).astype(BF16), _pad_rows(a2[0], LORA_W).astype(BF16),
                   _pad_rows(g2[0], LORA_G).astype(BF16), e256, tri, bsz, seq)

    x1, h2 = _out_proj(o_attn, o_rwkv, w_out[0].astype(BF16), x2, mod3, norm_ffn_g, seq)

    act = _ffn_up(h2, w_up[0], _pad_rows(conv_w[0], 8), conv_b, seq)
    out = _ffn_down(act, w_down[0], x1, mod3, seq)
    return out.reshape(bsz, seq, D_MODEL)
```

```python
import functools
import math

import jax
import jax.numpy as jnp
from jax import lax
from jax.experimental import pallas as pl
from jax.experimental.pallas import tpu as pltpu

F32 = jnp.float32
BF16 = jnp.bfloat16

D_MODEL = 2048
D_ATTN = 1024
D_RWKV = 1024
HEAD_DIM = 64
ATTN_HEADS = 8
ATTN_V_DIM = 128
RWKV_HEADS = 16
D_FF = 5632
NUM_BUCKETS = 32
MAX_EXACT = 16
MAX_DISTANCE = 128
NORM_EPS = 1e-6
RWKV_GN_EPS = 64e-5
NEG_INF = -1e30
LOG2E = math.log2(math.e)
LAMBDA_INIT = 0.8 - 0.6 * math.exp(0.0)

LANES = 128
LORA_W = 128
LORA_G = 256
LORA_COLS = 2 * LORA_W + LORA_G

TM = 1024
TN = 512
TM_DOWN = 512
TM_OUT = 512
TQ = 256
HEAD_PAIR = 4
V_ROWS = ATTN_V_DIM + 16
CHUNK = 64
GROUP = 256
N_GROUPS = D_RWKV // GROUP

VMEM_LIMIT = 56 * 1024 * 1024


def _mm(a, b):
    return jnp.dot(a.astype(BF16), b.astype(BF16), preferred_element_type=F32)


def _mm_nt(a, b):
    return lax.dot_general(a.astype(BF16), b.astype(BF16), (((1,), (1,)), ((), ())),
                           preferred_element_type=F32)


def _mm_tn(a, b):
    return lax.dot_general(a.astype(BF16), b.astype(BF16), (((0,), (0,)), ((), ())),
                           preferred_element_type=F32)


def _params(*sem):
    return pltpu.CompilerParams(dimension_semantics=sem, vmem_limit_bytes=VMEM_LIMIT)


def _adaln_kernel(c_ref, w_ref, b_ref, o_ref):
    c = c_ref[...]
    c_act = c * (1.0 / (1.0 + jnp.exp(-c)))
    acc = jnp.dot(c_act, w_ref[...], preferred_element_type=F32, precision=lax.Precision.HIGHEST)
    o_ref[...] = acc + b_ref[...]


def _adaln(c, w_ada, b_ada):
    bsz = c.shape[0]
    n_out = w_ada.shape[1]
    tn = 1024
    return pl.pallas_call(
        _adaln_kernel,
        grid=(n_out // tn,),
        in_specs=[pl.BlockSpec((bsz, D_MODEL), lambda j: (0, 0)),
                  pl.BlockSpec((D_MODEL, tn), lambda j: (0, j)),
                  pl.BlockSpec((1, tn), lambda j: (0, j))],
        out_specs=pl.BlockSpec((bsz, tn), lambda j: (0, j)),
        out_shape=jax.ShapeDtypeStruct((bsz, n_out), F32),
        compiler_params=_params("arbitrary"),
        name="adaln",
    )(c, w_ada, b_ada.reshape(1, n_out))


def _modulated_norm(x, g, scale, shift):
    y = x * lax.rsqrt(jnp.mean(x * x, axis=-1, keepdims=True) + NORM_EPS)
    return (y * g) * (1.0 + scale) + shift


QK_TILES = 2 * D_ATTN // TN
ATTN_TILES = 3 * D_ATTN // TN
W_IN_TILES = (3 * D_ATTN + 3 * D_RWKV) // TN
LORA_TILES = 2 * LORA_COLS // TN


def _proj_kernel(x_ref, g_ref, sc_ref, sh_ref, w_ref, wl_ref, gain_ref, e_ref, qkv_ref, pr_ref, h_ref):
    j = pl.program_id(1)

    @pl.when(j == 0)
    def _():
        h_ref[...] = _modulated_norm(x_ref[...], g_ref[...], sc_ref[...], sh_ref[...]).astype(BF16)

    def project(weights_ref):
        return jnp.dot(h_ref[...], weights_ref[...].astype(BF16), preferred_element_type=F32)

    @pl.when(j < QK_TILES)
    def _():
        acc = project(w_ref)
        half = TN // 2
        for s in range(2):
            a = acc[:, s * half:(s + 1) * half]
            ss = _mm(a * a, e_ref[...])
            y = a * lax.rsqrt(ss * (1.0 / HEAD_DIM) + NORM_EPS) * gain_ref[:, s * half:(s + 1) * half]
            qkv_ref[:, s * half:(s + 1) * half] = y.astype(qkv_ref.dtype)

    @pl.when(jnp.logical_and(j >= QK_TILES, j < ATTN_TILES))
    def _():
        qkv_ref[...] = project(w_ref).astype(qkv_ref.dtype)

    @pl.when(jnp.logical_and(j >= ATTN_TILES, j < W_IN_TILES))
    def _():
        pr_ref[...] = project(w_ref)

    @pl.when(j >= W_IN_TILES)
    def _():
        pr_ref[...] = project(wl_ref)


def _proj(x2, norm_g, mod3, w_in, w_lora, gain, e256, seq):
    n = x2.shape[0]
    tiles_per_batch = seq // TM
    batch = lambda i, j: i // tiles_per_batch
    return pl.pallas_call(
        _proj_kernel,
        grid=(n // TM, W_IN_TILES + LORA_TILES),
        in_specs=[pl.BlockSpec((TM, D_MODEL), lambda i, j: (i, 0)),
                  pl.BlockSpec((1, D_MODEL), lambda i, j: (0, 0)),
                  pl.BlockSpec((None, 1, D_MODEL), lambda i, j: (batch(i, j), 0, 1)),
                  pl.BlockSpec((None, 1, D_MODEL), lambda i, j: (batch(i, j), 0, 0)),
                  pl.BlockSpec((D_MODEL, TN), lambda i, j: (0, jnp.minimum(j, W_IN_TILES - 1))),
                  pl.BlockSpec((D_MODEL, TN), lambda i, j: (0, jnp.maximum(j - W_IN_TILES, 0))),
                  pl.BlockSpec((1, TN), lambda i, j: (0, jnp.minimum(j, QK_TILES - 1))),
                  pl.BlockSpec((TN // 2, TN // 2), lambda i, j: (0, 0))],
        out_specs=[pl.BlockSpec((TM, TN), lambda i, j: (i, jnp.minimum(j, ATTN_TILES - 1))),
                   pl.BlockSpec((TM, TN), lambda i, j: (i, jnp.maximum(j - ATTN_TILES, 0)))],
        out_shape=[jax.ShapeDtypeStruct((n, 3 * D_ATTN), BF16),
                   jax.ShapeDtypeStruct((n, 3 * D_RWKV + 2 * LORA_COLS), F32)],
        scratch_shapes=[pltpu.VMEM((TM, D_MODEL), BF16)],
        compiler_params=_params("arbitrary", "arbitrary"),
        name="proj",
    )(x2, norm_g, mod3, mod3, w_in, w_lora, gain, e256)


def _attn_tables_kernel(rb_ref, lq1_ref, lk1_ref, lq2_ref, lk2_ref, bias_ref, lam_ref):
    h = pl.program_id(0)
    rows = lax.broadcasted_iota(jnp.int32, (2 * TQ, TQ), 0)
    cols = lax.broadcasted_iota(jnp.int32, (2 * TQ, TQ), 1)
    far = rb_ref[NUM_BUCKETS - 1, h]
    dist = cols - rows + TQ
    n = jnp.maximum(dist, 0)
    nf = jnp.maximum(n, 1).astype(F32)
    large = MAX_EXACT + (jnp.log(nf / MAX_EXACT) / math.log(MAX_DISTANCE / MAX_EXACT)
                         * (NUM_BUCKETS - MAX_EXACT)).astype(jnp.int32)
    large = jnp.minimum(large, NUM_BUCKETS - 1)
    bucket = jnp.where(n < MAX_EXACT, n, large)
    bias = jnp.zeros((2 * TQ, TQ), F32)
    for b in range(NUM_BUCKETS):
        bias = jnp.where(bucket == b, rb_ref[b, h], bias)
    bias_ref[...] = jnp.where(dist >= 0, (bias - far) * LOG2E, NEG_INF)
    s1 = jnp.sum(lq1_ref[...] * lk1_ref[...], axis=1, keepdims=True)
    s2 = jnp.sum(lq2_ref[...] * lk2_ref[...], axis=1, keepdims=True)
    lam = jnp.exp(s1) - jnp.exp(s2) + LAMBDA_INIT
    lam_ref[...] = jnp.broadcast_to(lam, lam_ref.shape)


def _attn_tables(rel_bias, lq1, lk1, lq2, lk2):
    vec = pl.BlockSpec((1, HEAD_DIM), lambda h: (0, 0))
    return pl.pallas_call(
        _attn_tables_kernel,
        grid=(ATTN_HEADS,),
        in_specs=[pl.BlockSpec(memory_space=pltpu.SMEM), vec, vec, vec, vec],
        out_specs=[pl.BlockSpec((None, 2 * TQ, TQ), lambda h: (h, 0, 0)),
                   pl.BlockSpec((8, TQ), lambda h: (0, 0))],
        out_shape=[jax.ShapeDtypeStruct((ATTN_HEADS, 2 * TQ, TQ), F32),
                   jax.ShapeDtypeStruct((8, TQ), F32)],
        compiler_params=_params("arbitrary"),
        name="attn_tables",
    )(rel_bias, lq1, lk1, lq2, lk2)


def _attn_scores(qs_ref, k_ref, j):
    start = pl.multiple_of(j * TQ, TQ)
    return [lax.dot_general(k_ref[pl.ds(start, TQ), hh * ATTN_V_DIM:(hh + 1) * ATTN_V_DIM], qs_ref[hh],
                            (((1,), (1,)), ((), ())), preferred_element_type=F32)
            for hh in range(HEAD_PAIR)]


def _attn_update(vt_ref, m_ref, acc_ref, sts, j, biases):
    heads = range(HEAD_PAIR)
    if biases is not None:
        sts = [sts[hh] + jnp.concatenate([biases[hh], biases[hh]], axis=1) for hh in heads]
    m_prev = [m_ref[hh] for hh in heads]
    m_new = [jnp.maximum(m_prev[hh], jnp.max(sts[hh], axis=0, keepdims=True)) for hh in heads]
    alpha = [jnp.exp2(m_prev[hh] - m_new[hh]) for hh in heads]
    p = [jnp.exp2(sts[hh] - m_new[hh]) for hh in heads]
    start = pl.multiple_of(j * TQ, TQ)
    pv = [jnp.dot(vt_ref[hh, :, pl.ds(start, TQ)], p[hh].astype(BF16), preferred_element_type=F32)
          for hh in heads]
    for hh in heads:
        acc_ref[hh] = alpha[hh] * acc_ref[hh] + pv[hh]
        m_ref[hh] = m_new[hh]


def _diff_attn_kernel(q_ref, k_ref, v_ref, bias_ref, lam_ref, sg_ref, o_ref,
                      qs_ref, vt_ref, s_ref, m_ref, acc_ref, *, seq):
    qi = pl.program_id(2)
    heads = range(HEAD_PAIR)

    @pl.when(qi == 0)
    def _():
        step = 2 * TQ
        for hh in heads:
            for c in range(seq // step):
                v = v_ref[c * step:(c + 1) * step, hh * ATTN_V_DIM:(hh + 1) * ATTN_V_DIM]
                vt_ref[hh, 0:ATTN_V_DIM, c * step:(c + 1) * step] = v.astype(F32).T.astype(BF16)
            ones_row = lax.broadcasted_iota(jnp.int32, (V_ROWS - ATTN_V_DIM, seq), 0) == 0
            vt_ref[hh, ATTN_V_DIM:V_ROWS, :] = jnp.where(ones_row, 1.0, 0.0).astype(BF16)

    lane = lax.broadcasted_iota(jnp.int32, (TQ, ATTN_V_DIM), 1)
    for hh in heads:
        q = q_ref[:, hh * ATTN_V_DIM:(hh + 1) * ATTN_V_DIM]
        zero = jnp.zeros_like(q)
        qs_ref[hh, 0:TQ, :] = jnp.where(lane < HEAD_DIM, q, zero)
        qs_ref[hh, TQ:2 * TQ, :] = jnp.where(lane >= HEAD_DIM, q, zero)
    m_ref[...] = jnp.full(m_ref.shape, NEG_INF, F32)
    acc_ref[...] = jnp.zeros(acc_ref.shape, F32)
    scores = functools.partial(_attn_scores, qs_ref, k_ref)
    update = functools.partial(_attn_update, vt_ref, m_ref, acc_ref)

    def put(buf, sts):
        for hh in heads:
            s_ref[buf, hh] = sts[hh]

    get = lambda buf: [s_ref[buf, hh] for hh in heads]
    bias_prev = lambda: [bias_ref[hh, 0:TQ, :] for hh in heads]
    bias_diag = lambda: [bias_ref[hh, TQ:2 * TQ, :] for hh in heads]

    n_far = jnp.maximum(qi - 1, 0)
    put(0, scores(0))

    def far_pair(p, carry):
        j = 2 * p
        put(1, scores(j + 1))
        update(get(0), j, None)
        put(0, scores(j + 2))
        update(get(1), j + 1, None)
        return carry

    lax.fori_loop(0, n_far // 2, far_pair, 0)

    @pl.when(n_far % 2 == 1)
    def _():
        update(get(0), n_far - 1, None)
        put(0, scores(n_far))

    @pl.when(qi >= 1)
    def _():
        put(1, scores(qi))
        update(get(0), qi - 1, bias_prev())
        update(get(1), qi, bias_diag())

    @pl.when(qi == 0)
    def _():
        update(get(0), qi, bias_diag())

    for hh in heads:
        acc = acc_ref[hh]
        ot = acc[0:ATTN_V_DIM] / acc[ATTN_V_DIM:ATTN_V_DIM + 1]
        dt = ot[:, 0:TQ] - lam_ref[0:1, :] * ot[:, TQ:2 * TQ]
        yt = dt * lax.rsqrt(jnp.mean(dt * dt, axis=0, keepdims=True) + NORM_EPS)
        o_ref[:, hh * ATTN_V_DIM:(hh + 1) * ATTN_V_DIM] = (
            yt.T * sg_ref[...] * (1.0 - LAMBDA_INIT)).astype(o_ref.dtype)


def _diff_attn(qkv, bias_tiles, lam, subln_g, bsz, seq):
    nq = seq // TQ
    width = HEAD_PAIR * ATTN_V_DIM
    h_blocks = D_ATTN // width
    return pl.pallas_call(
        functools.partial(_diff_attn_kernel, seq=seq),
        grid=(bsz, ATTN_HEADS // HEAD_PAIR, nq),
        in_specs=[pl.BlockSpec((TQ, width), lambda b, h, i: (b * nq + i, h)),
                  pl.BlockSpec((seq, width), lambda b, h, i: (b, h_blocks + h)),
                  pl.BlockSpec((seq, width), lambda b, h, i: (b, 2 * h_blocks + h)),
                  pl.BlockSpec((HEAD_PAIR, 2 * TQ, TQ), lambda b, h, i: (h, 0, 0)),
                  pl.BlockSpec((8, TQ), lambda b, h, i: (0, 0)),
                  pl.BlockSpec((1, ATTN_V_DIM), lambda b, h, i: (0, 0))],
        out_specs=pl.BlockSpec((TQ, width), lambda b, h, i: (b * nq + i, h)),
        out_shape=jax.ShapeDtypeStruct((bsz * seq, D_ATTN), BF16),
        scratch_shapes=[pltpu.VMEM((HEAD_PAIR, 2 * TQ, ATTN_V_DIM), BF16),
                        pltpu.VMEM((HEAD_PAIR, V_ROWS, seq), BF16),
                        pltpu.VMEM((2, HEAD_PAIR, TQ, 2 * TQ), F32),
                        pltpu.VMEM((HEAD_PAIR, 1, 2 * TQ), F32),
                        pltpu.VMEM((HEAD_PAIR, V_ROWS, 2 * TQ), F32)],
        compiler_params=_params("arbitrary", "arbitrary", "arbitrary"),
        name="diff_attn",
    )(qkv, qkv, qkv, bias_tiles, lam, subln_g)


P_MU_R, P_MU_K, P_MU_V, P_W0, P_A0, P_KK, P_KA, P_RK, P_LNG, P_LNB = range(10)
P_ROWS = 16


S_AT, S_RT, S_BT, S_KT, S_BREM, S_KREM, S_V = range(7)
S_BONUS, S_GATE = range(2)
M_AM, M_ARB, M_ARK, M_RT, M_V, M_BREM, M_KREM = range(7)
F_U0, F_BONUS, F_GATE = range(3)


def _rwkv_prep(g, rp_ref, kp_ref, vp_ref, lo_ref, pv_ref, w2_ref, a2_ref, g2_ref, e_ref, tri_ref, prev_ref,
               ops_ref, epi_ref, dec_ref):
    cols = slice(g * GROUP, (g + 1) * GROUP)

    def shifted(x, prev):
        row0 = lax.broadcasted_iota(jnp.int32, x.shape, 0) == 0
        return jnp.where(row0, prev, pltpu.roll(x, 1, axis=0))

    def mixed(ref, col, mu_row):
        x = ref[:, cols]
        prev = prev_ref[0:1, col * D_RWKV + g * GROUP:col * D_RWKV + (g + 1) * GROUP]
        return x + (shifted(x, prev) - x) * pv_ref[mu_row:mu_row + 1, cols]

    pv = lambda i: pv_ref[i:i + 1, cols]
    r = mixed(rp_ref, 0, P_MU_R)
    k = mixed(kp_ref, 1, P_MU_K)
    v = mixed(vp_ref, 2, P_MU_V)

    lora = lo_ref[:, :LORA_COLS] + shifted(lo_ref[:, LORA_COLS:], prev_ref[0:1, 3 * D_RWKV + LORA_COLS:])
    w_in = pv(P_W0) + _mm(jnp.tanh(lora[:, :LORA_W]), w2_ref[:, cols])
    w = -(jnp.maximum(-w_in, 0.0) + jnp.log(1.0 + jnp.exp(-jnp.abs(w_in)))) - 0.5
    wlog = -jnp.exp(w)
    a = 1.0 / (1.0 + jnp.exp(-(pv(P_A0) + _mm(lora[:, LORA_W:2 * LORA_W], a2_ref[:, cols]))))
    gate = _mm(1.0 / (1.0 + jnp.exp(-lora[:, 2 * LORA_W:])), g2_ref[:, cols])

    head_sum = lambda x: _mm(x, e_ref[...])
    kk = k * pv(P_KK)
    kk = kk / jnp.maximum(jnp.sqrt(head_sum(kk * kk)), 1e-12)
    k2 = k * (1.0 + (a - 1.0) * pv(P_KA))
    b = kk * a
    cum = jnp.dot(tri_ref[...], wlog, preferred_element_type=F32, precision=lax.Precision.HIGHEST)

    tot = cum[CHUNK - 1:CHUNK, :]
    e_neg = jnp.exp(-cum)
    e_rem = jnp.exp(tot - cum)
    staged = {S_AT: jnp.exp(cum - wlog) * (-kk), S_RT: jnp.exp(cum) * r, S_BT: e_neg * b, S_KT: e_neg * k2,
              S_BREM: e_rem * b, S_KREM: e_rem * k2, S_V: v}
    for slot, val in staged.items():
        ops_ref[slot, :, cols] = val.astype(BF16)
    epi_ref[S_BONUS, :, cols] = head_sum(r * k2 * pv(P_RK)) * v
    epi_ref[S_GATE, :, cols] = gate
    dec_ref[:, cols] = jnp.broadcast_to(jnp.exp(tot), (dec_ref.shape[0], GROUP))


def _head_sum(x, e_ref):
    return jnp.concatenate([_mm(x[:, i * GROUP:(i + 1) * GROUP], e_ref[...]) for i in range(N_GROUPS)],
                           axis=1)


def _rwkv_tables():
    rows_bd = lax.broadcasted_iota(jnp.int32, (GROUP, GROUP), 0) // HEAD_DIM
    lanes_bd = lax.broadcasted_iota(jnp.int32, (GROUP, GROUP), 1) // HEAD_DIM
    bd_mask = rows_bd == lanes_bd
    t_idx = lax.broadcasted_iota(jnp.int32, (CHUNK, GROUP), 0)
    s_idx = lax.broadcasted_iota(jnp.int32, (CHUNK, GROUP), 1) % HEAD_DIM

    def bd(x):
        return jnp.where(bd_mask, jnp.concatenate([x] * (GROUP // HEAD_DIM), axis=0), jnp.zeros((), x.dtype))

    return bd, s_idx < t_idx, s_idx <= t_idx, jnp.where(s_idx == t_idx, 1.0, 0.0).astype(F32)


def _rwkv_solve(ops_ref, epi_ref, dec_ref, mid_ref, midf_ref, mdec_ref):
    groups = range(N_GROUPS)
    cols = lambda g: slice(g * GROUP, (g + 1) * GROUP)
    grp = lambda slot, g: ops_ref[slot, :, cols(g)]
    bd, m_strict, m_incl, eye = _rwkv_tables()

    at = [grp(S_AT, g) for g in groups]
    aa = [_mm_nt(jnp.concatenate([at[g], grp(S_RT, g)], axis=0),
                 jnp.concatenate([bd(grp(S_BT, g)), bd(grp(S_KT, g))], axis=0)) for g in groups]
    a_ab = [jnp.where(m_strict, aa[g][:CHUNK, :GROUP], 0.0) for g in groups]
    a_ak = [jnp.where(m_strict, aa[g][:CHUNK, GROUP:], 0.0) for g in groups]
    for g in groups:
        mid_ref[M_ARB, :, cols(g)] = jnp.where(m_incl, aa[g][CHUNK:, :GROUP], 0.0).astype(BF16)
        mid_ref[M_ARK, :, cols(g)] = jnp.where(m_incl, aa[g][CHUNK:, GROUP:], 0.0).astype(BF16)
    yield

    minv = [eye + a_ab[g] for g in groups]
    nk = [_mm(a_ab[g], bd(a_ab[g])) for g in groups]
    yield
    for _ in range(int(math.log2(CHUNK)) - 2):
        res = [_mm(jnp.concatenate([nk[g], minv[g]], axis=0), bd(nk[g])) for g in groups]
        nk = [res[g][:CHUNK] for g in groups]
        minv = [minv[g] + res[g][CHUNK:] for g in groups]
        yield
    minv = [minv[g] + _mm(minv[g], bd(nk[g])) for g in groups]
    yield

    x1 = [_mm(a_ak[g], bd(grp(S_V, g))) for g in groups]
    yield
    ma = [_mm(minv[g], jnp.concatenate([bd(at[g]), bd(x1[g].astype(BF16))], axis=1)) for g in groups]
    for g in groups:
        mid_ref[M_AM, :, cols(g)] = ma[g][:, :GROUP].astype(BF16)
        midf_ref[F_U0, :, cols(g)] = ma[g][:, GROUP:]
    for src, dst in ((S_RT, M_RT), (S_V, M_V), (S_BREM, M_BREM), (S_KREM, M_KREM)):
        mid_ref[dst] = ops_ref[src]
    midf_ref[F_BONUS] = epi_ref[S_BONUS]
    midf_ref[F_GATE] = epi_ref[S_GATE]
    mdec_ref[...] = dec_ref[...]
    yield


def _rwkv_advance(mid_ref, midf_ref, mdec_ref, pv_ref, e_ref, st_ref, o_ref):
    groups = range(N_GROUPS)
    heads = GROUP // HEAD_DIM
    cols = lambda g: slice(g * GROUP, (g + 1) * GROUP)
    grp = lambda slot, g: mid_ref[slot, :, cols(g)]
    bd, _, _, _ = _rwkv_tables()

    st = [st_ref[g] for g in groups]
    c1 = [_mm_nt(jnp.concatenate([grp(M_AM, g), grp(M_RT, g)], axis=0), bd(st[g])) for g in groups]
    yield
    u = [c1[g][:CHUNK] + midf_ref[F_U0, :, cols(g)] for g in groups]
    uv = [jnp.concatenate([bd(u[g].astype(BF16)), bd(grp(M_V, g))], axis=0) for g in groups]
    y = [c1[g][CHUNK:] + _mm(jnp.concatenate([grp(M_ARB, g), grp(M_ARK, g)], axis=1), uv[g]) for g in groups]
    upd = [_mm_tn(uv[g], jnp.concatenate([bd(grp(M_BREM, g)), bd(grp(M_KREM, g))], axis=0))
           for g in groups]
    yield
    for g in groups:
        folded = sum(upd[g][i * HEAD_DIM:(i + 1) * HEAD_DIM] for i in range(heads))
        st_ref[g] = st[g] * mdec_ref[0:1, cols(g)] + folded
    y = jnp.concatenate(y, axis=1)
    pv = lambda i: pv_ref[i:i + 1, :]
    mu = _head_sum(y, e_ref) * (1.0 / HEAD_DIM)
    yield
    d = y - mu
    var = _head_sum(d * d, e_ref) * (1.0 / HEAD_DIM)
    yield
    yn = d * lax.rsqrt(var + RWKV_GN_EPS) * pv(P_LNG) + pv(P_LNB)
    o_ref[...] = ((yn + midf_ref[F_BONUS]) * midf_ref[F_GATE]).astype(o_ref.dtype)
    yield


def _interleave(*stages):
    live = list(stages)
    while live:
        for s in list(live):
            try:
                next(s)
            except StopIteration:
                live.remove(s)


def _rwkv_kernel(rp_ref, kp_ref, vp_ref, lo_ref, pv_ref, w2_ref, a2_ref, g2_ref, e_ref, tri_ref,
                 o_ref, st_ref, prev_ref, ops_ref, epi_ref, dec_ref, mid_ref, midf_ref, mdec_ref):
    c = pl.program_id(1)

    @pl.when(c == 0)
    def _():
        for ref in (st_ref, prev_ref, epi_ref, dec_ref, midf_ref, mdec_ref):
            ref[...] = jnp.zeros(ref.shape, F32)
        ops_ref[...] = jnp.zeros(ops_ref.shape, BF16)
        mid_ref[...] = jnp.zeros(mid_ref.shape, BF16)

    def prep_stage(slot):
        for g in range(N_GROUPS):
            _rwkv_prep(g, rp_ref, kp_ref, vp_ref, lo_ref, pv_ref, w2_ref, a2_ref, g2_ref, e_ref, tri_ref,
                       prev_ref, ops_ref.at[slot], epi_ref.at[slot], dec_ref.at[slot])
            yield

    def step(new, old):
        _interleave(_rwkv_solve(ops_ref.at[old], epi_ref.at[old], dec_ref.at[old],
                                mid_ref.at[old], midf_ref.at[old], mdec_ref.at[old]),
                    _rwkv_advance(mid_ref.at[new], midf_ref.at[new], mdec_ref.at[new], pv_ref, e_ref,
                                  st_ref, o_ref),
                    prep_stage(new))
        for col, ref in enumerate((rp_ref, kp_ref, vp_ref, lo_ref)):
            prev_ref[0:1, col * D_RWKV:(col + 1) * D_RWKV] = ref[CHUNK - 1:CHUNK, :]

    @pl.when(c % 2 == 0)
    def _():
        step(0, 1)

    @pl.when(c % 2 == 1)
    def _():
        step(1, 0)


def _rwkv(pr, pvec, w2p, a2p, g2p, e256, tri, bsz, seq):
    nc = seq // CHUNK
    col = lambda cb: pl.BlockSpec((CHUNK, D_RWKV), lambda b, c: (b * nc + jnp.minimum(c, nc - 1), cb))
    full = lambda shape: pl.BlockSpec(shape, lambda b, c: (0, 0))
    return pl.pallas_call(
        _rwkv_kernel,
        grid=(bsz, nc + 2),
        in_specs=[col(0), col(1), col(2), col(3),
                  full((P_ROWS, D_RWKV)), full((LORA_W, D_RWKV)), full((LORA_W, D_RWKV)),
                  full((LORA_G, D_RWKV)), full((GROUP, GROUP)), full((CHUNK, CHUNK))],
        out_specs=pl.BlockSpec((CHUNK, D_RWKV), lambda b, c: (b * nc + jnp.maximum(c - 2, 0), 0)),
        out_shape=jax.ShapeDtypeStruct((bsz * seq, D_RWKV), BF16),
        scratch_shapes=[pltpu.VMEM((N_GROUPS, HEAD_DIM, GROUP), F32),
                        pltpu.VMEM((8, 4 * D_RWKV), F32),
                        pltpu.VMEM((2, 7, CHUNK, D_RWKV), BF16),
                        pltpu.VMEM((2, 2, CHUNK, D_RWKV), F32),
                        pltpu.VMEM((2, 8, D_RWKV), F32),
                        pltpu.VMEM((2, 7, CHUNK, D_RWKV), BF16),
                        pltpu.VMEM((2, 3, CHUNK, D_RWKV), F32),
                        pltpu.VMEM((2, 8, D_RWKV), F32)],
        compiler_params=_params("arbitrary", "arbitrary"),
        name="rwkv7",
    )(pr, pr, pr, pr, pvec, w2p, a2p, g2p, e256, tri)


def _out_proj_kernel(oa_ref, or_ref, w_ref, x_ref, gate_ref, g_ref, sc_ref, sh_ref, x1_ref, h2_ref):
    acc = jnp.dot(oa_ref[...], w_ref[0:D_ATTN, :], preferred_element_type=F32)
    acc = acc + jnp.dot(or_ref[...], w_ref[D_ATTN:, :], preferred_element_type=F32)
    x1 = x_ref[...] + gate_ref[...] * acc
    x1_ref[...] = x1
    h2_ref[...] = _modulated_norm(x1, g_ref[...], sc_ref[...], sh_ref[...]).astype(BF16)


def _out_proj(o_attn, o_rwkv, w_bf, x2, mod3, norm_g, seq):
    n = x2.shape[0]
    tiles_per_batch = seq // TM_OUT
    mod = lambda col: pl.BlockSpec((None, 1, D_MODEL), lambda i: (i // tiles_per_batch, 0, col))
    return pl.pallas_call(
        _out_proj_kernel,
        grid=(n // TM_OUT,),
        in_specs=[pl.BlockSpec((TM_OUT, D_ATTN), lambda i: (i, 0)),
                  pl.BlockSpec((TM_OUT, D_RWKV), lambda i: (i, 0)),
                  pl.BlockSpec((D_MODEL, D_MODEL), lambda i: (0, 0), pipeline_mode=pl.Buffered(1)),
                  pl.BlockSpec((TM_OUT, D_MODEL), lambda i: (i, 0)),
                  mod(2),
                  pl.BlockSpec((1, D_MODEL), lambda i: (0, 0)),
                  mod(4), mod(3)],
        out_specs=[pl.BlockSpec((TM_OUT, D_MODEL), lambda i: (i, 0)),
                   pl.BlockSpec((TM_OUT, D_MODEL), lambda i: (i, 0))],
        out_shape=[jax.ShapeDtypeStruct((n, D_MODEL), F32),
                   jax.ShapeDtypeStruct((n, D_MODEL), BF16)],
        compiler_params=_params("arbitrary"),
        name="out_proj",
    )(o_attn, o_rwkv, w_bf, x2, mod3, norm_g, mod3, mod3)


def _ffn_up_kernel(h_ref, wg_ref, wv_ref, cwg_ref, cwv_ref, cbg_ref, cbv_ref, o_ref,
                   wgb_ref, wvb_ref, carry_ref, *, tiles_per_batch):
    i = pl.program_id(1)

    @pl.when(i == 0)
    def _():
        wgb_ref[...] = wg_ref[...].astype(BF16)
        wvb_ref[...] = wv_ref[...].astype(BF16)
        carry_ref[...] = jnp.zeros(carry_ref.shape, F32)

    first = (i % tiles_per_batch) == 0

    def conv(up, prev, cw_ref, cb_ref):
        def taps(x, x1, x2):
            return cb_ref[...] + cw_ref[0:1, :] * x2 + cw_ref[1:2, :] * x1 + cw_ref[2:3, :] * x

        y = taps(up, pltpu.roll(up, 1, axis=0), pltpu.roll(up, 2, axis=0))
        ext = jnp.concatenate([prev, up[0:8, :]], axis=0)
        head = taps(ext, pltpu.roll(ext, 1, axis=0), pltpu.roll(ext, 2, axis=0))[8:16, :]
        return jnp.concatenate([head, y[8:, :]], axis=0)

    h = h_ref[...]
    up_g = jnp.dot(h, wgb_ref[...], preferred_element_type=F32)
    up_v = jnp.dot(h, wvb_ref[...], preferred_element_type=F32)
    gate = conv(up_g, jnp.where(first, 0.0, carry_ref[0]), cwg_ref, cbg_ref)
    val = conv(up_v, jnp.where(first, 0.0, carry_ref[1]), cwv_ref, cbv_ref)
    carry_ref[0] = up_g[TM - 8:, :]
    carry_ref[1] = up_v[TM - 8:, :]
    o_ref[...] = (gate * (1.0 / (1.0 + jnp.exp(-gate))) * val).astype(o_ref.dtype)


def _ffn_up(h2, w_up, conv_w8, conv_b, seq):
    n = h2.shape[0]
    tiles_per_batch = seq // TM
    nf = D_FF // TN
    return pl.pallas_call(
        functools.partial(_ffn_up_kernel, tiles_per_batch=tiles_per_batch),
        grid=(nf, n // TM),
        in_specs=[pl.BlockSpec((TM, D_MODEL), lambda j, i: (i, 0)),
                  pl.BlockSpec((D_MODEL, TN), lambda j, i: (0, j)),
                  pl.BlockSpec((D_MODEL, TN), lambda j, i: (0, nf + j)),
                  pl.BlockSpec((8, TN), lambda j, i: (0, j)),
                  pl.BlockSpec((8, TN), lambda j, i: (0, nf + j)),
                  pl.BlockSpec((1, TN), lambda j, i: (0, j)),
                  pl.BlockSpec((1, TN), lambda j, i: (0, nf + j))],
        out_specs=pl.BlockSpec((TM, TN), lambda j, i: (i, j)),
        out_shape=jax.ShapeDtypeStruct((n, D_FF), BF16),
        scratch_shapes=[pltpu.VMEM((D_MODEL, TN), BF16),
                        pltpu.VMEM((D_MODEL, TN), BF16),
                        pltpu.VMEM((2, 8, TN), F32)],
        compiler_params=_params("arbitrary", "arbitrary"),
        name="ffn_up",
    )(h2, w_up, w_up, conv_w8, conv_w8, conv_b, conv_b)


def _ffn_down_kernel(a_ref, w_ref, x_ref, gate_ref, o_ref, wb_ref):
    @pl.when(pl.program_id(1) == 0)
    def _():
        wb_ref[...] = w_ref[...].astype(BF16)

    acc = jnp.dot(a_ref[...], wb_ref[...], preferred_element_type=F32)
    o_ref[...] = x_ref[...] + gate_ref[...] * acc


def _ffn_down(act, w_down, x1, mod3, seq):
    n = x1.shape[0]
    tiles_per_batch = seq // TM_DOWN
    gate_col = 5 * (D_MODEL // TN)
    return pl.pallas_call(
        _ffn_down_kernel,
        grid=(D_MODEL // TN, n // TM_DOWN),
        in_specs=[pl.BlockSpec((TM_DOWN, D_FF), lambda j, i: (i, 0)),
                  pl.BlockSpec((D_FF, TN), lambda j, i: (0, j)),
                  pl.BlockSpec((TM_DOWN, TN), lambda j, i: (i, j)),
                  pl.BlockSpec((None, 1, TN), lambda j, i: (i // tiles_per_batch, 0, gate_col + j))],
        out_specs=pl.BlockSpec((TM_DOWN, TN), lambda j, i: (i, j)),
        out_shape=jax.ShapeDtypeStruct((n, D_MODEL), F32),
        scratch_shapes=[pltpu.VMEM((D_FF, TN), BF16)],
        compiler_params=_params("arbitrary", "arbitrary"),
        name="ffn_down",
    )(act, w_down, x1, mod3)


def _pad_cols(w, width):
    return jnp.pad(w, ((0, 0), (0, width - w.shape[1])))


def _pad_rows(w, height):
    return jnp.pad(w, ((0, height - w.shape[0]), (0, 0)))


def kernel(x, c, rel_bias, w_ada, b_ada, norm_mix_g, w_in, q_norm_g, k_norm_g, lambda_q1, lambda_k1,
           lambda_q2, lambda_k2, attn_subln_g, mu_rkv, mu_wag, w0, w1, w2, a0, a1, a2, g1, g2, k_k,
           k_a, r_k, ln_x_g, ln_x_b, w_out, norm_ffn_g, w_up, conv_w, conv_b, w_down):
    bsz, seq, _ = x.shape
    n = bsz * seq
    x2 = x.reshape(n, D_MODEL)

    mod = _adaln(c, w_ada[0], b_ada[0])
    mod3 = mod.reshape(bsz, 1, 6 * D_MODEL)

    idx = jnp.arange(GROUP) // HEAD_DIM
    e256 = (idx[:, None] == idx[None, :]).astype(BF16)
    tri = (jnp.arange(CHUNK)[:, None] >= jnp.arange(CHUNK)[None, :]).astype(F32)

    gain = jnp.concatenate([jnp.tile(q_norm_g[0], D_ATTN // HEAD_DIM) * (HEAD_DIM ** -0.5 * LOG2E),
                            jnp.tile(k_norm_g[0], D_ATTN // HEAD_DIM)]).reshape(1, 2 * D_ATTN)
    mu = mu_wag[0]
    lora_w = (_pad_cols(w1[0], LORA_W), _pad_cols(a1[0], LORA_W), _pad_cols(g1[0], LORA_G))
    w_lora = jnp.concatenate([(1.0 - mu[i])[:, None] * w for i, w in enumerate(lora_w)]
                             + [mu[i][:, None] * w for i, w in enumerate(lora_w)], axis=1)
    qkv, pr = _proj(x2, norm_mix_g, mod3, w_in[0], w_lora, gain, e256, seq)

    bias_tiles, lam = _attn_tables(rel_bias, lambda_q1, lambda_k1, lambda_q2, lambda_k2)
    o_attn = _diff_attn(qkv, bias_tiles, lam, attn_subln_g, bsz, seq)

    pvec = jnp.concatenate([mu_rkv[0], w0, a0, k_k, k_a, r_k.reshape(1, D_RWKV), ln_x_g, ln_x_b,
                            jnp.zeros((P_ROWS - 10, D_RWKV), F32)], axis=0)
    o_rwkv = _rwkv(pr, pvec, _pad_rows(w2[0], LORA_W).astype(BF16), _pad_rows(a2[0], LORA_W).astype(BF16),
                   _pad_rows(g2[0], LORA_G).astype(BF16), e256, tri, bsz, seq)

    x1, h2 = _out_proj(o_attn, o_rwkv, w_out[0].astype(BF16), x2, mod3, norm_ffn_g, seq)

    act = _ffn_up(h2, w_up[0], _pad_rows(conv_w[0], 8), conv_b, seq)
    out = _ffn_down(act, w_down[0], x1, mod3, seq)
    return out.reshape(bsz, seq, D_MODEL)
```

```python
import functools
import math

import jax
import jax.numpy as jnp
from jax import lax
from jax.experimental import pallas as pl
from jax.experimental.pallas import tpu as pltpu

F32 = jnp.float32
BF16 = jnp.bfloat16

D_MODEL = 2048
D_ATTN = 1024
D_RWKV = 1024
HEAD_DIM = 64
ATTN_HEADS = 8
ATTN_V_DIM = 128
RWKV_HEADS = 16
D_FF = 5632
NUM_BUCKETS = 32
MAX_EXACT = 16
MAX_DISTANCE = 128
NORM_EPS = 1e-6
RWKV_GN_EPS = 64e-5
NEG_INF = -1e30
LOG2E = math.log2(math.e)
LAMBDA_INIT = 0.8 - 0.6 * math.exp(0.0)

LANES = 128
LORA_W = 128
LORA_G = 256
LORA_COLS = 2 * LORA_W + LORA_G

TM = 1024
TN = 512
TM_DOWN = 512
TM_OUT = 512
TQ = 256
HEADS_PER_STEP = 4
V_ROWS = ATTN_V_DIM + 16
CHUNK = 64
GROUP = 256
N_GROUPS = D_RWKV // GROUP

VMEM_LIMIT = 56 * 1024 * 1024


def _mm(a, b):
    return jnp.dot(a.astype(BF16), b.astype(BF16), preferred_element_type=F32)


def _mm_nt(a, b):
    return lax.dot_general(a.astype(BF16), b.astype(BF16), (((1,), (1,)), ((), ())),
                           preferred_element_type=F32)


def _mm_tn(a, b):
    return lax.dot_general(a.astype(BF16), b.astype(BF16), (((0,), (0,)), ((), ())),
                           preferred_element_type=F32)


def _params(*sem):
    return pltpu.CompilerParams(dimension_semantics=sem, vmem_limit_bytes=VMEM_LIMIT)


def _adaln_kernel(c_ref, w_ref, b_ref, o_ref):
    c = c_ref[...]
    c_act = c * (1.0 / (1.0 + jnp.exp(-c)))
    w = w_ref[...]
    w_hi = w.astype(BF16)
    w_lo = (w - w_hi.astype(F32)).astype(BF16)
    c_hi = c_act.astype(BF16)
    c_lo = (c_act - c_hi.astype(F32)).astype(BF16)
    rows = c.shape[0]
    t = jnp.dot(jnp.concatenate([c_hi, c_lo], axis=0), w_hi, preferred_element_type=F32)
    acc = t[:rows] + t[rows:] + jnp.dot(c_hi, w_lo, preferred_element_type=F32)
    o_ref[...] = acc + b_ref[...]


def _adaln(c, w_ada, b_ada):
    bsz = c.shape[0]
    n_out = w_ada.shape[1]
    tn = 1024
    return pl.pallas_call(
        _adaln_kernel,
        grid=(n_out // tn,),
        in_specs=[pl.BlockSpec((bsz, D_MODEL), lambda j: (0, 0)),
                  pl.BlockSpec((D_MODEL, tn), lambda j: (0, j)),
                  pl.BlockSpec((1, tn), lambda j: (0, j))],
        out_specs=pl.BlockSpec((bsz, tn), lambda j: (0, j)),
        out_shape=jax.ShapeDtypeStruct((bsz, n_out), F32),
        compiler_params=_params("arbitrary"),
        name="adaln",
    )(c, w_ada, b_ada.reshape(1, n_out))


def _modulated_norm(x, g, scale, shift):
    y = x * lax.rsqrt(jnp.mean(x * x, axis=-1, keepdims=True) + NORM_EPS)
    return (y * g) * (1.0 + scale) + shift


QK_TILES = 2 * D_ATTN // TN
ATTN_TILES = 3 * D_ATTN // TN
W_IN_TILES = (3 * D_ATTN + 3 * D_RWKV) // TN
LORA_TILES = 2 * LORA_COLS // TN


def _proj_kernel(x_ref, g_ref, sc_ref, sh_ref, w_ref, wl_ref, gain_ref, e_ref, qkv_ref, pr_ref, h_ref):
    j = pl.program_id(1)

    @pl.when(j == 0)
    def _():
        h_ref[...] = _modulated_norm(x_ref[...], g_ref[...], sc_ref[...], sh_ref[...]).astype(BF16)

    def project(weights_ref):
        return jnp.dot(h_ref[...], weights_ref[...].astype(BF16), preferred_element_type=F32)

    @pl.when(j < QK_TILES)
    def _():
        acc = project(w_ref)
        half = TN // 2
        for s in range(2):
            a = acc[:, s * half:(s + 1) * half]
            ss = _mm(a * a, e_ref[...])
            y = a * lax.rsqrt(ss * (1.0 / HEAD_DIM) + NORM_EPS) * gain_ref[:, s * half:(s + 1) * half]
            qkv_ref[:, s * half:(s + 1) * half] = y.astype(qkv_ref.dtype)

    @pl.when(jnp.logical_and(j >= QK_TILES, j < ATTN_TILES))
    def _():
        qkv_ref[...] = project(w_ref).astype(qkv_ref.dtype)

    @pl.when(jnp.logical_and(j >= ATTN_TILES, j < W_IN_TILES))
    def _():
        pr_ref[...] = project(w_ref)

    @pl.when(j >= W_IN_TILES)
    def _():
        pr_ref[...] = project(wl_ref)


def _proj(x2, norm_g, mod3, w_in, w_lora, gain, e256, seq):
    n = x2.shape[0]
    tiles_per_batch = seq // TM
    batch = lambda i, j: i // tiles_per_batch
    return pl.pallas_call(
        _proj_kernel,
        grid=(n // TM, W_IN_TILES + LORA_TILES),
        in_specs=[pl.BlockSpec((TM, D_MODEL), lambda i, j: (i, 0)),
                  pl.BlockSpec((1, D_MODEL), lambda i, j: (0, 0)),
                  pl.BlockSpec((None, 1, D_MODEL), lambda i, j: (batch(i, j), 0, 1)),
                  pl.BlockSpec((None, 1, D_MODEL), lambda i, j: (batch(i, j), 0, 0)),
                  pl.BlockSpec((D_MODEL, TN), lambda i, j: (0, jnp.minimum(j, W_IN_TILES - 1))),
                  pl.BlockSpec((D_MODEL, TN), lambda i, j: (0, jnp.maximum(j - W_IN_TILES, 0))),
                  pl.BlockSpec((1, TN), lambda i, j: (0, jnp.minimum(j, QK_TILES - 1))),
                  pl.BlockSpec((TN // 2, TN // 2), lambda i, j: (0, 0))],
        out_specs=[pl.BlockSpec((TM, TN), lambda i, j: (i, jnp.minimum(j, ATTN_TILES - 1))),
                   pl.BlockSpec((TM, TN), lambda i, j: (i, jnp.maximum(j - ATTN_TILES, 0)))],
        out_shape=[jax.ShapeDtypeStruct((n, 3 * D_ATTN), BF16),
                   jax.ShapeDtypeStruct((n, 3 * D_RWKV + 2 * LORA_COLS), F32)],
        scratch_shapes=[pltpu.VMEM((TM, D_MODEL), BF16)],
        compiler_params=_params("arbitrary", "arbitrary"),
        name="proj",
    )(x2, norm_g, mod3, mod3, w_in, w_lora, gain, e256)


def _attn_tables_kernel(rb_ref, lq1_ref, lk1_ref, lq2_ref, lk2_ref, bias_ref, lam_ref):
    h = pl.program_id(0)
    rows = lax.broadcasted_iota(jnp.int32, (2 * TQ, TQ), 0)
    cols = lax.broadcasted_iota(jnp.int32, (2 * TQ, TQ), 1)
    far = rb_ref[NUM_BUCKETS - 1, h]
    dist = cols - rows + TQ
    n = jnp.maximum(dist, 0)
    nf = jnp.maximum(n, 1).astype(F32)
    large = MAX_EXACT + (jnp.log(nf / MAX_EXACT) / math.log(MAX_DISTANCE / MAX_EXACT)
                         * (NUM_BUCKETS - MAX_EXACT)).astype(jnp.int32)
    large = jnp.minimum(large, NUM_BUCKETS - 1)
    bucket = jnp.where(n < MAX_EXACT, n, large)
    bias = jnp.zeros((2 * TQ, TQ), F32)
    for b in range(NUM_BUCKETS):
        bias = jnp.where(bucket == b, rb_ref[b, h], bias)
    bias_ref[...] = jnp.where(dist >= 0, (bias - far) * LOG2E, NEG_INF)
    s1 = jnp.sum(lq1_ref[...] * lk1_ref[...], axis=1, keepdims=True)
    s2 = jnp.sum(lq2_ref[...] * lk2_ref[...], axis=1, keepdims=True)
    lam = jnp.exp(s1) - jnp.exp(s2) + LAMBDA_INIT
    lam_ref[...] = jnp.broadcast_to(lam, lam_ref.shape)


def _attn_tables(rel_bias, lq1, lk1, lq2, lk2):
    vec = pl.BlockSpec((1, HEAD_DIM), lambda h: (0, 0))
    return pl.pallas_call(
        _attn_tables_kernel,
        grid=(ATTN_HEADS,),
        in_specs=[pl.BlockSpec(memory_space=pltpu.SMEM), vec, vec, vec, vec],
        out_specs=[pl.BlockSpec((None, 2 * TQ, TQ), lambda h: (h, 0, 0)),
                   pl.BlockSpec((8, TQ), lambda h: (0, 0))],
        out_shape=[jax.ShapeDtypeStruct((ATTN_HEADS, 2 * TQ, TQ), F32),
                   jax.ShapeDtypeStruct((8, TQ), F32)],
        compiler_params=_params("arbitrary"),
        name="attn_tables",
    )(rel_bias, lq1, lk1, lq2, lk2)


def _attn_scores(qs_ref, k_ref, j):
    start = pl.multiple_of(j * TQ, TQ)
    return [lax.dot_general(k_ref[pl.ds(start, TQ), hh * ATTN_V_DIM:(hh + 1) * ATTN_V_DIM], qs_ref[hh],
                            (((1,), (1,)), ((), ())), preferred_element_type=F32)
            for hh in range(HEADS_PER_STEP)]


def _attn_update(vt_ref, m_ref, acc_ref, sts, j, biases):
    heads = range(HEADS_PER_STEP)
    if biases is not None:
        sts = [sts[hh] + jnp.concatenate([biases[hh], biases[hh]], axis=1) for hh in heads]
    m_prev = [m_ref[hh] for hh in heads]
    m_new = [jnp.maximum(m_prev[hh], jnp.max(sts[hh], axis=0, keepdims=True)) for hh in heads]
    alpha = [jnp.exp2(m_prev[hh] - m_new[hh]) for hh in heads]
    p = [jnp.exp2((sts[hh] - m_new[hh]).astype(BF16)) for hh in heads]
    start = pl.multiple_of(j * TQ, TQ)
    pv = [jnp.dot(vt_ref[hh, :, pl.ds(start, TQ)], p[hh], preferred_element_type=F32) for hh in heads]
    for hh in heads:
        acc_ref[hh] = alpha[hh] * acc_ref[hh] + pv[hh]
        m_ref[hh] = m_new[hh]


def _diff_attn_kernel(q_ref, k_ref, v_ref, bias_ref, lam_ref, sg_ref, o_ref,
                      qs_ref, vt_ref, s_ref, m_ref, acc_ref, *, seq):
    qi = pl.program_id(2)
    heads = range(HEADS_PER_STEP)

    @pl.when(qi == 0)
    def _():
        step = 2 * TQ
        for hh in heads:
            for c in range(seq // step):
                v = v_ref[c * step:(c + 1) * step, hh * ATTN_V_DIM:(hh + 1) * ATTN_V_DIM]
                vt_ref[hh, 0:ATTN_V_DIM, c * step:(c + 1) * step] = v.astype(F32).T.astype(BF16)
            ones_row = lax.broadcasted_iota(jnp.int32, (V_ROWS - ATTN_V_DIM, seq), 0) == 0
            vt_ref[hh, ATTN_V_DIM:V_ROWS, :] = jnp.where(ones_row, 1.0, 0.0).astype(BF16)

    lane = lax.broadcasted_iota(jnp.int32, (TQ, ATTN_V_DIM), 1)
    for hh in heads:
        q = q_ref[:, hh * ATTN_V_DIM:(hh + 1) * ATTN_V_DIM]
        zero = jnp.zeros_like(q)
        qs_ref[hh, 0:TQ, :] = jnp.where(lane < HEAD_DIM, q, zero)
        qs_ref[hh, TQ:2 * TQ, :] = jnp.where(lane >= HEAD_DIM, q, zero)
    m_ref[...] = jnp.full(m_ref.shape, NEG_INF, F32)
    acc_ref[...] = jnp.zeros(acc_ref.shape, F32)
    scores = functools.partial(_attn_scores, qs_ref, k_ref)
    update = functools.partial(_attn_update, vt_ref, m_ref, acc_ref)

    def put(buf, sts):
        for hh in heads:
            s_ref[buf, hh] = sts[hh]

    get = lambda buf: [s_ref[buf, hh] for hh in heads]
    bias_prev = lambda: [bias_ref[hh, 0:TQ, :] for hh in heads]
    bias_diag = lambda: [bias_ref[hh, TQ:2 * TQ, :] for hh in heads]

    n_far = jnp.maximum(qi - 1, 0)
    put(0, scores(0))

    def far_pair(p, carry):
        j = 2 * p
        put(1, scores(j + 1))
        update(get(0), j, None)
        put(0, scores(j + 2))
        update(get(1), j + 1, None)
        return carry

    lax.fori_loop(0, n_far // 2, far_pair, 0)

    @pl.when(n_far % 2 == 1)
    def _():
        update(get(0), n_far - 1, None)
        put(0, scores(n_far))

    @pl.when(qi >= 1)
    def _():
        put(1, scores(qi))
        update(get(0), qi - 1, bias_prev())
        update(get(1), qi, bias_diag())

    @pl.when(qi == 0)
    def _():
        update(get(0), qi, bias_diag())

    for hh in heads:
        acc = acc_ref[hh]
        ot = acc[0:ATTN_V_DIM] / acc[ATTN_V_DIM:ATTN_V_DIM + 1]
        dt = ot[:, 0:TQ] - lam_ref[0:1, :] * ot[:, TQ:2 * TQ]
        yt = dt * lax.rsqrt(jnp.mean(dt * dt, axis=0, keepdims=True) + NORM_EPS)
        o_ref[:, hh * ATTN_V_DIM:(hh + 1) * ATTN_V_DIM] = (
            yt.T * sg_ref[...] * (1.0 - LAMBDA_INIT)).astype(o_ref.dtype)


def _diff_attn(qkv, bias_tiles, lam, subln_g, bsz, seq):
    nq = seq // TQ
    width = HEADS_PER_STEP * ATTN_V_DIM
    h_blocks = D_ATTN // width
    return pl.pallas_call(
        functools.partial(_diff_attn_kernel, seq=seq),
        grid=(bsz, ATTN_HEADS // HEADS_PER_STEP, nq),
        in_specs=[pl.BlockSpec((TQ, width), lambda b, h, i: (b * nq + i, h)),
                  pl.BlockSpec((seq, width), lambda b, h, i: (b, h_blocks + h)),
                  pl.BlockSpec((seq, width), lambda b, h, i: (b, 2 * h_blocks + h)),
                  pl.BlockSpec((HEADS_PER_STEP, 2 * TQ, TQ), lambda b, h, i: (h, 0, 0)),
                  pl.BlockSpec((8, TQ), lambda b, h, i: (0, 0)),
                  pl.BlockSpec((1, ATTN_V_DIM), lambda b, h, i: (0, 0))],
        out_specs=pl.BlockSpec((TQ, width), lambda b, h, i: (b * nq + i, h)),
        out_shape=jax.ShapeDtypeStruct((bsz * seq, D_ATTN), BF16),
        scratch_shapes=[pltpu.VMEM((HEADS_PER_STEP, 2 * TQ, ATTN_V_DIM), BF16),
                        pltpu.VMEM((HEADS_PER_STEP, V_ROWS, seq), BF16),
                        pltpu.VMEM((2, HEADS_PER_STEP, TQ, 2 * TQ), F32),
                        pltpu.VMEM((HEADS_PER_STEP, 1, 2 * TQ), F32),
                        pltpu.VMEM((HEADS_PER_STEP, V_ROWS, 2 * TQ), F32)],
        compiler_params=_params("arbitrary", "arbitrary", "arbitrary"),
        name="diff_attn",
    )(qkv, qkv, qkv, bias_tiles, lam, subln_g)


P_MU_R, P_MU_K, P_MU_V, P_W0, P_A0, P_KK, P_KA, P_RK, P_LNG, P_LNB = range(10)
P_ROWS = 16


S_AT, S_RT, S_BT, S_KT, S_BREM, S_KREM, S_V = range(7)
S_BONUS, S_GATE = range(2)
M_AM, M_ARB, M_ARK, M_RT, M_V, M_BREM, M_KREM = range(7)
F_U0, F_BONUS, F_GATE = range(3)


def _rwkv_prep(g, rp_ref, kp_ref, vp_ref, lo_ref, pv_ref, w2_ref, a2_ref, g2_ref, e_ref, tri_ref, prev_ref,
               ops_ref, epi_ref, dec_ref):
    cols = slice(g * GROUP, (g + 1) * GROUP)

    def shifted(x, prev):
        row0 = lax.broadcasted_iota(jnp.int32, x.shape, 0) == 0
        return jnp.where(row0, prev, pltpu.roll(x, 1, axis=0))

    def mixed(ref, col, mu_row):
        x = ref[:, cols]
        prev = prev_ref[0:1, col * D_RWKV + g * GROUP:col * D_RWKV + (g + 1) * GROUP]
        return x + (shifted(x, prev) - x) * pv_ref[mu_row:mu_row + 1, cols]

    pv = lambda i: pv_ref[i:i + 1, cols]
    r = mixed(rp_ref, 0, P_MU_R)
    k = mixed(kp_ref, 1, P_MU_K)
    v = mixed(vp_ref, 2, P_MU_V)

    lora = lo_ref[:, :LORA_COLS] + shifted(lo_ref[:, LORA_COLS:], prev_ref[0:1, 3 * D_RWKV + LORA_COLS:])
    w_in = pv(P_W0) + _mm(jnp.tanh(lora[:, :LORA_W]), w2_ref[:, cols])
    w = -(jnp.maximum(-w_in, 0.0) + jnp.log(1.0 + jnp.exp(-jnp.abs(w_in)))) - 0.5
    wlog = -jnp.exp(w)
    a = 1.0 / (1.0 + jnp.exp(-(pv(P_A0) + _mm(lora[:, LORA_W:2 * LORA_W], a2_ref[:, cols]))))
    gate = _mm(1.0 / (1.0 + jnp.exp(-lora[:, 2 * LORA_W:])), g2_ref[:, cols])

    head_sum = lambda x: _mm(x, e_ref[...])
    kk = k * pv(P_KK)
    kk = kk / jnp.maximum(jnp.sqrt(head_sum(kk * kk)), 1e-12)
    k2 = k * (1.0 + (a - 1.0) * pv(P_KA))
    b = kk * a
    cum = jnp.dot(tri_ref[...], wlog, preferred_element_type=F32, precision=lax.Precision.HIGHEST)

    tot = cum[CHUNK - 1:CHUNK, :]
    e_neg = jnp.exp(-cum)
    e_rem = jnp.exp(tot - cum)
    staged = {S_AT: jnp.exp(cum - wlog) * (-kk), S_RT: jnp.exp(cum) * r, S_BT: e_neg * b, S_KT: e_neg * k2,
              S_BREM: e_rem * b, S_KREM: e_rem * k2, S_V: v}
    for slot, val in staged.items():
        ops_ref[slot, :, cols] = val.astype(BF16)
    epi_ref[S_BONUS, :, cols] = head_sum(r * k2 * pv(P_RK)) * v
    epi_ref[S_GATE, :, cols] = gate
    dec_ref[:, cols] = jnp.broadcast_to(jnp.exp(tot), (dec_ref.shape[0], GROUP))


def _head_sum(x, e_ref):
    return jnp.concatenate([_mm(x[:, i * GROUP:(i + 1) * GROUP], e_ref[...]) for i in range(N_GROUPS)],
                           axis=1)


def _rwkv_tables():
    rows_bd = lax.broadcasted_iota(jnp.int32, (GROUP, GROUP), 0) // HEAD_DIM
    lanes_bd = lax.broadcasted_iota(jnp.int32, (GROUP, GROUP), 1) // HEAD_DIM
    bd_mask = rows_bd == lanes_bd
    t_idx = lax.broadcasted_iota(jnp.int32, (CHUNK, GROUP), 0)
    s_idx = lax.broadcasted_iota(jnp.int32, (CHUNK, GROUP), 1) % HEAD_DIM

    def bd(x):
        return jnp.where(bd_mask, jnp.concatenate([x] * (GROUP // HEAD_DIM), axis=0), jnp.zeros((), x.dtype))

    return bd, s_idx < t_idx, s_idx <= t_idx, jnp.where(s_idx == t_idx, 1.0, 0.0).astype(F32)


def _rwkv_solve(ops_ref, epi_ref, dec_ref, mid_ref, midf_ref, mdec_ref):
    groups = range(N_GROUPS)
    cols = lambda g: slice(g * GROUP, (g + 1) * GROUP)
    grp = lambda slot, g: ops_ref[slot, :, cols(g)]
    bd, m_strict, m_incl, eye = _rwkv_tables()

    at = [grp(S_AT, g) for g in groups]
    aa = [_mm_nt(jnp.concatenate([at[g], grp(S_RT, g)], axis=0),
                 jnp.concatenate([bd(grp(S_BT, g)), bd(grp(S_KT, g))], axis=0)) for g in groups]
    a_ab = [jnp.where(m_strict, aa[g][:CHUNK, :GROUP], 0.0) for g in groups]
    a_ak = [jnp.where(m_strict, aa[g][:CHUNK, GROUP:], 0.0) for g in groups]
    for g in groups:
        mid_ref[M_ARB, :, cols(g)] = jnp.where(m_incl, aa[g][CHUNK:, :GROUP], 0.0).astype(BF16)
        mid_ref[M_ARK, :, cols(g)] = jnp.where(m_incl, aa[g][CHUNK:, GROUP:], 0.0).astype(BF16)
    yield

    minv = [eye + a_ab[g] for g in groups]
    nk = [_mm(a_ab[g], bd(a_ab[g])) for g in groups]
    yield
    for _ in range(int(math.log2(CHUNK)) - 2):
        res = [_mm(jnp.concatenate([nk[g], minv[g]], axis=0), bd(nk[g])) for g in groups]
        nk = [res[g][:CHUNK] for g in groups]
        minv = [minv[g] + res[g][CHUNK:] for g in groups]
        yield
    minv = [minv[g] + _mm(minv[g], bd(nk[g])) for g in groups]
    yield

    x1 = [_mm(a_ak[g], bd(grp(S_V, g))) for g in groups]
    yield
    ma = [_mm(minv[g], jnp.concatenate([bd(at[g]), bd(x1[g].astype(BF16))], axis=1)) for g in groups]
    for g in groups:
        mid_ref[M_AM, :, cols(g)] = ma[g][:, :GROUP].astype(BF16)
        midf_ref[F_U0, :, cols(g)] = ma[g][:, GROUP:]
    for src, dst in ((S_RT, M_RT), (S_V, M_V), (S_BREM, M_BREM), (S_KREM, M_KREM)):
        mid_ref[dst] = ops_ref[src]
    midf_ref[F_BONUS] = epi_ref[S_BONUS]
    midf_ref[F_GATE] = epi_ref[S_GATE]
    mdec_ref[...] = dec_ref[...]
    yield


def _rwkv_advance(mid_ref, midf_ref, mdec_ref, pv_ref, e_ref, st_ref, o_ref):
    groups = range(N_GROUPS)
    heads = GROUP // HEAD_DIM
    cols = lambda g: slice(g * GROUP, (g + 1) * GROUP)
    grp = lambda slot, g: mid_ref[slot, :, cols(g)]
    bd, _, _, _ = _rwkv_tables()

    st = [st_ref[g] for g in groups]
    c1 = [_mm_nt(jnp.concatenate([grp(M_AM, g), grp(M_RT, g)], axis=0), bd(st[g])) for g in groups]
    yield
    u = [c1[g][:CHUNK] + midf_ref[F_U0, :, cols(g)] for g in groups]
    uv = [jnp.concatenate([bd(u[g].astype(BF16)), bd(grp(M_V, g))], axis=0) for g in groups]
    y = [c1[g][CHUNK:] + _mm(jnp.concatenate([grp(M_ARB, g), grp(M_ARK, g)], axis=1), uv[g]) for g in groups]
    upd = [_mm_tn(uv[g], jnp.concatenate([bd(grp(M_BREM, g)), bd(grp(M_KREM, g))], axis=0))
           for g in groups]
    yield
    for g in groups:
        folded = sum(upd[g][i * HEAD_DIM:(i + 1) * HEAD_DIM] for i in range(heads))
        st_ref[g] = st[g] * mdec_ref[0:1, cols(g)] + folded
    y = jnp.concatenate(y, axis=1)
    pv = lambda i: pv_ref[i:i + 1, :]
    mu = _head_sum(y, e_ref) * (1.0 / HEAD_DIM)
    yield
    d = y - mu
    var = _head_sum(d * d, e_ref) * (1.0 / HEAD_DIM)
    yield
    yn = d * lax.rsqrt(var + RWKV_GN_EPS) * pv(P_LNG) + pv(P_LNB)
    o_ref[...] = ((yn + midf_ref[F_BONUS]) * midf_ref[F_GATE]).astype(o_ref.dtype)
    yield


def _interleave(*stages):
    live = list(stages)
    while live:
        for s in list(live):
            try:
                next(s)
            except StopIteration:
                live.remove(s)


def _rwkv_kernel(rp_ref, kp_ref, vp_ref, lo_ref, pv_ref, w2_ref, a2_ref, g2_ref, e_ref, tri_ref,
                 o_ref, st_ref, prev_ref, ops_ref, epi_ref, dec_ref, mid_ref, midf_ref, mdec_ref):
    c = pl.program_id(1)

    @pl.when(c == 0)
    def _():
        for ref in (st_ref, prev_ref, epi_ref, dec_ref, midf_ref, mdec_ref):
            ref[...] = jnp.zeros(ref.shape, F32)
        ops_ref[...] = jnp.zeros(ops_ref.shape, BF16)
        mid_ref[...] = jnp.zeros(mid_ref.shape, BF16)

    def prep_stage(slot):
        for g in range(N_GROUPS):
            _rwkv_prep(g, rp_ref, kp_ref, vp_ref, lo_ref, pv_ref, w2_ref, a2_ref, g2_ref, e_ref, tri_ref,
                       prev_ref, ops_ref.at[slot], epi_ref.at[slot], dec_ref.at[slot])
            yield

    def step(new, old):
        _interleave(_rwkv_solve(ops_ref.at[old], epi_ref.at[old], dec_ref.at[old],
                                mid_ref.at[old], midf_ref.at[old], mdec_ref.at[old]),
                    _rwkv_advance(mid_ref.at[new], midf_ref.at[new], mdec_ref.at[new], pv_ref, e_ref,
                                  st_ref, o_ref),
                    prep_stage(new))
        for col, ref in enumerate((rp_ref, kp_ref, vp_ref, lo_ref)):
            prev_ref[0:1, col * D_RWKV:(col + 1) * D_RWKV] = ref[CHUNK - 1:CHUNK, :]

    @pl.when(c % 2 == 0)
    def _():
        step(0, 1)

    @pl.when(c % 2 == 1)
    def _():
        step(1, 0)


def _rwkv(pr, pvec, w2p, a2p, g2p, e256, tri, bsz, seq):
    nc = seq // CHUNK
    col = lambda cb: pl.BlockSpec((CHUNK, D_RWKV), lambda b, c: (b * nc + jnp.minimum(c, nc - 1), cb))
    full = lambda shape: pl.BlockSpec(shape, lambda b, c: (0, 0))
    return pl.pallas_call(
        _rwkv_kernel,
        grid=(bsz, nc + 2),
        in_specs=[col(0), col(1), col(2), col(3),
                  full((P_ROWS, D_RWKV)), full((LORA_W, D_RWKV)), full((LORA_W, D_RWKV)),
                  full((LORA_G, D_RWKV)), full((GROUP, GROUP)), full((CHUNK, CHUNK))],
        out_specs=pl.BlockSpec((CHUNK, D_RWKV), lambda b, c: (b * nc + jnp.maximum(c - 2, 0), 0)),
        out_shape=jax.ShapeDtypeStruct((bsz * seq, D_RWKV), BF16),
        scratch_shapes=[pltpu.VMEM((N_GROUPS, HEAD_DIM, GROUP), F32),
                        pltpu.VMEM((8, 4 * D_RWKV), F32),
                        pltpu.VMEM((2, 7, CHUNK, D_RWKV), BF16),
                        pltpu.VMEM((2, 2, CHUNK, D_RWKV), F32),
                        pltpu.VMEM((2, 8, D_RWKV), F32),
                        pltpu.VMEM((2, 7, CHUNK, D_RWKV), BF16),
                        pltpu.VMEM((2, 3, CHUNK, D_RWKV), F32),
                        pltpu.VMEM((2, 8, D_RWKV), F32)],
        compiler_params=_params("arbitrary", "arbitrary"),
        name="rwkv7",
    )(pr, pr, pr, pr, pvec, w2p, a2p, g2p, e256, tri)


def _out_proj_kernel(oa_ref, or_ref, w_ref, x_ref, gate_ref, g_ref, sc_ref, sh_ref, x1_ref, h2_ref):
    acc = jnp.dot(oa_ref[...], w_ref[0:D_ATTN, :], preferred_element_type=F32)
    acc = acc + jnp.dot(or_ref[...], w_ref[D_ATTN:, :], preferred_element_type=F32)
    x1 = x_ref[...] + gate_ref[...] * acc
    x1_ref[...] = x1
    h2_ref[...] = _modulated_norm(x1, g_ref[...], sc_ref[...], sh_ref[...]).astype(BF16)


def _out_proj(o_attn, o_rwkv, w_bf, x2, mod3, norm_g, seq):
    n = x2.shape[0]
    tiles_per_batch = seq // TM_OUT
    mod = lambda col: pl.BlockSpec((None, 1, D_MODEL), lambda i: (i // tiles_per_batch, 0, col))
    return pl.pallas_call(
        _out_proj_kernel,
        grid=(n // TM_OUT,),
        in_specs=[pl.BlockSpec((TM_OUT, D_ATTN), lambda i: (i, 0)),
                  pl.BlockSpec((TM_OUT, D_RWKV), lambda i: (i, 0)),
                  pl.BlockSpec((D_MODEL, D_MODEL), lambda i: (0, 0), pipeline_mode=pl.Buffered(1)),
                  pl.BlockSpec((TM_OUT, D_MODEL), lambda i: (i, 0)),
                  mod(2),
                  pl.BlockSpec((1, D_MODEL), lambda i: (0, 0)),
                  mod(4), mod(3)],
        out_specs=[pl.BlockSpec((TM_OUT, D_MODEL), lambda i: (i, 0)),
                   pl.BlockSpec((TM_OUT, D_MODEL), lambda i: (i, 0))],
        out_shape=[jax.ShapeDtypeStruct((n, D_MODEL), F32),
                   jax.ShapeDtypeStruct((n, D_MODEL), BF16)],
        compiler_params=_params("arbitrary"),
        name="out_proj",
    )(o_attn, o_rwkv, w_bf, x2, mod3, norm_g, mod3, mod3)


def _ffn_up_kernel(h_ref, wg_ref, wv_ref, cwg_ref, cwv_ref, cbg_ref, cbv_ref, o_ref,
                   wgb_ref, wvb_ref, carry_ref, *, tiles_per_batch):
    i = pl.program_id(1)

    @pl.when(i == 0)
    def _():
        wgb_ref[...] = wg_ref[...].astype(BF16)
        wvb_ref[...] = wv_ref[...].astype(BF16)
        carry_ref[...] = jnp.zeros(carry_ref.shape, F32)

    first = (i % tiles_per_batch) == 0

    def conv(up, prev, cw_ref, cb_ref):
        def taps(x, x1, x2):
            return cb_ref[...] + cw_ref[0:1, :] * x2 + cw_ref[1:2, :] * x1 + cw_ref[2:3, :] * x

        y = taps(up, pltpu.roll(up, 1, axis=0), pltpu.roll(up, 2, axis=0))
        ext = jnp.concatenate([prev, up[0:8, :]], axis=0)
        head = taps(ext, pltpu.roll(ext, 1, axis=0), pltpu.roll(ext, 2, axis=0))[8:16, :]
        return jnp.concatenate([head, y[8:, :]], axis=0)

    h = h_ref[...]
    up_g = jnp.dot(h, wgb_ref[...], preferred_element_type=F32)
    up_v = jnp.dot(h, wvb_ref[...], preferred_element_type=F32)
    gate = conv(up_g, jnp.where(first, 0.0, carry_ref[0]), cwg_ref, cbg_ref)
    val = conv(up_v, jnp.where(first, 0.0, carry_ref[1]), cwv_ref, cbv_ref)
    carry_ref[0] = up_g[TM - 8:, :]
    carry_ref[1] = up_v[TM - 8:, :]
    o_ref[...] = (gate * (1.0 / (1.0 + jnp.exp(-gate))) * val).astype(o_ref.dtype)


def _ffn_up(h2, w_up, conv_w8, conv_b, seq):
    n = h2.shape[0]
    tiles_per_batch = seq // TM
    nf = D_FF // TN
    return pl.pallas_call(
        functools.partial(_ffn_up_kernel, tiles_per_batch=tiles_per_batch),
        grid=(nf, n // TM),
        in_specs=[pl.BlockSpec((TM, D_MODEL), lambda j, i: (i, 0)),
                  pl.BlockSpec((D_MODEL, TN), lambda j, i: (0, j)),
                  pl.BlockSpec((D_MODEL, TN), lambda j, i: (0, nf + j)),
                  pl.BlockSpec((8, TN), lambda j, i: (0, j)),
                  pl.BlockSpec((8, TN), lambda j, i: (0, nf + j)),
                  pl.BlockSpec((1, TN), lambda j, i: (0, j)),
                  pl.BlockSpec((1, TN), lambda j, i: (0, nf + j))],
        out_specs=pl.BlockSpec((TM, TN), lambda j, i: (i, j)),
        out_shape=jax.ShapeDtypeStruct((n, D_FF), BF16),
        scratch_shapes=[pltpu.VMEM((D_MODEL, TN), BF16),
                        pltpu.VMEM((D_MODEL, TN), BF16),
                        pltpu.VMEM((2, 8, TN), F32)],
        compiler_params=_params("arbitrary", "arbitrary"),
        name="ffn_up",
    )(h2, w_up, w_up, conv_w8, conv_w8, conv_b, conv_b)


def _ffn_down_kernel(a_ref, w_ref, x_ref, gate_ref, o_ref, wb_ref):
    @pl.when(pl.program_id(1) == 0)
    def _():
        wb_ref[...] = w_ref[...].astype(BF16)

    acc = jnp.dot(a_ref[...], wb_ref[...], preferred_element_type=F32)
    o_ref[...] = x_ref[...] + gate_ref[...] * acc


def _ffn_down(act, w_down, x1, mod3, seq):
    n = x1.shape[0]
    tiles_per_batch = seq // TM_DOWN
    gate_col = 5 * (D_MODEL // TN)
    return pl.pallas_call(
        _ffn_down_kernel,
        grid=(D_MODEL // TN, n // TM_DOWN),
        in_specs=[pl.BlockSpec((TM_DOWN, D_FF), lambda j, i: (i, 0)),
                  pl.BlockSpec((D_FF, TN), lambda j, i: (0, j)),
                  pl.BlockSpec((TM_DOWN, TN), lambda j, i: (i, j)),
                  pl.BlockSpec((None, 1, TN), lambda j, i: (i // tiles_per_batch, 0, gate_col + j))],
        out_specs=pl.BlockSpec((TM_DOWN, TN), lambda j, i: (i, j)),
        out_shape=jax.ShapeDtypeStruct((n, D_MODEL), F32),
        scratch_shapes=[pltpu.VMEM((D_FF, TN), BF16)],
        compiler_params=_params("arbitrary", "arbitrary"),
        name="ffn_down",
    )(act, w_down, x1, mod3)


def _pad_cols(w, width):
    return jnp.pad(w, ((0, 0), (0, width - w.shape[1])))


def _pad_rows(w, height):
    return jnp.pad(w, ((0, height - w.shape[0]), (0, 0)))


def kernel(x, c, rel_bias, w_ada, b_ada, norm_mix_g, w_in, q_norm_g, k_norm_g, lambda_q1, lambda_k1,
           lambda_q2, lambda_k2, attn_subln_g, mu_rkv, mu_wag, w0, w1, w2, a0, a1, a2, g1, g2, k_k,
           k_a, r_k, ln_x_g, ln_x_b, w_out, norm_ffn_g, w_up, conv_w, conv_b, w_down):
    bsz, seq, _ = x.shape
    n = bsz * seq
    x2 = x.reshape(n, D_MODEL)

    mod = _adaln(c, w_ada[0], b_ada[0])
    mod3 = mod.reshape(bsz, 1, 6 * D_MODEL)

    idx = jnp.arange(GROUP) // HEAD_DIM
    e256 = (idx[:, None] == idx[None, :]).astype(BF16)
    tri = (jnp.arange(CHUNK)[:, None] >= jnp.arange(CHUNK)[None, :]).astype(F32)

    gain = jnp.concatenate([jnp.tile(q_norm_g[0], D_ATTN // HEAD_DIM) * (HEAD_DIM ** -0.5 * LOG2E),
                            jnp.tile(k_norm_g[0], D_ATTN // HEAD_DIM)]).reshape(1, 2 * D_ATTN)
    mu = mu_wag[0]
    lora_w = (_pad_cols(w1[0], LORA_W), _pad_cols(a1[0], LORA_W), _pad_cols(g1[0], LORA_G))
    w_lora = jnp.concatenate([(1.0 - mu[i])[:, None] * w for i, w in enumerate(lora_w)]
                             + [mu[i][:, None] * w for i, w in enumerate(lora_w)], axis=1)
    qkv, pr = _proj(x2, norm_mix_g, mod3, w_in[0], w_lora, gain, e256, seq)

    bias_tiles, lam = _attn_tables(rel_bias, lambda_q1, lambda_k1, lambda_q2, lambda_k2)
    o_attn = _diff_attn(qkv, bias_tiles, lam, attn_subln_g, bsz, seq)

    pvec = jnp.concatenate([mu_rkv[0], w0, a0, k_k, k_a, r_k.reshape(1, D_RWKV), ln_x_g, ln_x_b,
                            jnp.zeros((P_ROWS - 10, D_RWKV), F32)], axis=0)
    o_rwkv = _rwkv(pr, pvec, _pad_rows(w2[0], LORA_W).astype(BF16), _pad_rows(a2[0], LORA_W).astype(BF16),
                   _pad_rows(g2[0], LORA_G).astype(BF16), e256, tri, bsz, seq)

    x1, h2 = _out_proj(o_attn, o_rwkv, w_out[0].astype(BF16), x2, mod3, norm_ffn_g, seq)

    act = _ffn_up(h2, w_up[0], _pad_rows(conv_w[0], 8), conv_b, seq)
    out = _ffn_down(act, w_down[0], x1, mod3, seq)
    return out.reshape(bsz, seq, D_MODEL)
```

```python
import functools
import math

import jax
import jax.numpy as jnp
from jax import lax
from jax.experimental import pallas as pl
from jax.experimental.pallas import tpu as pltpu

F32 = jnp.float32
BF16 = jnp.bfloat16

D_MODEL = 2048
D_ATTN = 1024
D_RWKV = 1024
HEAD_DIM = 64
ATTN_HEADS = 8
ATTN_V_DIM = 128
RWKV_HEADS = 16
D_FF = 5632
NUM_BUCKETS = 32
MAX_EXACT = 16
MAX_DISTANCE = 128
NORM_EPS = 1e-6
RWKV_GN_EPS = 64e-5
NEG_INF = -1e30
LOG2E = math.log2(math.e)
LAMBDA_INIT = 0.8 - 0.6 * math.exp(0.0)

LANES = 128
LORA_W = 128
LORA_G = 256
LORA_COLS = 2 * LORA_W + LORA_G

TM = 1024
TN = 512
TM_DOWN = 512
TM_OUT = 512
TQ = 256
HEADS_PER_STEP = 8
V_ROWS = ATTN_V_DIM + 16
CHUNK = 64
GROUP = 256
N_GROUPS = D_RWKV // GROUP
SEQS_PER_STEP = 2

VMEM_LIMIT = 56 * 1024 * 1024


def _mm(a, b):
    return jnp.dot(a.astype(BF16), b.astype(BF16), preferred_element_type=F32)


def _mm_nt(a, b):
    return lax.dot_general(a.astype(BF16), b.astype(BF16), (((1,), (1,)), ((), ())),
                           preferred_element_type=F32)


def _mm_tn(a, b):
    return lax.dot_general(a.astype(BF16), b.astype(BF16), (((0,), (0,)), ((), ())),
                           preferred_element_type=F32)


def _params(*sem):
    return pltpu.CompilerParams(dimension_semantics=sem, vmem_limit_bytes=VMEM_LIMIT)


def _adaln_kernel(c_ref, w_ref, b_ref, o_ref):
    c = c_ref[...]
    c_act = c * (1.0 / (1.0 + jnp.exp(-c)))
    w = w_ref[...]
    w_hi = w.astype(BF16)
    w_lo = (w - w_hi.astype(F32)).astype(BF16)
    c_hi = c_act.astype(BF16)
    c_lo = (c_act - c_hi.astype(F32)).astype(BF16)
    rows = c.shape[0]
    t = jnp.dot(jnp.concatenate([c_hi, c_lo], axis=0), w_hi, preferred_element_type=F32)
    acc = t[:rows] + t[rows:] + jnp.dot(c_hi, w_lo, preferred_element_type=F32)
    o_ref[...] = acc + b_ref[...]


def _adaln(c, w_ada, b_ada):
    bsz = c.shape[0]
    n_out = w_ada.shape[1]
    tn = 1024
    return pl.pallas_call(
        _adaln_kernel,
        grid=(n_out // tn,),
        in_specs=[pl.BlockSpec((bsz, D_MODEL), lambda j: (0, 0)),
                  pl.BlockSpec((D_MODEL, tn), lambda j: (0, j)),
                  pl.BlockSpec((1, tn), lambda j: (0, j))],
        out_specs=pl.BlockSpec((bsz, tn), lambda j: (0, j)),
        out_shape=jax.ShapeDtypeStruct((bsz, n_out), F32),
        compiler_params=_params("arbitrary"),
        name="adaln",
    )(c, w_ada, b_ada.reshape(1, n_out))


def _modulated_norm(x, g, scale, shift):
    y = x * lax.rsqrt(jnp.mean(x * x, axis=-1, keepdims=True) + NORM_EPS)
    return (y * g) * (1.0 + scale) + shift


QK_TILES = 2 * D_ATTN // TN
ATTN_TILES = 3 * D_ATTN // TN
W_IN_TILES = (3 * D_ATTN + 3 * D_RWKV) // TN
LORA_TILES = 2 * LORA_COLS // TN


def _proj_kernel(x_ref, g_ref, sc_ref, sh_ref, w_ref, wl_ref, gain_ref, e_ref, qkv_ref, pr_ref, h_ref):
    j = pl.program_id(1)

    @pl.when(j == 0)
    def _():
        h_ref[...] = _modulated_norm(x_ref[...], g_ref[...], sc_ref[...], sh_ref[...]).astype(BF16)

    def project(weights_ref):
        return jnp.dot(h_ref[...], weights_ref[...].astype(BF16), preferred_element_type=F32)

    @pl.when(j < QK_TILES)
    def _():
        acc = project(w_ref)
        half = TN // 2
        for s in range(2):
            a = acc[:, s * half:(s + 1) * half]
            ss = _mm(a * a, e_ref[...])
            y = a * lax.rsqrt(ss * (1.0 / HEAD_DIM) + NORM_EPS) * gain_ref[:, s * half:(s + 1) * half]
            qkv_ref[:, s * half:(s + 1) * half] = y.astype(qkv_ref.dtype)

    @pl.when(jnp.logical_and(j >= QK_TILES, j < ATTN_TILES))
    def _():
        qkv_ref[...] = project(w_ref).astype(qkv_ref.dtype)

    @pl.when(jnp.logical_and(j >= ATTN_TILES, j < W_IN_TILES))
    def _():
        pr_ref[...] = project(w_ref)

    @pl.when(j >= W_IN_TILES)
    def _():
        pr_ref[...] = project(wl_ref)


def _proj(x2, norm_g, mod3, w_in, w_lora, gain, e256, seq):
    n = x2.shape[0]
    tiles_per_batch = seq // TM
    batch = lambda i, j: i // tiles_per_batch
    return pl.pallas_call(
        _proj_kernel,
        grid=(n // TM, W_IN_TILES + LORA_TILES),
        in_specs=[pl.BlockSpec((TM, D_MODEL), lambda i, j: (i, 0)),
                  pl.BlockSpec((1, D_MODEL), lambda i, j: (0, 0)),
                  pl.BlockSpec((None, 1, D_MODEL), lambda i, j: (batch(i, j), 0, 1)),
                  pl.BlockSpec((None, 1, D_MODEL), lambda i, j: (batch(i, j), 0, 0)),
                  pl.BlockSpec((D_MODEL, TN), lambda i, j: (0, jnp.minimum(j, W_IN_TILES - 1))),
                  pl.BlockSpec((D_MODEL, TN), lambda i, j: (0, jnp.maximum(j - W_IN_TILES, 0))),
                  pl.BlockSpec((1, TN), lambda i, j: (0, jnp.minimum(j, QK_TILES - 1))),
                  pl.BlockSpec((TN // 2, TN // 2), lambda i, j: (0, 0))],
        out_specs=[pl.BlockSpec((TM, TN), lambda i, j: (i, jnp.minimum(j, ATTN_TILES - 1))),
                   pl.BlockSpec((TM, TN), lambda i, j: (i, jnp.maximum(j - ATTN_TILES, 0)))],
        out_shape=[jax.ShapeDtypeStruct((n, 3 * D_ATTN), BF16),
                   jax.ShapeDtypeStruct((n, 3 * D_RWKV + 2 * LORA_COLS), F32)],
        scratch_shapes=[pltpu.VMEM((TM, D_MODEL), BF16)],
        compiler_params=_params("arbitrary", "arbitrary"),
        name="proj",
    )(x2, norm_g, mod3, mod3, w_in, w_lora, gain, e256)


def _attn_tables_kernel(rb_ref, lq1_ref, lk1_ref, lq2_ref, lk2_ref, bias_ref, lam_ref):
    h = pl.program_id(0)
    rows = lax.broadcasted_iota(jnp.int32, (2 * TQ, TQ), 0)
    cols = lax.broadcasted_iota(jnp.int32, (2 * TQ, TQ), 1)
    far = rb_ref[NUM_BUCKETS - 1, h]
    dist = cols - rows + TQ
    n = jnp.maximum(dist, 0)
    nf = jnp.maximum(n, 1).astype(F32)
    large = MAX_EXACT + (jnp.log(nf / MAX_EXACT) / math.log(MAX_DISTANCE / MAX_EXACT)
                         * (NUM_BUCKETS - MAX_EXACT)).astype(jnp.int32)
    large = jnp.minimum(large, NUM_BUCKETS - 1)
    bucket = jnp.where(n < MAX_EXACT, n, large)
    bias = jnp.zeros((2 * TQ, TQ), F32)
    for b in range(NUM_BUCKETS):
        bias = jnp.where(bucket == b, rb_ref[b, h], bias)
    bias_ref[...] = jnp.where(dist >= 0, (bias - far) * LOG2E, NEG_INF)
    s1 = jnp.sum(lq1_ref[...] * lk1_ref[...], axis=1, keepdims=True)
    s2 = jnp.sum(lq2_ref[...] * lk2_ref[...], axis=1, keepdims=True)
    lam = jnp.exp(s1) - jnp.exp(s2) + LAMBDA_INIT
    lam_ref[...] = jnp.broadcast_to(lam, lam_ref.shape)


def _attn_tables(rel_bias, lq1, lk1, lq2, lk2):
    vec = pl.BlockSpec((1, HEAD_DIM), lambda h: (0, 0))
    return pl.pallas_call(
        _attn_tables_kernel,
        grid=(ATTN_HEADS,),
        in_specs=[pl.BlockSpec(memory_space=pltpu.SMEM), vec, vec, vec, vec],
        out_specs=[pl.BlockSpec((None, 2 * TQ, TQ), lambda h: (h, 0, 0)),
                   pl.BlockSpec((8, TQ), lambda h: (0, 0))],
        out_shape=[jax.ShapeDtypeStruct((ATTN_HEADS, 2 * TQ, TQ), F32),
                   jax.ShapeDtypeStruct((8, TQ), F32)],
        compiler_params=_params("arbitrary"),
        name="attn_tables",
    )(rel_bias, lq1, lk1, lq2, lk2)


def _attn_scores(qs_ref, k_ref, j):
    start = pl.multiple_of(j * TQ, TQ)
    return [lax.dot_general(k_ref[pl.ds(start, TQ), hh * ATTN_V_DIM:(hh + 1) * ATTN_V_DIM], qs_ref[hh],
                            (((1,), (1,)), ((), ())), preferred_element_type=F32)
            for hh in range(HEADS_PER_STEP)]


def _attn_update(vt_ref, m_ref, acc_ref, sts, j, biases):
    heads = range(HEADS_PER_STEP)
    if biases is not None:
        sts = [sts[hh] + jnp.concatenate([biases[hh], biases[hh]], axis=1) for hh in heads]
    m_prev = [m_ref[hh] for hh in heads]
    m_new = [jnp.maximum(m_prev[hh], jnp.max(sts[hh], axis=0, keepdims=True)) for hh in heads]
    alpha = [jnp.exp2(m_prev[hh] - m_new[hh]) for hh in heads]
    p = [jnp.exp2(sts[hh] - m_new[hh]) for hh in heads]
    start = pl.multiple_of(j * TQ, TQ)
    pv = [jnp.dot(vt_ref[hh, :, pl.ds(start, TQ)], p[hh].astype(BF16), preferred_element_type=F32)
          for hh in heads]
    for hh in heads:
        acc_ref[hh] = alpha[hh] * acc_ref[hh] + pv[hh]
        m_ref[hh] = m_new[hh]


def _diff_attn_kernel(q_ref, k_ref, v_ref, bias_ref, lam_ref, sg_ref, o_ref,
                      qs_ref, vt_ref, s_ref, m_ref, acc_ref, *, seq):
    qi = pl.program_id(2)
    heads = range(HEADS_PER_STEP)

    @pl.when(qi == 0)
    def _():
        step = 2 * TQ
        for hh in heads:
            for c in range(seq // step):
                v = v_ref[c * step:(c + 1) * step, hh * ATTN_V_DIM:(hh + 1) * ATTN_V_DIM]
                vt_ref[hh, 0:ATTN_V_DIM, c * step:(c + 1) * step] = v.astype(F32).T.astype(BF16)
            ones_row = lax.broadcasted_iota(jnp.int32, (V_ROWS - ATTN_V_DIM, seq), 0) == 0
            vt_ref[hh, ATTN_V_DIM:V_ROWS, :] = jnp.where(ones_row, 1.0, 0.0).astype(BF16)

    lane = lax.broadcasted_iota(jnp.int32, (TQ, ATTN_V_DIM), 1)
    for hh in heads:
        q = q_ref[:, hh * ATTN_V_DIM:(hh + 1) * ATTN_V_DIM]
        zero = jnp.zeros_like(q)
        qs_ref[hh, 0:TQ, :] = jnp.where(lane < HEAD_DIM, q, zero)
        qs_ref[hh, TQ:2 * TQ, :] = jnp.where(lane >= HEAD_DIM, q, zero)
    m_ref[...] = jnp.full(m_ref.shape, NEG_INF, F32)
    acc_ref[...] = jnp.zeros(acc_ref.shape, F32)
    scores = functools.partial(_attn_scores, qs_ref, k_ref)
    update = functools.partial(_attn_update, vt_ref, m_ref, acc_ref)

    def put(buf, sts):
        for hh in heads:
            s_ref[buf, hh] = sts[hh]

    get = lambda buf: [s_ref[buf, hh] for hh in heads]
    bias_prev = lambda: [bias_ref[hh, 0:TQ, :] for hh in heads]
    bias_diag = lambda: [bias_ref[hh, TQ:2 * TQ, :] for hh in heads]

    n_far = jnp.maximum(qi - 1, 0)
    put(0, scores(0))

    def far_pair(p, carry):
        j = 2 * p
        put(1, scores(j + 1))
        update(get(0), j, None)
        put(0, scores(j + 2))
        update(get(1), j + 1, None)
        return carry

    lax.fori_loop(0, n_far // 2, far_pair, 0)

    @pl.when(n_far % 2 == 1)
    def _():
        put(1, scores(qi - 1))
        update(get(0), qi - 2, None)
        put(0, scores(qi))
        update(get(1), qi - 1, bias_prev())
        update(get(0), qi, bias_diag())

    @pl.when(jnp.logical_and(qi >= 1, n_far % 2 == 0))
    def _():
        put(1, scores(qi))
        update(get(0), qi - 1, bias_prev())
        update(get(1), qi, bias_diag())

    @pl.when(qi == 0)
    def _():
        update(get(0), qi, bias_diag())

    for hh in heads:
        acc = acc_ref[hh]
        ot = acc[0:ATTN_V_DIM] / acc[ATTN_V_DIM:ATTN_V_DIM + 1]
        dt = ot[:, 0:TQ] - lam_ref[0:1, :] * ot[:, TQ:2 * TQ]
        yt = dt * lax.rsqrt(jnp.mean(dt * dt, axis=0, keepdims=True) + NORM_EPS)
        o_ref[:, hh * ATTN_V_DIM:(hh + 1) * ATTN_V_DIM] = (
            yt.T * sg_ref[...] * (1.0 - LAMBDA_INIT)).astype(o_ref.dtype)


def _diff_attn(qkv, bias_tiles, lam, subln_g, bsz, seq):
    nq = seq // TQ
    width = HEADS_PER_STEP * ATTN_V_DIM
    h_blocks = D_ATTN // width
    return pl.pallas_call(
        functools.partial(_diff_attn_kernel, seq=seq),
        grid=(bsz, ATTN_HEADS // HEADS_PER_STEP, nq),
        in_specs=[pl.BlockSpec((TQ, width), lambda b, h, i: (b * nq + i, h)),
                  pl.BlockSpec((seq, width), lambda b, h, i: (b, h_blocks + h), pipeline_mode=pl.Buffered(1)),
                  pl.BlockSpec((seq, width), lambda b, h, i: (b, 2 * h_blocks + h), pipeline_mode=pl.Buffered(1)),
                  pl.BlockSpec((HEADS_PER_STEP, 2 * TQ, TQ), lambda b, h, i: (h, 0, 0)),
                  pl.BlockSpec((8, TQ), lambda b, h, i: (0, 0)),
                  pl.BlockSpec((1, ATTN_V_DIM), lambda b, h, i: (0, 0))],
        out_specs=pl.BlockSpec((TQ, width), lambda b, h, i: (b * nq + i, h)),
        out_shape=jax.ShapeDtypeStruct((bsz * seq, D_ATTN), BF16),
        scratch_shapes=[pltpu.VMEM((HEADS_PER_STEP, 2 * TQ, ATTN_V_DIM), BF16),
                        pltpu.VMEM((HEADS_PER_STEP, V_ROWS, seq), BF16),
                        pltpu.VMEM((2, HEADS_PER_STEP, TQ, 2 * TQ), F32),
                        pltpu.VMEM((HEADS_PER_STEP, 1, 2 * TQ), F32),
                        pltpu.VMEM((HEADS_PER_STEP, V_ROWS, 2 * TQ), F32)],
        compiler_params=_params("arbitrary", "arbitrary", "arbitrary"),
        name="diff_attn",
    )(qkv, qkv, qkv, bias_tiles, lam, subln_g)


P_MU_R, P_MU_K, P_MU_V, P_W0, P_A0, P_KK, P_KA, P_RK, P_LNG, P_LNB = range(10)
P_ROWS = 16


S_AT, S_RT, S_BT, S_KT, S_BREM, S_KREM, S_V = range(7)
S_BONUS, S_GATE = range(2)
M_AM, M_ARB, M_ARK, M_RT, M_V, M_BREM, M_KREM = range(7)
F_U0, F_BONUS, F_GATE = range(3)


def _rwkv_prep(g, rp_ref, kp_ref, vp_ref, lo_ref, pv_ref, w2_ref, a2_ref, g2_ref, e_ref, tri_ref, prev_ref,
               ops_ref, epi_ref, dec_ref):
    cols = slice(g * GROUP, (g + 1) * GROUP)

    def shifted(x, prev):
        row0 = lax.broadcasted_iota(jnp.int32, x.shape, 0) == 0
        return jnp.where(row0, prev, pltpu.roll(x, 1, axis=0))

    def mixed(ref, col, mu_row):
        x = ref[:, cols]
        prev = prev_ref[0:1, col * D_RWKV + g * GROUP:col * D_RWKV + (g + 1) * GROUP]
        return x + (shifted(x, prev) - x) * pv_ref[mu_row:mu_row + 1, cols]

    pv = lambda i: pv_ref[i:i + 1, cols]
    r = mixed(rp_ref, 0, P_MU_R)
    k = mixed(kp_ref, 1, P_MU_K)
    v = mixed(vp_ref, 2, P_MU_V)

    lora = lo_ref[:, :LORA_COLS] + shifted(lo_ref[:, LORA_COLS:], prev_ref[0:1, 3 * D_RWKV + LORA_COLS:])
    w_in = pv(P_W0) + _mm(jnp.tanh(lora[:, :LORA_W]), w2_ref[:, cols])
    w = -(jnp.maximum(-w_in, 0.0) + jnp.log(1.0 + jnp.exp(-jnp.abs(w_in)))) - 0.5
    wlog = -LOG2E * jnp.exp(w)
    a = 1.0 / (1.0 + jnp.exp(-(pv(P_A0) + _mm(lora[:, LORA_W:2 * LORA_W], a2_ref[:, cols]))))
    gate = _mm(1.0 / (1.0 + jnp.exp(-lora[:, 2 * LORA_W:])), g2_ref[:, cols])

    head_sum = lambda x: _mm(x, e_ref[...])
    kk = k * pv(P_KK)
    kk = kk / jnp.maximum(jnp.sqrt(head_sum(kk * kk)), 1e-12)
    k2 = k * (1.0 + (a - 1.0) * pv(P_KA))
    b = kk * a
    cum = jnp.zeros_like(wlog)
    rest = wlog
    for _ in range(3):
        term = rest.astype(BF16)
        cum = cum + jnp.dot(tri_ref[...], term, preferred_element_type=F32)
        rest = rest - term.astype(F32)

    tot = cum[CHUNK - 1:CHUNK, :]
    e_neg = jnp.exp2(-cum)
    e_rem = jnp.exp2(tot - cum)
    staged = {S_AT: jnp.exp2(cum - wlog) * (-kk), S_RT: jnp.exp2(cum) * r, S_BT: e_neg * b, S_KT: e_neg * k2,
              S_BREM: e_rem * b, S_KREM: e_rem * k2, S_V: v}
    for slot, val in staged.items():
        ops_ref[slot, :, cols] = val.astype(BF16)
    epi_ref[S_BONUS, :, cols] = head_sum(r * k2 * pv(P_RK)) * v
    epi_ref[S_GATE, :, cols] = gate
    dec_ref[:, cols] = jnp.broadcast_to(jnp.exp2(tot), (dec_ref.shape[0], GROUP))


def _head_sum(x, e_ref):
    return jnp.concatenate([_mm(x[:, i * GROUP:(i + 1) * GROUP], e_ref[...]) for i in range(N_GROUPS)],
                           axis=1)


def _rwkv_tables():
    rows_bd = lax.broadcasted_iota(jnp.int32, (GROUP, GROUP), 0) // HEAD_DIM
    lanes_bd = lax.broadcasted_iota(jnp.int32, (GROUP, GROUP), 1) // HEAD_DIM
    bd_mask = rows_bd == lanes_bd
    t_idx = lax.broadcasted_iota(jnp.int32, (CHUNK, GROUP), 0)
    s_idx = lax.broadcasted_iota(jnp.int32, (CHUNK, GROUP), 1) % HEAD_DIM

    def bd(x):
        return jnp.where(bd_mask, jnp.concatenate([x] * (GROUP // HEAD_DIM), axis=0), jnp.zeros((), x.dtype))

    return bd, s_idx < t_idx, s_idx <= t_idx, jnp.where(s_idx == t_idx, 1.0, 0.0).astype(F32)


def _rwkv_solve(ops_ref, epi_ref, dec_ref, mid_ref, midf_ref, mdec_ref):
    groups = range(N_GROUPS)
    cols = lambda g: slice(g * GROUP, (g + 1) * GROUP)
    grp = lambda slot, g: ops_ref[slot, :, cols(g)]
    bd, m_strict, m_incl, eye = _rwkv_tables()

    at = [grp(S_AT, g) for g in groups]
    aa = [_mm_nt(jnp.concatenate([at[g], grp(S_RT, g)], axis=0),
                 jnp.concatenate([bd(grp(S_BT, g)), bd(grp(S_KT, g))], axis=0)) for g in groups]
    a_ab = [jnp.where(m_strict, aa[g][:CHUNK, :GROUP], 0.0) for g in groups]
    a_ak = [jnp.where(m_strict, aa[g][:CHUNK, GROUP:], 0.0) for g in groups]
    for g in groups:
        mid_ref[M_ARB, :, cols(g)] = jnp.where(m_incl, aa[g][CHUNK:, :GROUP], 0.0).astype(BF16)
        mid_ref[M_ARK, :, cols(g)] = jnp.where(m_incl, aa[g][CHUNK:, GROUP:], 0.0).astype(BF16)
    yield

    minv = [eye + a_ab[g] for g in groups]
    nk = [_mm(a_ab[g], bd(a_ab[g])) for g in groups]
    yield
    for _ in range(int(math.log2(CHUNK)) - 2):
        res = [_mm(jnp.concatenate([nk[g], minv[g]], axis=0), bd(nk[g])) for g in groups]
        nk = [res[g][:CHUNK] for g in groups]
        minv = [minv[g] + res[g][CHUNK:] for g in groups]
        yield
    minv = [minv[g] + _mm(minv[g], bd(nk[g])) for g in groups]
    yield

    x1 = [_mm(a_ak[g], bd(grp(S_V, g))) for g in groups]
    yield
    ma = [_mm(minv[g], jnp.concatenate([bd(at[g]), bd(x1[g].astype(BF16))], axis=1)) for g in groups]
    for g in groups:
        mid_ref[M_AM, :, cols(g)] = ma[g][:, :GROUP].astype(BF16)
        midf_ref[F_U0, :, cols(g)] = ma[g][:, GROUP:]
    for src, dst in ((S_RT, M_RT), (S_V, M_V), (S_BREM, M_BREM), (S_KREM, M_KREM)):
        mid_ref[dst] = ops_ref[src]
    midf_ref[F_BONUS] = epi_ref[S_BONUS]
    midf_ref[F_GATE] = epi_ref[S_GATE]
    mdec_ref[...] = dec_ref[...]
    yield


def _rwkv_advance(mid_ref, midf_ref, mdec_ref, pv_ref, e_ref, st_ref, o_ref):
    groups = range(N_GROUPS)
    heads = GROUP // HEAD_DIM
    cols = lambda g: slice(g * GROUP, (g + 1) * GROUP)
    grp = lambda slot, g: mid_ref[slot, :, cols(g)]
    bd, _, _, _ = _rwkv_tables()

    st = [st_ref[g] for g in groups]
    c1 = [_mm_nt(jnp.concatenate([grp(M_AM, g), grp(M_RT, g)], axis=0), bd(st[g])) for g in groups]
    yield
    u = [c1[g][:CHUNK] + midf_ref[F_U0, :, cols(g)] for g in groups]
    uv = [jnp.concatenate([bd(u[g].astype(BF16)), bd(grp(M_V, g))], axis=0) for g in groups]
    y = [c1[g][CHUNK:] + _mm(jnp.concatenate([grp(M_ARB, g), grp(M_ARK, g)], axis=1), uv[g]) for g in groups]
    upd = [_mm_tn(uv[g], jnp.concatenate([bd(grp(M_BREM, g)), bd(grp(M_KREM, g))], axis=0))
           for g in groups]
    yield
    for g in groups:
        folded = sum(upd[g][i * HEAD_DIM:(i + 1) * HEAD_DIM] for i in range(heads))
        st_ref[g] = st[g] * mdec_ref[0:1, cols(g)] + folded
    y = jnp.concatenate(y, axis=1)
    pv = lambda i: pv_ref[i:i + 1, :]
    mu = _head_sum(y, e_ref) * (1.0 / HEAD_DIM)
    yield
    d = y - mu
    var = _head_sum(d * d, e_ref) * (1.0 / HEAD_DIM)
    yield
    yn = d * lax.rsqrt(var + RWKV_GN_EPS) * pv(P_LNG) + pv(P_LNB)
    o_ref[...] = ((yn + midf_ref[F_BONUS]) * midf_ref[F_GATE]).astype(o_ref.dtype)
    yield


def _interleave(*stages):
    live = list(stages)
    while live:
        for s in list(live):
            try:
                next(s)
            except StopIteration:
                live.remove(s)


def _rwkv_kernel(rp_ref, kp_ref, vp_ref, lo_ref, pv_ref, w2_ref, a2_ref, g2_ref, e_ref, tri_ref,
                 o_ref, st_ref, prev_ref, ops_ref, epi_ref, dec_ref, mid_ref, midf_ref, mdec_ref):
    c = pl.program_id(1)

    @pl.when(c == 0)
    def _():
        for ref in (st_ref, prev_ref, epi_ref, dec_ref, midf_ref, mdec_ref):
            ref[...] = jnp.zeros(ref.shape, F32)
        ops_ref[...] = jnp.zeros(ops_ref.shape, BF16)
        mid_ref[...] = jnp.zeros(mid_ref.shape, BF16)

    def prep_stage(s, slot):
        for g in range(N_GROUPS):
            _rwkv_prep(g, rp_ref.at[s], kp_ref.at[s], vp_ref.at[s], lo_ref.at[s], pv_ref, w2_ref, a2_ref,
                       g2_ref, e_ref, tri_ref, prev_ref.at[s], ops_ref.at[s, slot], epi_ref.at[s, slot],
                       dec_ref.at[s, slot])
            yield

    def step(new, old):
        seqs = range(SEQS_PER_STEP)
        _interleave(*[_rwkv_solve(ops_ref.at[s, old], epi_ref.at[s, old], dec_ref.at[s, old],
                                  mid_ref.at[s, old], midf_ref.at[s, old], mdec_ref.at[s, old]) for s in seqs],
                    *[_rwkv_advance(mid_ref.at[s, new], midf_ref.at[s, new], mdec_ref.at[s, new], pv_ref,
                                    e_ref, st_ref.at[s], o_ref.at[s]) for s in seqs],
                    *[prep_stage(s, new) for s in seqs])
        for s in seqs:
            for col, ref in enumerate((rp_ref, kp_ref, vp_ref, lo_ref)):
                prev_ref[s, 0:1, col * D_RWKV:(col + 1) * D_RWKV] = ref[s, CHUNK - 1:CHUNK, :]

    @pl.when(c % 2 == 0)
    def _():
        step(0, 1)

    @pl.when(c % 2 == 1)
    def _():
        step(1, 0)


def _rwkv(pr, pvec, w2p, a2p, g2p, e256, tri, bsz, seq):
    nc = seq // CHUNK
    n_seq = SEQS_PER_STEP
    pr3 = pr.reshape(bsz, seq, pr.shape[-1])
    col = lambda cb: pl.BlockSpec((n_seq, CHUNK, D_RWKV), lambda b, c: (b, jnp.minimum(c, nc - 1), cb))
    full = lambda shape: pl.BlockSpec(shape, lambda b, c: (0, 0))
    out = pl.pallas_call(
        _rwkv_kernel,
        grid=(bsz // n_seq, nc + 2),
        in_specs=[col(0), col(1), col(2), col(3),
                  full((P_ROWS, D_RWKV)), full((LORA_W, D_RWKV)), full((LORA_W, D_RWKV)),
                  full((LORA_G, D_RWKV)), full((GROUP, GROUP)), full((CHUNK, CHUNK))],
        out_specs=pl.BlockSpec((n_seq, CHUNK, D_RWKV), lambda b, c: (b, jnp.maximum(c - 2, 0), 0)),
        out_shape=jax.ShapeDtypeStruct((bsz, seq, D_RWKV), BF16),
        scratch_shapes=[pltpu.VMEM((n_seq, N_GROUPS, HEAD_DIM, GROUP), F32),
                        pltpu.VMEM((n_seq, 8, 4 * D_RWKV), F32),
                        pltpu.VMEM((n_seq, 2, 7, CHUNK, D_RWKV), BF16),
                        pltpu.VMEM((n_seq, 2, 2, CHUNK, D_RWKV), F32),
                        pltpu.VMEM((n_seq, 2, 8, D_RWKV), F32),
                        pltpu.VMEM((n_seq, 2, 7, CHUNK, D_RWKV), BF16),
                        pltpu.VMEM((n_seq, 2, 3, CHUNK, D_RWKV), F32),
                        pltpu.VMEM((n_seq, 2, 8, D_RWKV), F32)],
        compiler_params=_params("arbitrary", "arbitrary"),
        name="rwkv7",
    )(pr3, pr3, pr3, pr3, pvec, w2p, a2p, g2p, e256, tri)
    return out.reshape(bsz * seq, D_RWKV)


def _out_proj_kernel(oa_ref, or_ref, w_ref, x_ref, gate_ref, g_ref, sc_ref, sh_ref, x1_ref, h2_ref):
    acc = jnp.dot(oa_ref[...], w_ref[0:D_ATTN, :], preferred_element_type=F32)
    acc = acc + jnp.dot(or_ref[...], w_ref[D_ATTN:, :], preferred_element_type=F32)
    x1 = x_ref[...] + gate_ref[...] * acc
    x1_ref[...] = x1
    h2_ref[...] = _modulated_norm(x1, g_ref[...], sc_ref[...], sh_ref[...]).astype(BF16)


def _out_proj(o_attn, o_rwkv, w_bf, x2, mod3, norm_g, seq):
    n = x2.shape[0]
    tiles_per_batch = seq // TM_OUT
    mod = lambda col: pl.BlockSpec((None, 1, D_MODEL), lambda i: (i // tiles_per_batch, 0, col))
    return pl.pallas_call(
        _out_proj_kernel,
        grid=(n // TM_OUT,),
        in_specs=[pl.BlockSpec((TM_OUT, D_ATTN), lambda i: (i, 0)),
                  pl.BlockSpec((TM_OUT, D_RWKV), lambda i: (i, 0)),
                  pl.BlockSpec((D_MODEL, D_MODEL), lambda i: (0, 0), pipeline_mode=pl.Buffered(1)),
                  pl.BlockSpec((TM_OUT, D_MODEL), lambda i: (i, 0)),
                  mod(2),
                  pl.BlockSpec((1, D_MODEL), lambda i: (0, 0)),
                  mod(4), mod(3)],
        out_specs=[pl.BlockSpec((TM_OUT, D_MODEL), lambda i: (i, 0)),
                   pl.BlockSpec((TM_OUT, D_MODEL), lambda i: (i, 0))],
        out_shape=[jax.ShapeDtypeStruct((n, D_MODEL), F32),
                   jax.ShapeDtypeStruct((n, D_MODEL), BF16)],
        compiler_params=_params("arbitrary"),
        name="out_proj",
    )(o_attn, o_rwkv, w_bf, x2, mod3, norm_g, mod3, mod3)


def _ffn_up_kernel(h_ref, wg_ref, wv_ref, cwg_ref, cwv_ref, cbg_ref, cbv_ref, o_ref,
                   wgb_ref, wvb_ref, carry_ref, *, tiles_per_batch):
    i = pl.program_id(1)

    @pl.when(i == 0)
    def _():
        wgb_ref[...] = wg_ref[...].astype(BF16)
        wvb_ref[...] = wv_ref[...].astype(BF16)
        carry_ref[...] = jnp.zeros(carry_ref.shape, F32)

    first = (i % tiles_per_batch) == 0

    def conv(up, prev, cw_ref, cb_ref):
        def taps(x, x1, x2):
            return cb_ref[...] + cw_ref[0:1, :] * x2 + cw_ref[1:2, :] * x1 + cw_ref[2:3, :] * x

        y = taps(up, pltpu.roll(up, 1, axis=0), pltpu.roll(up, 2, axis=0))
        ext = jnp.concatenate([prev, up[0:8, :]], axis=0)
        head = taps(ext, pltpu.roll(ext, 1, axis=0), pltpu.roll(ext, 2, axis=0))[8:16, :]
        return jnp.concatenate([head, y[8:, :]], axis=0)

    h = h_ref[...]
    up_g = jnp.dot(h, wgb_ref[...], preferred_element_type=F32)
    up_v = jnp.dot(h, wvb_ref[...], preferred_element_type=F32)
    gate = conv(up_g, jnp.where(first, 0.0, carry_ref[0]), cwg_ref, cbg_ref)
    val = conv(up_v, jnp.where(first, 0.0, carry_ref[1]), cwv_ref, cbv_ref)
    carry_ref[0] = up_g[TM - 8:, :]
    carry_ref[1] = up_v[TM - 8:, :]
    o_ref[...] = (gate * (1.0 / (1.0 + jnp.exp(-gate))) * val).astype(o_ref.dtype)


def _ffn_up(h2, w_up, conv_w8, conv_b, seq):
    n = h2.shape[0]
    tiles_per_batch = seq // TM
    nf = D_FF // TN
    return pl.pallas_call(
        functools.partial(_ffn_up_kernel, tiles_per_batch=tiles_per_batch),
        grid=(nf, n // TM),
        in_specs=[pl.BlockSpec((TM, D_MODEL), lambda j, i: (i, 0)),
                  pl.BlockSpec((D_MODEL, TN), lambda j, i: (0, j)),
                  pl.BlockSpec((D_MODEL, TN), lambda j, i: (0, nf + j)),
                  pl.BlockSpec((8, TN), lambda j, i: (0, j)),
                  pl.BlockSpec((8, TN), lambda j, i: (0, nf + j)),
                  pl.BlockSpec((1, TN), lambda j, i: (0, j)),
                  pl.BlockSpec((1, TN), lambda j, i: (0, nf + j))],
        out_specs=pl.BlockSpec((TM, TN), lambda j, i: (i, j)),
        out_shape=jax.ShapeDtypeStruct((n, D_FF), BF16),
        scratch_shapes=[pltpu.VMEM((D_MODEL, TN), BF16),
                        pltpu.VMEM((D_MODEL, TN), BF16),
                        pltpu.VMEM((2, 8, TN), F32)],
        compiler_params=_params("arbitrary", "arbitrary"),
        name="ffn_up",
    )(h2, w_up, w_up, conv_w8, conv_w8, conv_b, conv_b)


def _ffn_down_kernel(a_ref, w_ref, x_ref, gate_ref, o_ref, wb_ref):
    @pl.when(pl.program_id(1) == 0)
    def _():
        wb_ref[...] = w_ref[...].astype(BF16)

    acc = jnp.dot(a_ref[...], wb_ref[...], preferred_element_type=F32)
    o_ref[...] = x_ref[...] + gate_ref[...] * acc


def _ffn_down(act, w_down, x1, mod3, seq):
    n = x1.shape[0]
    tiles_per_batch = seq // TM_DOWN
    gate_col = 5 * (D_MODEL // TN)
    return pl.pallas_call(
        _ffn_down_kernel,
        grid=(D_MODEL // TN, n // TM_DOWN),
        in_specs=[pl.BlockSpec((TM_DOWN, D_FF), lambda j, i: (i, 0)),
                  pl.BlockSpec((D_FF, TN), lambda j, i: (0, j)),
                  pl.BlockSpec((TM_DOWN, TN), lambda j, i: (i, j)),
                  pl.BlockSpec((None, 1, TN), lambda j, i: (i // tiles_per_batch, 0, gate_col + j))],
        out_specs=pl.BlockSpec((TM_DOWN, TN), lambda j, i: (i, j)),
        out_shape=jax.ShapeDtypeStruct((n, D_MODEL), F32),
        scratch_shapes=[pltpu.VMEM((D_FF, TN), BF16)],
        compiler_params=_params("arbitrary", "arbitrary"),
        name="ffn_down",
    )(act, w_down, x1, mod3)


def _pad_cols(w, width):
    return jnp.pad(w, ((0, 0), (0, width - w.shape[1])))


def _pad_rows(w, height):
    return jnp.pad(w, ((0, height - w.shape[0]), (0, 0)))


def kernel(x, c, rel_bias, w_ada, b_ada, norm_mix_g, w_in, q_norm_g, k_norm_g, lambda_q1, lambda_k1,
           lambda_q2, lambda_k2, attn_subln_g, mu_rkv, mu_wag, w0, w1, w2, a0, a1, a2, g1, g2, k_k,
           k_a, r_k, ln_x_g, ln_x_b, w_out, norm_ffn_g, w_up, conv_w, conv_b, w_down):
    bsz, seq, _ = x.shape
    n = bsz * seq
    x2 = x.reshape(n, D_MODEL)

    mod = _adaln(c, w_ada[0], b_ada[0])
    mod3 = mod.reshape(bsz, 1, 6 * D_MODEL)

    idx = jnp.arange(GROUP) // HEAD_DIM
    e256 = (idx[:, None] == idx[None, :]).astype(BF16)
    tri = (jnp.arange(CHUNK)[:, None] >= jnp.arange(CHUNK)[None, :]).astype(BF16)

    gain = jnp.concatenate([jnp.tile(q_norm_g[0], D_ATTN // HEAD_DIM) * (HEAD_DIM ** -0.5 * LOG2E),
                            jnp.tile(k_norm_g[0], D_ATTN // HEAD_DIM)]).reshape(1, 2 * D_ATTN)
    mu = mu_wag[0]
    lora_w = (_pad_cols(w1[0], LORA_W), _pad_cols(a1[0], LORA_W), _pad_cols(g1[0], LORA_G))
    w_lora = jnp.concatenate([(1.0 - mu[i])[:, None] * w for i, w in enumerate(lora_w)]
                             + [mu[i][:, None] * w for i, w in enumerate(lora_w)], axis=1)
    qkv, pr = _proj(x2, norm_mix_g, mod3, w_in[0], w_lora, gain, e256, seq)

    bias_tiles, lam = _attn_tables(rel_bias, lambda_q1, lambda_k1, lambda_q2, lambda_k2)
    o_attn = _diff_attn(qkv, bias_tiles, lam, attn_subln_g, bsz, seq)

    pvec = jnp.concatenate([mu_rkv[0], w0, a0, k_k, k_a, r_k.reshape(1, D_RWKV), ln_x_g, ln_x_b,
                            jnp.zeros((P_ROWS - 10, D_RWKV), F32)], axis=0)
    o_rwkv = _rwkv(pr, pvec, _pad_rows(w2[0], LORA_W).astype(BF16), _pad_rows(a2[0], LORA_W).astype(BF16),
                   _pad_rows(g2[0], LORA_G).astype(BF16), e256, tri, bsz, seq)

    x1, h2 = _out_proj(o_attn, o_rwkv, w_out[0].astype(BF16), x2, mod3, norm_ffn_g, seq)

    act = _ffn_up(h2, w_up[0], _pad_rows(conv_w[0], 8), conv_b, seq)
    out = _ffn_down(act, w_down[0], x1, mod3, seq)
    return out.reshape(bsz, seq, D_MODEL)
```

```python
import functools
import math

import jax
import jax.numpy as jnp
from jax import lax
from jax.experimental import pallas as pl
from jax.experimental.pallas import tpu as pltpu

F32 = jnp.float32
BF16 = jnp.bfloat16

D_MODEL = 2048
D_ATTN = 1024
D_RWKV = 1024
HEAD_DIM = 64
ATTN_HEADS = 8
ATTN_V_DIM = 128
RWKV_HEADS = 16
D_FF = 5632
NUM_BUCKETS = 32
MAX_EXACT = 16
MAX_DISTANCE = 128
NORM_EPS = 1e-6
RWKV_GN_EPS = 64e-5
NEG_INF = -1e30
LOG2E = math.log2(math.e)
LAMBDA_INIT = 0.8 - 0.6 * math.exp(0.0)

LANES = 128
LORA_W = 128
LORA_G = 256
LORA_COLS = 2 * LORA_W + LORA_G

TM = 1024
TN = 512
TM_DOWN = 512
TM_OUT = 512
TQ = 256
HEADS_PER_STEP = 8
V_ROWS = ATTN_V_DIM + 16
CHUNK = 64
GROUP = 256
N_GROUPS = D_RWKV // GROUP
SEQS_PER_STEP = 2

VMEM_LIMIT = 56 * 1024 * 1024


def _mm(a, b):
    return jnp.dot(a.astype(BF16), b.astype(BF16), preferred_element_type=F32)


def _mm_nt(a, b):
    return lax.dot_general(a.astype(BF16), b.astype(BF16), (((1,), (1,)), ((), ())),
                           preferred_element_type=F32)


def _mm_tn(a, b):
    return lax.dot_general(a.astype(BF16), b.astype(BF16), (((0,), (0,)), ((), ())),
                           preferred_element_type=F32)


def _params(*sem):
    return pltpu.CompilerParams(dimension_semantics=sem, vmem_limit_bytes=VMEM_LIMIT)


def _adaln_kernel(c_ref, w_ref, b_ref, o_ref):
    c = c_ref[...]
    c_act = c * (1.0 / (1.0 + jnp.exp(-c)))
    w = w_ref[...]
    w_hi = w.astype(BF16)
    w_lo = (w - w_hi.astype(F32)).astype(BF16)
    c_hi = c_act.astype(BF16)
    c_lo = (c_act - c_hi.astype(F32)).astype(BF16)
    rows = c.shape[0]
    t = jnp.dot(jnp.concatenate([c_hi, c_lo], axis=0), w_hi, preferred_element_type=F32)
    acc = t[:rows] + t[rows:] + jnp.dot(c_hi, w_lo, preferred_element_type=F32)
    o_ref[...] = acc + b_ref[...]


def _adaln(c, w_ada, b_ada):
    bsz = c.shape[0]
    n_out = w_ada.shape[1]
    tn = 1024
    return pl.pallas_call(
        _adaln_kernel,
        grid=(n_out // tn,),
        in_specs=[pl.BlockSpec((bsz, D_MODEL), lambda j: (0, 0)),
                  pl.BlockSpec((D_MODEL, tn), lambda j: (0, j)),
                  pl.BlockSpec((1, tn), lambda j: (0, j))],
        out_specs=pl.BlockSpec((bsz, tn), lambda j: (0, j)),
        out_shape=jax.ShapeDtypeStruct((bsz, n_out), F32),
        compiler_params=_params("arbitrary"),
        name="adaln",
    )(c, w_ada, b_ada.reshape(1, n_out))


def _modulated_norm(x, g, scale, shift):
    y = x * lax.rsqrt(jnp.mean(x * x, axis=-1, keepdims=True) + NORM_EPS)
    return (y * g) * (1.0 + scale) + shift


QK_TILES = 2 * D_ATTN // TN
ATTN_TILES = 3 * D_ATTN // TN
W_IN_TILES = (3 * D_ATTN + 3 * D_RWKV) // TN
LORA_TILES = 2 * LORA_COLS // TN


def _proj_kernel(x_ref, g_ref, sc_ref, sh_ref, w_ref, wl_ref, gain_ref, e_ref, qkv_ref, pr_ref, h_ref):
    j = pl.program_id(1)

    @pl.when(j == 0)
    def _():
        h_ref[...] = _modulated_norm(x_ref[...], g_ref[...], sc_ref[...], sh_ref[...]).astype(BF16)

    def project(weights_ref):
        return jnp.dot(h_ref[...], weights_ref[...].astype(BF16), preferred_element_type=F32)

    @pl.when(j < QK_TILES)
    def _():
        acc = project(w_ref)
        half = TN // 2
        for s in range(2):
            a = acc[:, s * half:(s + 1) * half]
            ss = _mm(a * a, e_ref[...])
            y = a * lax.rsqrt(ss * (1.0 / HEAD_DIM) + NORM_EPS) * gain_ref[:, s * half:(s + 1) * half]
            qkv_ref[:, s * half:(s + 1) * half] = y.astype(qkv_ref.dtype)

    @pl.when(jnp.logical_and(j >= QK_TILES, j < ATTN_TILES))
    def _():
        qkv_ref[...] = project(w_ref).astype(qkv_ref.dtype)

    @pl.when(jnp.logical_and(j >= ATTN_TILES, j < W_IN_TILES))
    def _():
        pr_ref[...] = project(w_ref)

    @pl.when(j >= W_IN_TILES)
    def _():
        pr_ref[...] = project(wl_ref)


def _proj(x2, norm_g, mod3, w_in, w_lora, gain, e256, seq):
    n = x2.shape[0]
    tiles_per_batch = seq // TM
    batch = lambda i, j: i // tiles_per_batch
    return pl.pallas_call(
        _proj_kernel,
        grid=(n // TM, W_IN_TILES + LORA_TILES),
        in_specs=[pl.BlockSpec((TM, D_MODEL), lambda i, j: (i, 0)),
                  pl.BlockSpec((1, D_MODEL), lambda i, j: (0, 0)),
                  pl.BlockSpec((None, 1, D_MODEL), lambda i, j: (batch(i, j), 0, 1)),
                  pl.BlockSpec((None, 1, D_MODEL), lambda i, j: (batch(i, j), 0, 0)),
                  pl.BlockSpec((D_MODEL, TN), lambda i, j: (0, jnp.minimum(j, W_IN_TILES - 1))),
                  pl.BlockSpec((D_MODEL, TN), lambda i, j: (0, jnp.maximum(j - W_IN_TILES, 0))),
                  pl.BlockSpec((1, TN), lambda i, j: (0, jnp.minimum(j, QK_TILES - 1))),
                  pl.BlockSpec((TN // 2, TN // 2), lambda i, j: (0, 0))],
        out_specs=[pl.BlockSpec((TM, TN), lambda i, j: (i, jnp.minimum(j, ATTN_TILES - 1))),
                   pl.BlockSpec((TM, TN), lambda i, j: (i, jnp.maximum(j - ATTN_TILES, 0)))],
        out_shape=[jax.ShapeDtypeStruct((n, 3 * D_ATTN), BF16),
                   jax.ShapeDtypeStruct((n, 3 * D_RWKV + 2 * LORA_COLS), F32)],
        scratch_shapes=[pltpu.VMEM((TM, D_MODEL), BF16)],
        compiler_params=_params("arbitrary", "arbitrary"),
        name="proj",
    )(x2, norm_g, mod3, mod3, w_in, w_lora, gain, e256)


def _attn_tables_kernel(rb_ref, lq1_ref, lk1_ref, lq2_ref, lk2_ref, bias_ref, lam_ref):
    h = pl.program_id(0)
    rows = lax.broadcasted_iota(jnp.int32, (2 * TQ, TQ), 0)
    cols = lax.broadcasted_iota(jnp.int32, (2 * TQ, TQ), 1)
    far = rb_ref[NUM_BUCKETS - 1, h]
    dist = cols - rows + TQ
    n = jnp.maximum(dist, 0)
    nf = jnp.maximum(n, 1).astype(F32)
    large = MAX_EXACT + (jnp.log(nf / MAX_EXACT) / math.log(MAX_DISTANCE / MAX_EXACT)
                         * (NUM_BUCKETS - MAX_EXACT)).astype(jnp.int32)
    large = jnp.minimum(large, NUM_BUCKETS - 1)
    bucket = jnp.where(n < MAX_EXACT, n, large)
    bias = jnp.zeros((2 * TQ, TQ), F32)
    for b in range(NUM_BUCKETS):
        bias = jnp.where(bucket == b, rb_ref[b, h], bias)
    bias_ref[...] = jnp.where(dist >= 0, (bias - far) * LOG2E, NEG_INF)
    s1 = jnp.sum(lq1_ref[...] * lk1_ref[...], axis=1, keepdims=True)
    s2 = jnp.sum(lq2_ref[...] * lk2_ref[...], axis=1, keepdims=True)
    lam = jnp.exp(s1) - jnp.exp(s2) + LAMBDA_INIT
    lam_ref[...] = jnp.broadcast_to(lam, lam_ref.shape)


def _attn_tables(rel_bias, lq1, lk1, lq2, lk2):
    vec = pl.BlockSpec((1, HEAD_DIM), lambda h: (0, 0))
    return pl.pallas_call(
        _attn_tables_kernel,
        grid=(ATTN_HEADS,),
        in_specs=[pl.BlockSpec(memory_space=pltpu.SMEM), vec, vec, vec, vec],
        out_specs=[pl.BlockSpec((None, 2 * TQ, TQ), lambda h: (h, 0, 0)),
                   pl.BlockSpec((8, TQ), lambda h: (0, 0))],
        out_shape=[jax.ShapeDtypeStruct((ATTN_HEADS, 2 * TQ, TQ), F32),
                   jax.ShapeDtypeStruct((8, TQ), F32)],
        compiler_params=_params("arbitrary"),
        name="attn_tables",
    )(rel_bias, lq1, lk1, lq2, lk2)


def _attn_scores(qs_ref, k_ref, j):
    start = pl.multiple_of(j * TQ, TQ)
    return [lax.dot_general(k_ref[pl.ds(start, TQ), hh * ATTN_V_DIM:(hh + 1) * ATTN_V_DIM], qs_ref[hh],
                            (((1,), (1,)), ((), ())), preferred_element_type=F32)
            for hh in range(HEADS_PER_STEP)]


def _attn_update(vt_ref, m_ref, acc_ref, sts, j, biases):
    heads = range(HEADS_PER_STEP)
    if biases is not None:
        sts = [sts[hh] + jnp.concatenate([biases[hh], biases[hh]], axis=1) for hh in heads]
    m_prev = [m_ref[hh] for hh in heads]
    m_new = [jnp.maximum(m_prev[hh], jnp.max(sts[hh], axis=0, keepdims=True)) for hh in heads]
    alpha = [jnp.exp2(m_prev[hh] - m_new[hh]) for hh in heads]
    p = [jnp.exp2(sts[hh] - m_new[hh]) for hh in heads]
    start = pl.multiple_of(j * TQ, TQ)
    pv = [jnp.dot(vt_ref[hh, :, pl.ds(start, TQ)], p[hh].astype(BF16), preferred_element_type=F32)
          for hh in heads]
    for hh in heads:
        acc_ref[hh] = alpha[hh] * acc_ref[hh] + pv[hh]
        m_ref[hh] = m_new[hh]


def _diff_attn_kernel(q_ref, k_ref, v_ref, bias_ref, lam_ref, sg_ref, o_ref,
                      qs_ref, vt_ref, s_ref, m_ref, acc_ref, *, seq):
    qi = pl.program_id(2)
    heads = range(HEADS_PER_STEP)

    @pl.when(qi == 0)
    def _():
        step = 2 * TQ
        for hh in heads:
            for c in range(seq // step):
                v = v_ref[c * step:(c + 1) * step, hh * ATTN_V_DIM:(hh + 1) * ATTN_V_DIM]
                vt_ref[hh, 0:ATTN_V_DIM, c * step:(c + 1) * step] = v.astype(F32).T.astype(BF16)
            ones_row = lax.broadcasted_iota(jnp.int32, (V_ROWS - ATTN_V_DIM, seq), 0) == 0
            vt_ref[hh, ATTN_V_DIM:V_ROWS, :] = jnp.where(ones_row, 1.0, 0.0).astype(BF16)

    lane = lax.broadcasted_iota(jnp.int32, (TQ, ATTN_V_DIM), 1)
    for hh in heads:
        q = q_ref[:, hh * ATTN_V_DIM:(hh + 1) * ATTN_V_DIM]
        zero = jnp.zeros_like(q)
        qs_ref[hh, 0:TQ, :] = jnp.where(lane < HEAD_DIM, q, zero)
        qs_ref[hh, TQ:2 * TQ, :] = jnp.where(lane >= HEAD_DIM, q, zero)
    scores = functools.partial(_attn_scores, qs_ref, k_ref)
    update = functools.partial(_attn_update, vt_ref, m_ref, acc_ref)

    def put(buf, sts):
        for hh in heads:
            s_ref[buf, hh] = sts[hh]

    get = lambda buf: [s_ref[buf, hh] for hh in heads]
    bias_prev = lambda: [bias_ref[hh, 0:TQ, :] for hh in heads]
    bias_diag = lambda: [bias_ref[hh, TQ:2 * TQ, :] for hh in heads]

    n_far = jnp.maximum(qi - 1, 0)
    put(0, scores(0))
    m_ref[...] = jnp.full(m_ref.shape, NEG_INF, F32)
    acc_ref[...] = jnp.zeros(acc_ref.shape, F32)

    def far_pair(p, carry):
        j = 2 * p
        put(1, scores(j + 1))
        update(get(0), j, None)
        put(0, scores(j + 2))
        update(get(1), j + 1, None)
        return carry

    lax.fori_loop(0, n_far // 2, far_pair, 0)

    @pl.when(n_far % 2 == 1)
    def _():
        put(1, scores(qi - 1))
        update(get(0), qi - 2, None)
        put(0, scores(qi))
        update(get(1), qi - 1, bias_prev())
        update(get(0), qi, bias_diag())

    @pl.when(jnp.logical_and(qi >= 1, n_far % 2 == 0))
    def _():
        put(1, scores(qi))
        update(get(0), qi - 1, bias_prev())
        update(get(1), qi, bias_diag())

    @pl.when(qi == 0)
    def _():
        update(get(0), qi, bias_diag())

    for hh in heads:
        acc = acc_ref[hh]
        ot = acc[0:ATTN_V_DIM] / acc[ATTN_V_DIM:ATTN_V_DIM + 1]
        dt = ot[:, 0:TQ] - lam_ref[0:1, :] * ot[:, TQ:2 * TQ]
        yt = dt * lax.rsqrt(jnp.mean(dt * dt, axis=0, keepdims=True) + NORM_EPS)
        o_ref[:, hh * ATTN_V_DIM:(hh + 1) * ATTN_V_DIM] = (
            yt.T * sg_ref[...] * (1.0 - LAMBDA_INIT)).astype(o_ref.dtype)


def _diff_attn(qkv, bias_tiles, lam, subln_g, bsz, seq):
    nq = seq // TQ
    width = HEADS_PER_STEP * ATTN_V_DIM
    h_blocks = D_ATTN // width
    return pl.pallas_call(
        functools.partial(_diff_attn_kernel, seq=seq),
        grid=(bsz, ATTN_HEADS // HEADS_PER_STEP, nq),
        in_specs=[pl.BlockSpec((TQ, width), lambda b, h, i: (b * nq + i, h)),
                  pl.BlockSpec((seq, width), lambda b, h, i: (b, h_blocks + h), pipeline_mode=pl.Buffered(1)),
                  pl.BlockSpec((seq, width), lambda b, h, i: (b, 2 * h_blocks + h), pipeline_mode=pl.Buffered(1)),
                  pl.BlockSpec((HEADS_PER_STEP, 2 * TQ, TQ), lambda b, h, i: (h, 0, 0)),
                  pl.BlockSpec((8, TQ), lambda b, h, i: (0, 0)),
                  pl.BlockSpec((1, ATTN_V_DIM), lambda b, h, i: (0, 0))],
        out_specs=pl.BlockSpec((TQ, width), lambda b, h, i: (b * nq + i, h)),
        out_shape=jax.ShapeDtypeStruct((bsz * seq, D_ATTN), BF16),
        scratch_shapes=[pltpu.VMEM((HEADS_PER_STEP, 2 * TQ, ATTN_V_DIM), BF16),
                        pltpu.VMEM((HEADS_PER_STEP, V_ROWS, seq), BF16),
                        pltpu.VMEM((2, HEADS_PER_STEP, TQ, 2 * TQ), F32),
                        pltpu.VMEM((HEADS_PER_STEP, 1, 2 * TQ), F32),
                        pltpu.VMEM((HEADS_PER_STEP, V_ROWS, 2 * TQ), F32)],
        compiler_params=_params("arbitrary", "arbitrary", "arbitrary"),
        name="diff_attn",
    )(qkv, qkv, qkv, bias_tiles, lam, subln_g)


P_MU_R, P_MU_K, P_MU_V, P_W0, P_A0, P_KK, P_KA, P_RK, P_LNG, P_LNB = range(10)
P_ROWS = 16


S_AT, S_RT, S_BT, S_KT, S_BREM, S_KREM, S_V = range(7)
S_BONUS, S_GATE = range(2)
M_AM, M_ARB, M_ARK, M_RT, M_V, M_BREM, M_KREM = range(7)
F_U0, F_BONUS, F_GATE = range(3)


def _rwkv_prep(g, rp_ref, kp_ref, vp_ref, lo_ref, pv_ref, w2_ref, a2_ref, g2_ref, e_ref, tri_ref, prev_ref,
               ops_ref, epi_ref, dec_ref):
    cols = slice(g * GROUP, (g + 1) * GROUP)

    def shifted(x, prev):
        row0 = lax.broadcasted_iota(jnp.int32, x.shape, 0) == 0
        return jnp.where(row0, prev, pltpu.roll(x, 1, axis=0))

    def mixed(ref, col, mu_row):
        x = ref[:, cols]
        prev = prev_ref[0:1, col * D_RWKV + g * GROUP:col * D_RWKV + (g + 1) * GROUP]
        return x + (shifted(x, prev) - x) * pv_ref[mu_row:mu_row + 1, cols]

    pv = lambda i: pv_ref[i:i + 1, cols]
    r = mixed(rp_ref, 0, P_MU_R)
    k = mixed(kp_ref, 1, P_MU_K)
    v = mixed(vp_ref, 2, P_MU_V)

    lora = lo_ref[:, :LORA_COLS] + shifted(lo_ref[:, LORA_COLS:], prev_ref[0:1, 3 * D_RWKV + LORA_COLS:])
    w_in = pv(P_W0) + _mm(jnp.tanh(lora[:, :LORA_W]), w2_ref[:, cols])
    w = -(jnp.maximum(-w_in, 0.0) + jnp.log(1.0 + jnp.exp(-jnp.abs(w_in)))) - 0.5
    wlog = -LOG2E * jnp.exp(w)
    a = 1.0 / (1.0 + jnp.exp(-(pv(P_A0) + _mm(lora[:, LORA_W:2 * LORA_W], a2_ref[:, cols]))))
    gate = _mm(1.0 / (1.0 + jnp.exp(-lora[:, 2 * LORA_W:])), g2_ref[:, cols])

    head_sum = lambda x: _mm(x, e_ref[...])
    kk = k * pv(P_KK)
    kk = kk / jnp.maximum(jnp.sqrt(head_sum(kk * kk)), 1e-12)
    k2 = k * (1.0 + (a - 1.0) * pv(P_KA))
    b = kk * a
    cum = jnp.zeros_like(wlog)
    rest = wlog
    for _ in range(3):
        term = rest.astype(BF16)
        cum = cum + jnp.dot(tri_ref[...], term, preferred_element_type=F32)
        rest = rest - term.astype(F32)

    tot = cum[CHUNK - 1:CHUNK, :]
    e_neg = jnp.exp2(-cum)
    e_rem = jnp.exp2(tot - cum)
    staged = {S_AT: jnp.exp2(cum - wlog) * (-kk), S_RT: jnp.exp2(cum) * r, S_BT: e_neg * b, S_KT: e_neg * k2,
              S_BREM: e_rem * b, S_KREM: e_rem * k2, S_V: v}
    for slot, val in staged.items():
        ops_ref[slot, :, cols] = val.astype(BF16)
    epi_ref[S_BONUS, :, cols] = head_sum(r * k2 * pv(P_RK)) * v
    epi_ref[S_GATE, :, cols] = gate
    dec_ref[:, cols] = jnp.broadcast_to(jnp.exp2(tot), (dec_ref.shape[0], GROUP))


def _head_sum(x, e_ref):
    return jnp.concatenate([_mm(x[:, i * GROUP:(i + 1) * GROUP], e_ref[...]) for i in range(N_GROUPS)],
                           axis=1)


def _rwkv_tables():
    rows_bd = lax.broadcasted_iota(jnp.int32, (GROUP, GROUP), 0) // HEAD_DIM
    lanes_bd = lax.broadcasted_iota(jnp.int32, (GROUP, GROUP), 1) // HEAD_DIM
    bd_mask = rows_bd == lanes_bd
    t_idx = lax.broadcasted_iota(jnp.int32, (CHUNK, GROUP), 0)
    s_idx = lax.broadcasted_iota(jnp.int32, (CHUNK, GROUP), 1) % HEAD_DIM

    def bd(x):
        return jnp.where(bd_mask, jnp.concatenate([x] * (GROUP // HEAD_DIM), axis=0), jnp.zeros((), x.dtype))

    return bd, s_idx < t_idx, s_idx <= t_idx, jnp.where(s_idx == t_idx, 1.0, 0.0).astype(F32)


def _rwkv_solve(ops_ref, epi_ref, dec_ref, mid_ref, midf_ref, mdec_ref):
    groups = range(N_GROUPS)
    cols = lambda g: slice(g * GROUP, (g + 1) * GROUP)
    grp = lambda slot, g: ops_ref[slot, :, cols(g)]
    bd, m_strict, m_incl, eye = _rwkv_tables()

    at = [grp(S_AT, g) for g in groups]
    aa = [_mm_nt(jnp.concatenate([at[g], grp(S_RT, g)], axis=0),
                 jnp.concatenate([bd(grp(S_BT, g)), bd(grp(S_KT, g))], axis=0)) for g in groups]
    a_ab = [jnp.where(m_strict, aa[g][:CHUNK, :GROUP], 0.0) for g in groups]
    a_ak = [jnp.where(m_strict, aa[g][:CHUNK, GROUP:], 0.0) for g in groups]
    for g in groups:
        mid_ref[M_ARB, :, cols(g)] = jnp.where(m_incl, aa[g][CHUNK:, :GROUP], 0.0).astype(BF16)
        mid_ref[M_ARK, :, cols(g)] = jnp.where(m_incl, aa[g][CHUNK:, GROUP:], 0.0).astype(BF16)
    yield

    minv = [eye + a_ab[g] for g in groups]
    nk = [_mm(a_ab[g], bd(a_ab[g])) for g in groups]
    yield
    for _ in range(int(math.log2(CHUNK)) - 2):
        res = [_mm(jnp.concatenate([nk[g], minv[g]], axis=0), bd(nk[g])) for g in groups]
        nk = [res[g][:CHUNK] for g in groups]
        minv = [minv[g] + res[g][CHUNK:] for g in groups]
        yield
    minv = [minv[g] + _mm(minv[g], bd(nk[g])) for g in groups]
    yield

    x1 = [_mm(a_ak[g], bd(grp(S_V, g))) for g in groups]
    yield
    ma = [_mm(minv[g], jnp.concatenate([bd(at[g]), bd(x1[g].astype(BF16))], axis=1)) for g in groups]
    for g in groups:
        mid_ref[M_AM, :, cols(g)] = ma[g][:, :GROUP].astype(BF16)
        midf_ref[F_U0, :, cols(g)] = ma[g][:, GROUP:]
    for src, dst in ((S_RT, M_RT), (S_V, M_V), (S_BREM, M_BREM), (S_KREM, M_KREM)):
        mid_ref[dst] = ops_ref[src]
    midf_ref[F_BONUS] = epi_ref[S_BONUS]
    midf_ref[F_GATE] = epi_ref[S_GATE]
    mdec_ref[...] = dec_ref[...]
    yield


def _rwkv_advance(mid_ref, midf_ref, mdec_ref, pv_ref, e_ref, st_ref, o_ref):
    groups = range(N_GROUPS)
    heads = GROUP // HEAD_DIM
    cols = lambda g: slice(g * GROUP, (g + 1) * GROUP)
    grp = lambda slot, g: mid_ref[slot, :, cols(g)]
    bd, _, _, _ = _rwkv_tables()

    st = [st_ref[g] for g in groups]
    c1 = [_mm_nt(jnp.concatenate([grp(M_AM, g), grp(M_RT, g)], axis=0), bd(st[g])) for g in groups]
    yield
    u = [c1[g][:CHUNK] + midf_ref[F_U0, :, cols(g)] for g in groups]
    uv = [jnp.concatenate([bd(u[g].astype(BF16)), bd(grp(M_V, g))], axis=0) for g in groups]
    y = [c1[g][CHUNK:] + _mm(jnp.concatenate([grp(M_ARB, g), grp(M_ARK, g)], axis=1), uv[g]) for g in groups]

    pair_rows = lax.broadcasted_iota(jnp.int32, (2 * heads * CHUNK, GROUP), 0) // (2 * CHUNK)
    pair_mask = pair_rows == lax.broadcasted_iota(jnp.int32, (2 * heads * CHUNK, GROUP), 1) // HEAD_DIM

    def state_delta(g):
        uv_t = jnp.concatenate([u[g], grp(M_V, g).astype(F32)], axis=0).T
        lhs = jnp.concatenate([uv_t[i * HEAD_DIM:(i + 1) * HEAD_DIM, :] for i in range(heads)], axis=1)
        bk = jnp.concatenate([grp(M_BREM, g), grp(M_KREM, g)], axis=0)
        rhs = jnp.where(pair_mask, jnp.concatenate([bk] * heads, axis=0), jnp.zeros((), BF16))
        return _mm(lhs, rhs)

    upd = [state_delta(g) for g in groups]
    yield
    for g in groups:
        st_ref[g] = st[g] * mdec_ref[0:1, cols(g)] + upd[g]
    y = jnp.concatenate(y, axis=1)
    pv = lambda i: pv_ref[i:i + 1, :]
    mu = _head_sum(y, e_ref) * (1.0 / HEAD_DIM)
    yield
    d = y - mu
    var = _head_sum(d * d, e_ref) * (1.0 / HEAD_DIM)
    yield
    yn = d * lax.rsqrt(var + RWKV_GN_EPS) * pv(P_LNG) + pv(P_LNB)
    o_ref[...] = ((yn + midf_ref[F_BONUS]) * midf_ref[F_GATE]).astype(o_ref.dtype)
    yield


def _interleave(*stages):
    live = list(stages)
    while live:
        for s in list(live):
            try:
                next(s)
            except StopIteration:
                live.remove(s)


def _rwkv_kernel(rp_ref, kp_ref, vp_ref, lo_ref, pv_ref, w2_ref, a2_ref, g2_ref, e_ref, tri_ref,
                 o_ref, st_ref, prev_ref, ops_ref, epi_ref, dec_ref, mid_ref, midf_ref, mdec_ref):
    c = pl.program_id(1)

    @pl.when(c == 0)
    def _():
        for ref in (st_ref, prev_ref, epi_ref, dec_ref, midf_ref, mdec_ref):
            ref[...] = jnp.zeros(ref.shape, F32)
        ops_ref[...] = jnp.zeros(ops_ref.shape, BF16)
        mid_ref[...] = jnp.zeros(mid_ref.shape, BF16)

    def prep_stage(s, slot):
        for g in range(N_GROUPS):
            _rwkv_prep(g, rp_ref.at[s], kp_ref.at[s], vp_ref.at[s], lo_ref.at[s], pv_ref, w2_ref, a2_ref,
                       g2_ref, e_ref, tri_ref, prev_ref.at[s], ops_ref.at[s, slot], epi_ref.at[s, slot],
                       dec_ref.at[s, slot])
            yield

    def step(new, old):
        seqs = range(SEQS_PER_STEP)
        _interleave(*[prep_stage(s, new) for s in seqs],
                    *[_rwkv_solve(ops_ref.at[s, old], epi_ref.at[s, old], dec_ref.at[s, old],
                                  mid_ref.at[s, old], midf_ref.at[s, old], mdec_ref.at[s, old]) for s in seqs],
                    *[_rwkv_advance(mid_ref.at[s, new], midf_ref.at[s, new], mdec_ref.at[s, new], pv_ref,
                                    e_ref, st_ref.at[s], o_ref.at[s]) for s in seqs])
        for s in seqs:
            for col, ref in enumerate((rp_ref, kp_ref, vp_ref, lo_ref)):
                prev_ref[s, 0:1, col * D_RWKV:(col + 1) * D_RWKV] = ref[s, CHUNK - 1:CHUNK, :]

    @pl.when(c % 2 == 0)
    def _():
        step(0, 1)

    @pl.when(c % 2 == 1)
    def _():
        step(1, 0)


def _rwkv(pr, pvec, w2p, a2p, g2p, e256, tri, bsz, seq):
    nc = seq // CHUNK
    n_seq = SEQS_PER_STEP
    pr3 = pr.reshape(bsz, seq, pr.shape[-1])
    col = lambda cb: pl.BlockSpec((n_seq, CHUNK, D_RWKV), lambda b, c: (b, jnp.minimum(c, nc - 1), cb))
    full = lambda shape: pl.BlockSpec(shape, lambda b, c: (0, 0))
    out = pl.pallas_call(
        _rwkv_kernel,
        grid=(bsz // n_seq, nc + 2),
        in_specs=[col(0), col(1), col(2), col(3),
                  full((P_ROWS, D_RWKV)), full((LORA_W, D_RWKV)), full((LORA_W, D_RWKV)),
                  full((LORA_G, D_RWKV)), full((GROUP, GROUP)), full((CHUNK, CHUNK))],
        out_specs=pl.BlockSpec((n_seq, CHUNK, D_RWKV), lambda b, c: (b, jnp.maximum(c - 2, 0), 0)),
        out_shape=jax.ShapeDtypeStruct((bsz, seq, D_RWKV), BF16),
        scratch_shapes=[pltpu.VMEM((n_seq, N_GROUPS, HEAD_DIM, GROUP), F32),
                        pltpu.VMEM((n_seq, 8, 4 * D_RWKV), F32),
                        pltpu.VMEM((n_seq, 2, 7, CHUNK, D_RWKV), BF16),
                        pltpu.VMEM((n_seq, 2, 2, CHUNK, D_RWKV), F32),
                        pltpu.VMEM((n_seq, 2, 8, D_RWKV), F32),
                        pltpu.VMEM((n_seq, 2, 7, CHUNK, D_RWKV), BF16),
                        pltpu.VMEM((n_seq, 2, 3, CHUNK, D_RWKV), F32),
                        pltpu.VMEM((n_seq, 2, 8, D_RWKV), F32)],
        compiler_params=_params("arbitrary", "arbitrary"),
        name="rwkv7",
    )(pr3, pr3, pr3, pr3, pvec, w2p, a2p, g2p, e256, tri)
    return out.reshape(bsz * seq, D_RWKV)


def _out_proj_kernel(oa_ref, or_ref, w_ref, x_ref, gate_ref, g_ref, sc_ref, sh_ref, x1_ref, h2_ref):
    acc = jnp.dot(oa_ref[...], w_ref[0:D_ATTN, :], preferred_element_type=F32)
    acc = acc + jnp.dot(or_ref[...], w_ref[D_ATTN:, :], preferred_element_type=F32)
    x1 = x_ref[...] + gate_ref[...] * acc
    x1_ref[...] = x1
    h2_ref[...] = _modulated_norm(x1, g_ref[...], sc_ref[...], sh_ref[...]).astype(BF16)


def _out_proj(o_attn, o_rwkv, w_bf, x2, mod3, norm_g, seq):
    n = x2.shape[0]
    tiles_per_batch = seq // TM_OUT
    mod = lambda col: pl.BlockSpec((None, 1, D_MODEL), lambda i: (i // tiles_per_batch, 0, col))
    return pl.pallas_call(
        _out_proj_kernel,
        grid=(n // TM_OUT,),
        in_specs=[pl.BlockSpec((TM_OUT, D_ATTN), lambda i: (i, 0)),
                  pl.BlockSpec((TM_OUT, D_RWKV), lambda i: (i, 0)),
                  pl.BlockSpec((D_MODEL, D_MODEL), lambda i: (0, 0), pipeline_mode=pl.Buffered(1)),
                  pl.BlockSpec((TM_OUT, D_MODEL), lambda i: (i, 0)),
                  mod(2),
                  pl.BlockSpec((1, D_MODEL), lambda i: (0, 0)),
                  mod(4), mod(3)],
        out_specs=[pl.BlockSpec((TM_OUT, D_MODEL), lambda i: (i, 0)),
                   pl.BlockSpec((TM_OUT, D_MODEL), lambda i: (i, 0))],
        out_shape=[jax.ShapeDtypeStruct((n, D_MODEL), F32),
                   jax.ShapeDtypeStruct((n, D_MODEL), BF16)],
        compiler_params=_params("arbitrary"),
        name="out_proj",
    )(o_attn, o_rwkv, w_bf, x2, mod3, norm_g, mod3, mod3)


def _ffn_up_kernel(h_ref, wg_ref, wv_ref, cwg_ref, cwv_ref, cbg_ref, cbv_ref, o_ref,
                   wgb_ref, wvb_ref, carry_ref, *, tiles_per_batch):
    i = pl.program_id(1)

    @pl.when(i == 0)
    def _():
        wgb_ref[...] = wg_ref[...].astype(BF16)
        wvb_ref[...] = wv_ref[...].astype(BF16)
        carry_ref[...] = jnp.zeros(carry_ref.shape, F32)

    first = (i % tiles_per_batch) == 0

    def conv(up, prev, cw_ref, cb_ref):
        def taps(x, x1, x2):
            return cb_ref[...] + cw_ref[0:1, :] * x2 + cw_ref[1:2, :] * x1 + cw_ref[2:3, :] * x

        y = taps(up, pltpu.roll(up, 1, axis=0), pltpu.roll(up, 2, axis=0))
        ext = jnp.concatenate([prev, up[0:8, :]], axis=0)
        head = taps(ext, pltpu.roll(ext, 1, axis=0), pltpu.roll(ext, 2, axis=0))[8:16, :]
        return jnp.concatenate([head, y[8:, :]], axis=0)

    h = h_ref[...]
    up_g = jnp.dot(h, wgb_ref[...], preferred_element_type=F32)
    up_v = jnp.dot(h, wvb_ref[...], preferred_element_type=F32)
    gate = conv(up_g, jnp.where(first, 0.0, carry_ref[0]), cwg_ref, cbg_ref)
    val = conv(up_v, jnp.where(first, 0.0, carry_ref[1]), cwv_ref, cbv_ref)
    carry_ref[0] = up_g[TM - 8:, :]
    carry_ref[1] = up_v[TM - 8:, :]
    o_ref[...] = (gate * (1.0 / (1.0 + jnp.exp(-gate))) * val).astype(o_ref.dtype)


def _ffn_up(h2, w_up, conv_w8, conv_b, seq):
    n = h2.shape[0]
    tiles_per_batch = seq // TM
    nf = D_FF // TN
    return pl.pallas_call(
        functools.partial(_ffn_up_kernel, tiles_per_batch=tiles_per_batch),
        grid=(nf, n // TM),
        in_specs=[pl.BlockSpec((TM, D_MODEL), lambda j, i: (i, 0)),
                  pl.BlockSpec((D_MODEL, TN), lambda j, i: (0, j)),
                  pl.BlockSpec((D_MODEL, TN), lambda j, i: (0, nf + j)),
                  pl.BlockSpec((8, TN), lambda j, i: (0, j)),
                  pl.BlockSpec((8, TN), lambda j, i: (0, nf + j)),
                  pl.BlockSpec((1, TN), lambda j, i: (0, j)),
                  pl.BlockSpec((1, TN), lambda j, i: (0, nf + j))],
        out_specs=pl.BlockSpec((TM, TN), lambda j, i: (i, j)),
        out_shape=jax.ShapeDtypeStruct((n, D_FF), BF16),
        scratch_shapes=[pltpu.VMEM((D_MODEL, TN), BF16),
                        pltpu.VMEM((D_MODEL, TN), BF16),
                        pltpu.VMEM((2, 8, TN), F32)],
        compiler_params=_params("arbitrary", "arbitrary"),
        name="ffn_up",
    )(h2, w_up, w_up, conv_w8, conv_w8, conv_b, conv_b)


def _ffn_down_kernel(a_ref, w_ref, x_ref, gate_ref, o_ref, wb_ref):
    @pl.when(pl.program_id(1) == 0)
    def _():
        wb_ref[...] = w_ref[...].astype(BF16)

    acc = jnp.dot(a_ref[...], wb_ref[...], preferred_element_type=F32)
    o_ref[...] = x_ref[...] + gate_ref[...] * acc


def _ffn_down(act, w_down, x1, mod3, seq):
    n = x1.shape[0]
    tiles_per_batch = seq // TM_DOWN
    gate_col = 5 * (D_MODEL // TN)
    return pl.pallas_call(
        _ffn_down_kernel,
        grid=(D_MODEL // TN, n // TM_DOWN),
        in_specs=[pl.BlockSpec((TM_DOWN, D_FF), lambda j, i: (i, 0)),
                  pl.BlockSpec((D_FF, TN), lambda j, i: (0, j)),
                  pl.BlockSpec((TM_DOWN, TN), lambda j, i: (i, j)),
                  pl.BlockSpec((None, 1, TN), lambda j, i: (i // tiles_per_batch, 0, gate_col + j))],
        out_specs=pl.BlockSpec((TM_DOWN, TN), lambda j, i: (i, j)),
        out_shape=jax.ShapeDtypeStruct((n, D_MODEL), F32),
        scratch_shapes=[pltpu.VMEM((D_FF, TN), BF16)],
        compiler_params=_params("arbitrary", "arbitrary"),
        name="ffn_down",
    )(act, w_down, x1, mod3)


def _pad_cols(w, width):
    return jnp.pad(w, ((0, 0), (0, width - w.shape[1])))


def _pad_rows(w, height):
    return jnp.pad(w, ((0, height - w.shape[0]), (0, 0)))


def kernel(x, c, rel_bias, w_ada, b_ada, norm_mix_g, w_in, q_norm_g, k_norm_g, lambda_q1, lambda_k1,
           lambda_q2, lambda_k2, attn_subln_g, mu_rkv, mu_wag, w0, w1, w2, a0, a1, a2, g1, g2, k_k,
           k_a, r_k, ln_x_g, ln_x_b, w_out, norm_ffn_g, w_up, conv_w, conv_b, w_down):
    bsz, seq, _ = x.shape
    n = bsz * seq
    x2 = x.reshape(n, D_MODEL)

    mod = _adaln(c, w_ada[0], b_ada[0])
    mod3 = mod.reshape(bsz, 1, 6 * D_MODEL)

    idx = jnp.arange(GROUP) // HEAD_DIM
    e256 = (idx[:, None] == idx[None, :]).astype(BF16)
    tri = (jnp.arange(CHUNK)[:, None] >= jnp.arange(CHUNK)[None, :]).astype(BF16)

    gain = jnp.concatenate([jnp.tile(q_norm_g[0], D_ATTN // HEAD_DIM) * (HEAD_DIM ** -0.5 * LOG2E),
                            jnp.tile(k_norm_g[0], D_ATTN // HEAD_DIM)]).reshape(1, 2 * D_ATTN)
    mu = mu_wag[0]
    lora_w = (_pad_cols(w1[0], LORA_W), _pad_cols(a1[0], LORA_W), _pad_cols(g1[0], LORA_G))
    w_lora = jnp.concatenate([(1.0 - mu[i])[:, None] * w for i, w in enumerate(lora_w)]
                             + [mu[i][:, None] * w for i, w in enumerate(lora_w)], axis=1)
    qkv, pr = _proj(x2, norm_mix_g, mod3, w_in[0], w_lora, gain, e256, seq)

    bias_tiles, lam = _attn_tables(rel_bias, lambda_q1, lambda_k1, lambda_q2, lambda_k2)
    o_attn = _diff_attn(qkv, bias_tiles, lam, attn_subln_g, bsz, seq)

    pvec = jnp.concatenate([mu_rkv[0], w0, a0, k_k, k_a, r_k.reshape(1, D_RWKV), ln_x_g, ln_x_b,
                            jnp.zeros((P_ROWS - 10, D_RWKV), F32)], axis=0)
    o_rwkv = _rwkv(pr, pvec, _pad_rows(w2[0], LORA_W).astype(BF16), _pad_rows(a2[0], LORA_W).astype(BF16),
                   _pad_rows(g2[0], LORA_G).astype(BF16), e256, tri, bsz, seq)

    x1, h2 = _out_proj(o_attn, o_rwkv, w_out[0].astype(BF16), x2, mod3, norm_ffn_g, seq)

    act = _ffn_up(h2, w_up[0], _pad_rows(conv_w[0], 8), conv_b, seq)
    out = _ffn_down(act, w_down[0], x1, mod3, seq)
    return out.reshape(bsz, seq, D_MODEL)
```

```python
import functools
import math

import jax
import jax.numpy as jnp
from jax import lax
from jax.experimental import pallas as pl
from jax.experimental.pallas import tpu as pltpu

F32 = jnp.float32
BF16 = jnp.bfloat16

D_MODEL = 2048
D_ATTN = 1024
D_RWKV = 1024
HEAD_DIM = 64
ATTN_HEADS = 8
ATTN_V_DIM = 128
RWKV_HEADS = 16
D_FF = 5632
NUM_BUCKETS = 32
MAX_EXACT = 16
MAX_DISTANCE = 128
NORM_EPS = 1e-6
RWKV_GN_EPS = 64e-5
NEG_INF = -1e30
LOG2E = math.log2(math.e)
LAMBDA_INIT = 0.8 - 0.6 * math.exp(0.0)

LANES = 128
LORA_W = 128
LORA_G = 256
LORA_COLS = 2 * LORA_W + LORA_G

TM = 1024
TN = 512
TM_DOWN = 512
TM_OUT = 512
TQ = 256
HEADS_PER_STEP = 8
V_ROWS = ATTN_V_DIM + 16
CHUNK = 64
GROUP = 256
N_GROUPS = D_RWKV // GROUP
PREFIX_TERMS = 3
SEQS_PER_STEP = 2

VMEM_LIMIT = 56 * 1024 * 1024


def _mm(a, b):
    return jnp.dot(a.astype(BF16), b.astype(BF16), preferred_element_type=F32)


def _mm_nt(a, b):
    return lax.dot_general(a.astype(BF16), b.astype(BF16), (((1,), (1,)), ((), ())),
                           preferred_element_type=F32)


def _mm_tn(a, b):
    return lax.dot_general(a.astype(BF16), b.astype(BF16), (((0,), (0,)), ((), ())),
                           preferred_element_type=F32)


def _params(*sem):
    return pltpu.CompilerParams(dimension_semantics=sem, vmem_limit_bytes=VMEM_LIMIT)


def _adaln_kernel(c_ref, w_ref, b_ref, o_ref):
    c = c_ref[...]
    c_act = c * (1.0 / (1.0 + jnp.exp(-c)))
    w = w_ref[...]
    w_hi = w.astype(BF16)
    w_lo = (w - w_hi.astype(F32)).astype(BF16)
    c_hi = c_act.astype(BF16)
    c_lo = (c_act - c_hi.astype(F32)).astype(BF16)
    rows = c.shape[0]
    t = jnp.dot(jnp.concatenate([c_hi, c_lo], axis=0), w_hi, preferred_element_type=F32)
    acc = t[:rows] + t[rows:] + jnp.dot(c_hi, w_lo, preferred_element_type=F32)
    o_ref[...] = acc + b_ref[...]


def _adaln(c, w_ada, b_ada):
    bsz = c.shape[0]
    n_out = w_ada.shape[1]
    tn = 1024
    return pl.pallas_call(
        _adaln_kernel,
        grid=(n_out // tn,),
        in_specs=[pl.BlockSpec((bsz, D_MODEL), lambda j: (0, 0)),
                  pl.BlockSpec((D_MODEL, tn), lambda j: (0, j)),
                  pl.BlockSpec((1, tn), lambda j: (0, j))],
        out_specs=pl.BlockSpec((bsz, tn), lambda j: (0, j)),
        out_shape=jax.ShapeDtypeStruct((bsz, n_out), F32),
        compiler_params=_params("arbitrary"),
        name="adaln",
    )(c, w_ada, b_ada.reshape(1, n_out))


def _modulated_norm(x, g, scale, shift):
    y = x * lax.rsqrt(jnp.mean(x * x, axis=-1, keepdims=True) + NORM_EPS)
    return (y * g) * (1.0 + scale) + shift


QK_TILES = 2 * D_ATTN // TN
ATTN_TILES = 3 * D_ATTN // TN
W_IN_TILES = (3 * D_ATTN + 3 * D_RWKV) // TN
LORA_TILES = 2 * LORA_COLS // TN


def _proj_kernel(x_ref, g_ref, sc_ref, sh_ref, w_ref, wl_ref, gain_ref, e_ref, qkv_ref, pr_ref, h_ref):
    j = pl.program_id(1)

    @pl.when(j == 0)
    def _():
        h_ref[...] = _modulated_norm(x_ref[...], g_ref[...], sc_ref[...], sh_ref[...]).astype(BF16)

    def project(weights_ref):
        return jnp.dot(h_ref[...], weights_ref[...].astype(BF16), preferred_element_type=F32)

    @pl.when(j < QK_TILES)
    def _():
        acc = project(w_ref)
        half = TN // 2
        for s in range(2):
            a = acc[:, s * half:(s + 1) * half]
            ss = _mm(a * a, e_ref[...])
            y = a * lax.rsqrt(ss * (1.0 / HEAD_DIM) + NORM_EPS) * gain_ref[:, s * half:(s + 1) * half]
            qkv_ref[:, s * half:(s + 1) * half] = y.astype(qkv_ref.dtype)

    @pl.when(jnp.logical_and(j >= QK_TILES, j < ATTN_TILES))
    def _():
        qkv_ref[...] = project(w_ref).astype(qkv_ref.dtype)

    @pl.when(jnp.logical_and(j >= ATTN_TILES, j < W_IN_TILES))
    def _():
        pr_ref[...] = project(w_ref)

    @pl.when(j >= W_IN_TILES)
    def _():
        pr_ref[...] = project(wl_ref)


def _proj(x2, norm_g, mod3, w_in, w_lora, gain, e256, seq):
    n = x2.shape[0]
    tiles_per_batch = seq // TM
    batch = lambda i, j: i // tiles_per_batch
    return pl.pallas_call(
        _proj_kernel,
        grid=(n // TM, W_IN_TILES + LORA_TILES),
        in_specs=[pl.BlockSpec((TM, D_MODEL), lambda i, j: (i, 0)),
                  pl.BlockSpec((1, D_MODEL), lambda i, j: (0, 0)),
                  pl.BlockSpec((None, 1, D_MODEL), lambda i, j: (batch(i, j), 0, 1)),
                  pl.BlockSpec((None, 1, D_MODEL), lambda i, j: (batch(i, j), 0, 0)),
                  pl.BlockSpec((D_MODEL, TN), lambda i, j: (0, jnp.minimum(j, W_IN_TILES - 1))),
                  pl.BlockSpec((D_MODEL, TN), lambda i, j: (0, jnp.maximum(j - W_IN_TILES, 0))),
                  pl.BlockSpec((1, TN), lambda i, j: (0, jnp.minimum(j, QK_TILES - 1))),
                  pl.BlockSpec((TN // 2, TN // 2), lambda i, j: (0, 0))],
        out_specs=[pl.BlockSpec((TM, TN), lambda i, j: (i, jnp.minimum(j, ATTN_TILES - 1))),
                   pl.BlockSpec((TM, TN), lambda i, j: (i, jnp.maximum(j - ATTN_TILES, 0)))],
        out_shape=[jax.ShapeDtypeStruct((n, 3 * D_ATTN), BF16),
                   jax.ShapeDtypeStruct((n, 3 * D_RWKV + 2 * LORA_COLS), F32)],
        scratch_shapes=[pltpu.VMEM((TM, D_MODEL), BF16)],
        compiler_params=_params("arbitrary", "arbitrary"),
        name="proj",
    )(x2, norm_g, mod3, mod3, w_in, w_lora, gain, e256)


def _attn_tables_kernel(rb_ref, lq1_ref, lk1_ref, lq2_ref, lk2_ref, bias_ref, lam_ref):
    h = pl.program_id(0)
    rows = lax.broadcasted_iota(jnp.int32, (2 * TQ, TQ), 0)
    cols = lax.broadcasted_iota(jnp.int32, (2 * TQ, TQ), 1)
    far = rb_ref[NUM_BUCKETS - 1, h]
    dist = cols - rows + TQ
    n = jnp.maximum(dist, 0)
    nf = jnp.maximum(n, 1).astype(F32)
    large = MAX_EXACT + (jnp.log(nf / MAX_EXACT) / math.log(MAX_DISTANCE / MAX_EXACT)
                         * (NUM_BUCKETS - MAX_EXACT)).astype(jnp.int32)
    large = jnp.minimum(large, NUM_BUCKETS - 1)
    bucket = jnp.where(n < MAX_EXACT, n, large)
    bias = jnp.zeros((2 * TQ, TQ), F32)
    for b in range(NUM_BUCKETS):
        bias = jnp.where(bucket == b, rb_ref[b, h], bias)
    bias_ref[...] = jnp.where(dist >= 0, (bias - far) * LOG2E, NEG_INF)
    s1 = jnp.sum(lq1_ref[...] * lk1_ref[...], axis=1, keepdims=True)
    s2 = jnp.sum(lq2_ref[...] * lk2_ref[...], axis=1, keepdims=True)
    lam = jnp.exp(s1) - jnp.exp(s2) + LAMBDA_INIT
    lam_ref[...] = jnp.broadcast_to(lam, lam_ref.shape)


def _attn_tables(rel_bias, lq1, lk1, lq2, lk2):
    vec = pl.BlockSpec((1, HEAD_DIM), lambda h: (0, 0))
    return pl.pallas_call(
        _attn_tables_kernel,
        grid=(ATTN_HEADS,),
        in_specs=[pl.BlockSpec(memory_space=pltpu.SMEM), vec, vec, vec, vec],
        out_specs=[pl.BlockSpec((None, 2 * TQ, TQ), lambda h: (h, 0, 0)),
                   pl.BlockSpec((8, TQ), lambda h: (0, 0))],
        out_shape=[jax.ShapeDtypeStruct((ATTN_HEADS, 2 * TQ, TQ), F32),
                   jax.ShapeDtypeStruct((8, TQ), F32)],
        compiler_params=_params("arbitrary"),
        name="attn_tables",
    )(rel_bias, lq1, lk1, lq2, lk2)


def _attn_scores(qs_ref, k_ref, j):
    start = pl.multiple_of(j * TQ, TQ)
    return [lax.dot_general(k_ref[pl.ds(start, TQ), hh * ATTN_V_DIM:(hh + 1) * ATTN_V_DIM], qs_ref[hh],
                            (((1,), (1,)), ((), ())), preferred_element_type=F32)
            for hh in range(HEADS_PER_STEP)]


def _attn_update(vt_ref, m_ref, acc_ref, sts, j, biases):
    heads = range(HEADS_PER_STEP)
    if biases is not None:
        sts = [sts[hh] + jnp.concatenate([biases[hh], biases[hh]], axis=1) for hh in heads]
    m_prev = [m_ref[hh] for hh in heads]
    m_new = [jnp.maximum(m_prev[hh], jnp.max(sts[hh], axis=0, keepdims=True)) for hh in heads]
    alpha = [jnp.exp2(m_prev[hh] - m_new[hh]) for hh in heads]
    p = [jnp.exp2(sts[hh] - m_new[hh]) for hh in heads]
    start = pl.multiple_of(j * TQ, TQ)
    pv = [jnp.dot(vt_ref[hh, :, pl.ds(start, TQ)], p[hh].astype(BF16), preferred_element_type=F32)
          for hh in heads]
    for hh in heads:
        acc_ref[hh] = alpha[hh] * acc_ref[hh] + pv[hh]
        m_ref[hh] = m_new[hh]


def _diff_attn_kernel(q_ref, k_ref, v_ref, bias_ref, lam_ref, sg_ref, o_ref,
                      qs_ref, vt_ref, s_ref, m_ref, acc_ref, *, seq):
    qi = pl.program_id(2)
    heads = range(HEADS_PER_STEP)

    @pl.when(qi == 0)
    def _():
        step = 2 * TQ
        for hh in heads:
            for c in range(seq // step):
                v = v_ref[c * step:(c + 1) * step, hh * ATTN_V_DIM:(hh + 1) * ATTN_V_DIM]
                vt_ref[hh, 0:ATTN_V_DIM, c * step:(c + 1) * step] = v.astype(F32).T.astype(BF16)
            ones_row = lax.broadcasted_iota(jnp.int32, (V_ROWS - ATTN_V_DIM, seq), 0) == 0
            vt_ref[hh, ATTN_V_DIM:V_ROWS, :] = jnp.where(ones_row, 1.0, 0.0).astype(BF16)

    lane = lax.broadcasted_iota(jnp.int32, (TQ, ATTN_V_DIM), 1)
    for hh in heads:
        q = q_ref[:, hh * ATTN_V_DIM:(hh + 1) * ATTN_V_DIM]
        zero = jnp.zeros_like(q)
        qs_ref[hh, 0:TQ, :] = jnp.where(lane < HEAD_DIM, q, zero)
        qs_ref[hh, TQ:2 * TQ, :] = jnp.where(lane >= HEAD_DIM, q, zero)
    scores = functools.partial(_attn_scores, qs_ref, k_ref)
    update = functools.partial(_attn_update, vt_ref, m_ref, acc_ref)

    def put(buf, sts):
        for hh in heads:
            s_ref[buf, hh] = sts[hh]

    get = lambda buf: [s_ref[buf, hh] for hh in heads]
    bias_prev = lambda: [bias_ref[hh, 0:TQ, :] for hh in heads]
    bias_diag = lambda: [bias_ref[hh, TQ:2 * TQ, :] for hh in heads]

    n_far = jnp.maximum(qi - 1, 0)
    put(0, scores(0))
    m_ref[...] = jnp.full(m_ref.shape, NEG_INF, F32)
    acc_ref[...] = jnp.zeros(acc_ref.shape, F32)

    def far_pair(p, carry):
        j = 2 * p
        put(1, scores(j + 1))
        update(get(0), j, None)
        put(0, scores(j + 2))
        update(get(1), j + 1, None)
        return carry

    lax.fori_loop(0, n_far // 2, far_pair, 0)

    @pl.when(n_far % 2 == 1)
    def _():
        put(1, scores(qi - 1))
        update(get(0), qi - 2, None)
        put(0, scores(qi))
        update(get(1), qi - 1, bias_prev())
        update(get(0), qi, bias_diag())

    @pl.when(jnp.logical_and(qi >= 1, n_far % 2 == 0))
    def _():
        put(1, scores(qi))
        update(get(0), qi - 1, bias_prev())
        update(get(1), qi, bias_diag())

    @pl.when(qi == 0)
    def _():
        update(get(0), qi, bias_diag())

    for hh in heads:
        acc = acc_ref[hh]
        ot = acc[0:ATTN_V_DIM] / acc[ATTN_V_DIM:ATTN_V_DIM + 1]
        dt = ot[:, 0:TQ] - lam_ref[0:1, :] * ot[:, TQ:2 * TQ]
        yt = dt * lax.rsqrt(jnp.mean(dt * dt, axis=0, keepdims=True) + NORM_EPS)
        o_ref[:, hh * ATTN_V_DIM:(hh + 1) * ATTN_V_DIM] = (
            yt.T * sg_ref[...] * (1.0 - LAMBDA_INIT)).astype(o_ref.dtype)


def _diff_attn(qkv, bias_tiles, lam, subln_g, bsz, seq):
    nq = seq // TQ
    width = HEADS_PER_STEP * ATTN_V_DIM
    h_blocks = D_ATTN // width
    return pl.pallas_call(
        functools.partial(_diff_attn_kernel, seq=seq),
        grid=(bsz, ATTN_HEADS // HEADS_PER_STEP, nq),
        in_specs=[pl.BlockSpec((TQ, width), lambda b, h, i: (b * nq + i, h)),
                  pl.BlockSpec((seq, width), lambda b, h, i: (b, h_blocks + h), pipeline_mode=pl.Buffered(1)),
                  pl.BlockSpec((seq, width), lambda b, h, i: (b, 2 * h_blocks + h), pipeline_mode=pl.Buffered(1)),
                  pl.BlockSpec((HEADS_PER_STEP, 2 * TQ, TQ), lambda b, h, i: (h, 0, 0)),
                  pl.BlockSpec((8, TQ), lambda b, h, i: (0, 0)),
                  pl.BlockSpec((1, ATTN_V_DIM), lambda b, h, i: (0, 0))],
        out_specs=pl.BlockSpec((TQ, width), lambda b, h, i: (b * nq + i, h)),
        out_shape=jax.ShapeDtypeStruct((bsz * seq, D_ATTN), BF16),
        scratch_shapes=[pltpu.VMEM((HEADS_PER_STEP, 2 * TQ, ATTN_V_DIM), BF16),
                        pltpu.VMEM((HEADS_PER_STEP, V_ROWS, seq), BF16),
                        pltpu.VMEM((2, HEADS_PER_STEP, TQ, 2 * TQ), F32),
                        pltpu.VMEM((HEADS_PER_STEP, 1, 2 * TQ), F32),
                        pltpu.VMEM((HEADS_PER_STEP, V_ROWS, 2 * TQ), F32)],
        compiler_params=_params("arbitrary", "arbitrary", "arbitrary"),
        name="diff_attn",
    )(qkv, qkv, qkv, bias_tiles, lam, subln_g)


P_MU_R, P_MU_K, P_MU_V, P_W0, P_A0, P_KK, P_KA, P_RK, P_LNG, P_LNB = range(10)
P_ROWS = 16


S_AT, S_RT, S_BT, S_KT, S_BREM, S_KREM, S_V = range(7)
S_BONUS, S_GATE = range(2)
M_AM, M_ARB, M_ARK, M_RT, M_V, M_BREM, M_KREM = range(7)
F_U0, F_BONUS, F_GATE = range(3)


def _rwkv_prep(g, rp_ref, kp_ref, vp_ref, lo_ref, pv_ref, w2_ref, a2_ref, g2_ref, e_ref, tri_ref, prev_ref,
               ops_ref, epi_ref, dec_ref):
    cols = slice(g * GROUP, (g + 1) * GROUP)

    def shifted(x, prev):
        row0 = lax.broadcasted_iota(jnp.int32, x.shape, 0) == 0
        return jnp.where(row0, prev, pltpu.roll(x, 1, axis=0))

    def mixed(ref, col, mu_row):
        x = ref[:, cols]
        prev = prev_ref[0:1, col * D_RWKV + g * GROUP:col * D_RWKV + (g + 1) * GROUP]
        return x + (shifted(x, prev) - x) * pv_ref[mu_row:mu_row + 1, cols]

    pv = lambda i: pv_ref[i:i + 1, cols]
    r = mixed(rp_ref, 0, P_MU_R)
    k = mixed(kp_ref, 1, P_MU_K)
    v = mixed(vp_ref, 2, P_MU_V)

    lora = lo_ref[:, :LORA_COLS] + shifted(lo_ref[:, LORA_COLS:], prev_ref[0:1, 3 * D_RWKV + LORA_COLS:])
    w_in = pv(P_W0) + _mm(jnp.tanh(lora[:, :LORA_W]), w2_ref[:, cols])
    w = -(jnp.maximum(-w_in, 0.0) + jnp.log(1.0 + jnp.exp(-jnp.abs(w_in)))) - 0.5
    wlog = -LOG2E * jnp.exp(w)
    a = 1.0 / (1.0 + jnp.exp(-(pv(P_A0) + _mm(lora[:, LORA_W:2 * LORA_W], a2_ref[:, cols]))))
    gate = _mm(1.0 / (1.0 + jnp.exp(-lora[:, 2 * LORA_W:])), g2_ref[:, cols])

    kk = k * pv(P_KK)
    k2 = k * (1.0 + (a - 1.0) * pv(P_KA))
    sums = _mm(jnp.concatenate([kk * kk, r * k2 * pv(P_RK)], axis=0), e_ref[...])
    kk = kk / jnp.maximum(jnp.sqrt(sums[:CHUNK]), 1e-12)
    b = kk * a
    terms = []
    rest = wlog
    for _ in range(PREFIX_TERMS):
        terms.append(rest.astype(BF16))
        rest = rest - terms[-1].astype(F32)
    cum = jnp.dot(tri_ref[...], jnp.concatenate(terms, axis=0), preferred_element_type=F32)

    tot = cum[CHUNK - 1:CHUNK, :]
    e_neg = jnp.exp2(-cum)
    e_rem = jnp.exp2(tot - cum)
    staged = {S_AT: jnp.exp2(cum - wlog) * (-kk), S_RT: jnp.exp2(cum) * r, S_BT: e_neg * b, S_KT: e_neg * k2,
              S_BREM: e_rem * b, S_KREM: e_rem * k2, S_V: v}
    for slot, val in staged.items():
        ops_ref[slot, :, cols] = val.astype(BF16)
    epi_ref[S_BONUS, :, cols] = sums[CHUNK:] * v
    epi_ref[S_GATE, :, cols] = gate
    dec_ref[:, cols] = jnp.broadcast_to(jnp.exp2(tot), (dec_ref.shape[0], GROUP))


def _head_sum(x, e_ref):
    return jnp.concatenate([_mm(x[:, i * GROUP:(i + 1) * GROUP], e_ref[...]) for i in range(N_GROUPS)],
                           axis=1)


def _rwkv_tables():
    rows_bd = lax.broadcasted_iota(jnp.int32, (GROUP, GROUP), 0) // HEAD_DIM
    lanes_bd = lax.broadcasted_iota(jnp.int32, (GROUP, GROUP), 1) // HEAD_DIM
    bd_mask = rows_bd == lanes_bd
    t_idx = lax.broadcasted_iota(jnp.int32, (CHUNK, GROUP), 0)
    s_idx = lax.broadcasted_iota(jnp.int32, (CHUNK, GROUP), 1) % HEAD_DIM

    def bd(x):
        return jnp.where(bd_mask, jnp.concatenate([x] * (GROUP // HEAD_DIM), axis=0), jnp.zeros((), x.dtype))

    return bd, s_idx < t_idx, s_idx <= t_idx, jnp.where(s_idx == t_idx, 1.0, 0.0).astype(F32)


def _rwkv_solve(ops_ref, epi_ref, dec_ref, mid_ref, midf_ref, mdec_ref):
    groups = range(N_GROUPS)
    cols = lambda g: slice(g * GROUP, (g + 1) * GROUP)
    grp = lambda slot, g: ops_ref[slot, :, cols(g)]
    bd, m_strict, m_incl, eye = _rwkv_tables()

    at = [grp(S_AT, g) for g in groups]
    aa = [_mm_nt(jnp.concatenate([at[g], grp(S_RT, g)], axis=0),
                 jnp.concatenate([bd(grp(S_BT, g)), bd(grp(S_KT, g))], axis=0)) for g in groups]
    a_ab = [jnp.where(m_strict, aa[g][:CHUNK, :GROUP], 0.0) for g in groups]
    a_ak = [jnp.where(m_strict, aa[g][:CHUNK, GROUP:], 0.0) for g in groups]
    for g in groups:
        mid_ref[M_ARB, :, cols(g)] = jnp.where(m_incl, aa[g][CHUNK:, :GROUP], 0.0).astype(BF16)
        mid_ref[M_ARK, :, cols(g)] = jnp.where(m_incl, aa[g][CHUNK:, GROUP:], 0.0).astype(BF16)
    yield

    minv = [eye + a_ab[g] for g in groups]
    nk = [_mm(a_ab[g], bd(a_ab[g])) for g in groups]
    yield
    for _ in range(int(math.log2(CHUNK)) - 2):
        res = [_mm(jnp.concatenate([nk[g], minv[g]], axis=0), bd(nk[g])) for g in groups]
        nk = [res[g][:CHUNK] for g in groups]
        minv = [minv[g] + res[g][CHUNK:] for g in groups]
        yield
    minv = [minv[g] + _mm(minv[g], bd(nk[g])) for g in groups]
    yield

    x1 = [_mm(a_ak[g], bd(grp(S_V, g))) for g in groups]
    yield
    ma = [_mm(minv[g], jnp.concatenate([bd(at[g]), bd(x1[g].astype(BF16))], axis=1)) for g in groups]
    for g in groups:
        mid_ref[M_AM, :, cols(g)] = ma[g][:, :GROUP].astype(BF16)
        midf_ref[F_U0, :, cols(g)] = ma[g][:, GROUP:]
    for src, dst in ((S_RT, M_RT), (S_V, M_V), (S_BREM, M_BREM), (S_KREM, M_KREM)):
        mid_ref[dst] = ops_ref[src]
    midf_ref[F_BONUS] = epi_ref[S_BONUS]
    midf_ref[F_GATE] = epi_ref[S_GATE]
    mdec_ref[...] = dec_ref[...]
    yield


def _rwkv_advance(mid_ref, midf_ref, mdec_ref, pv_ref, e_ref, st_ref, o_ref):
    groups = range(N_GROUPS)
    heads = GROUP // HEAD_DIM
    cols = lambda g: slice(g * GROUP, (g + 1) * GROUP)
    grp = lambda slot, g: mid_ref[slot, :, cols(g)]
    bd, _, _, _ = _rwkv_tables()

    st = [st_ref[g] for g in groups]
    c1 = [_mm_nt(jnp.concatenate([grp(M_AM, g), grp(M_RT, g)], axis=0), bd(st[g])) for g in groups]
    yield
    u = [c1[g][:CHUNK] + midf_ref[F_U0, :, cols(g)] for g in groups]
    uv = [jnp.concatenate([bd(u[g].astype(BF16)), bd(grp(M_V, g))], axis=0) for g in groups]
    y = [c1[g][CHUNK:] + _mm(jnp.concatenate([grp(M_ARB, g), grp(M_ARK, g)], axis=1), uv[g]) for g in groups]

    pair_rows = lax.broadcasted_iota(jnp.int32, (2 * heads * CHUNK, GROUP), 0) // (2 * CHUNK)
    pair_mask = pair_rows == lax.broadcasted_iota(jnp.int32, (2 * heads * CHUNK, GROUP), 1) // HEAD_DIM

    def state_delta(g):
        uv_t = jnp.concatenate([u[g], grp(M_V, g).astype(F32)], axis=0).T
        lhs = jnp.concatenate([uv_t[i * HEAD_DIM:(i + 1) * HEAD_DIM, :] for i in range(heads)], axis=1)
        bk = jnp.concatenate([grp(M_BREM, g), grp(M_KREM, g)], axis=0)
        rhs = jnp.where(pair_mask, jnp.concatenate([bk] * heads, axis=0), jnp.zeros((), BF16))
        return _mm(lhs, rhs)

    upd = [state_delta(g) for g in groups]
    yield
    for g in groups:
        st_ref[g] = st[g] * mdec_ref[0:1, cols(g)] + upd[g]
    y = jnp.concatenate(y, axis=1)
    pv = lambda i: pv_ref[i:i + 1, :]
    mu = _head_sum(y, e_ref) * (1.0 / HEAD_DIM)
    yield
    d = y - mu
    var = _head_sum(d * d, e_ref) * (1.0 / HEAD_DIM)
    yield
    yn = d * lax.rsqrt(var + RWKV_GN_EPS) * pv(P_LNG) + pv(P_LNB)
    o_ref[...] = ((yn + midf_ref[F_BONUS]) * midf_ref[F_GATE]).astype(o_ref.dtype)
    yield


def _interleave(*stages):
    live = list(stages)
    while live:
        for s in list(live):
            try:
                next(s)
            except StopIteration:
                live.remove(s)


def _rwkv_kernel(rp_ref, kp_ref, vp_ref, lo_ref, pv_ref, w2_ref, a2_ref, g2_ref, e_ref, tri_ref,
                 o_ref, st_ref, prev_ref, ops_ref, epi_ref, dec_ref, mid_ref, midf_ref, mdec_ref):
    c = pl.program_id(1)

    @pl.when(c == 0)
    def _():
        for ref in (st_ref, prev_ref, epi_ref, dec_ref, midf_ref, mdec_ref):
            ref[...] = jnp.zeros(ref.shape, F32)
        ops_ref[...] = jnp.zeros(ops_ref.shape, BF16)
        mid_ref[...] = jnp.zeros(mid_ref.shape, BF16)

    def prep_stage(s, slot):
        for g in range(N_GROUPS):
            _rwkv_prep(g, rp_ref.at[s], kp_ref.at[s], vp_ref.at[s], lo_ref.at[s], pv_ref, w2_ref, a2_ref,
                       g2_ref, e_ref, tri_ref, prev_ref.at[s], ops_ref.at[s, slot], epi_ref.at[s, slot],
                       dec_ref.at[s, slot])
            yield

    def step(new, old):
        seqs = range(SEQS_PER_STEP)
        _interleave(*[prep_stage(s, new) for s in seqs],
                    *[_rwkv_solve(ops_ref.at[s, old], epi_ref.at[s, old], dec_ref.at[s, old],
                                  mid_ref.at[s, old], midf_ref.at[s, old], mdec_ref.at[s, old]) for s in seqs],
                    *[_rwkv_advance(mid_ref.at[s, new], midf_ref.at[s, new], mdec_ref.at[s, new], pv_ref,
                                    e_ref, st_ref.at[s], o_ref.at[s]) for s in seqs])
        for s in seqs:
            for col, ref in enumerate((rp_ref, kp_ref, vp_ref, lo_ref)):
                prev_ref[s, 0:1, col * D_RWKV:(col + 1) * D_RWKV] = ref[s, CHUNK - 1:CHUNK, :]

    @pl.when(c % 2 == 0)
    def _():
        step(0, 1)

    @pl.when(c % 2 == 1)
    def _():
        step(1, 0)


def _rwkv(pr, pvec, w2p, a2p, g2p, e256, tri, bsz, seq):
    nc = seq // CHUNK
    n_seq = SEQS_PER_STEP
    pr3 = pr.reshape(bsz, seq, pr.shape[-1])
    col = lambda cb: pl.BlockSpec((n_seq, CHUNK, D_RWKV), lambda b, c: (b, jnp.minimum(c, nc - 1), cb))
    full = lambda shape: pl.BlockSpec(shape, lambda b, c: (0, 0))
    out = pl.pallas_call(
        _rwkv_kernel,
        grid=(bsz // n_seq, nc + 2),
        in_specs=[col(0), col(1), col(2), col(3),
                  full((P_ROWS, D_RWKV)), full((LORA_W, D_RWKV)), full((LORA_W, D_RWKV)),
                  full((LORA_G, D_RWKV)), full((GROUP, GROUP)), full((CHUNK, PREFIX_TERMS * CHUNK))],
        out_specs=pl.BlockSpec((n_seq, CHUNK, D_RWKV), lambda b, c: (b, jnp.maximum(c - 2, 0), 0)),
        out_shape=jax.ShapeDtypeStruct((bsz, seq, D_RWKV), BF16),
        scratch_shapes=[pltpu.VMEM((n_seq, N_GROUPS, HEAD_DIM, GROUP), F32),
                        pltpu.VMEM((n_seq, 8, 4 * D_RWKV), F32),
                        pltpu.VMEM((n_seq, 2, 7, CHUNK, D_RWKV), BF16),
                        pltpu.VMEM((n_seq, 2, 2, CHUNK, D_RWKV), F32),
                        pltpu.VMEM((n_seq, 2, 8, D_RWKV), F32),
                        pltpu.VMEM((n_seq, 2, 7, CHUNK, D_RWKV), BF16),
                        pltpu.VMEM((n_seq, 2, 3, CHUNK, D_RWKV), F32),
                        pltpu.VMEM((n_seq, 2, 8, D_RWKV), F32)],
        compiler_params=_params("arbitrary", "arbitrary"),
        name="rwkv7",
    )(pr3, pr3, pr3, pr3, pvec, w2p, a2p, g2p, e256, tri)
    return out.reshape(bsz * seq, D_RWKV)


def _out_proj_kernel(oa_ref, or_ref, w_ref, x_ref, gate_ref, g_ref, sc_ref, sh_ref, x1_ref, h2_ref):
    acc = jnp.dot(oa_ref[...], w_ref[0:D_ATTN, :], preferred_element_type=F32)
    acc = acc + jnp.dot(or_ref[...], w_ref[D_ATTN:, :], preferred_element_type=F32)
    x1 = x_ref[...] + gate_ref[...] * acc
    x1_ref[...] = x1
    h2_ref[...] = _modulated_norm(x1, g_ref[...], sc_ref[...], sh_ref[...]).astype(BF16)


def _out_proj(o_attn, o_rwkv, w_bf, x2, mod3, norm_g, seq):
    n = x2.shape[0]
    tiles_per_batch = seq // TM_OUT
    mod = lambda col: pl.BlockSpec((None, 1, D_MODEL), lambda i: (i // tiles_per_batch, 0, col))
    return pl.pallas_call(
        _out_proj_kernel,
        grid=(n // TM_OUT,),
        in_specs=[pl.BlockSpec((TM_OUT, D_ATTN), lambda i: (i, 0)),
                  pl.BlockSpec((TM_OUT, D_RWKV), lambda i: (i, 0)),
                  pl.BlockSpec((D_MODEL, D_MODEL), lambda i: (0, 0), pipeline_mode=pl.Buffered(1)),
                  pl.BlockSpec((TM_OUT, D_MODEL), lambda i: (i, 0)),
                  mod(2),
                  pl.BlockSpec((1, D_MODEL), lambda i: (0, 0)),
                  mod(4), mod(3)],
        out_specs=[pl.BlockSpec((TM_OUT, D_MODEL), lambda i: (i, 0)),
                   pl.BlockSpec((TM_OUT, D_MODEL), lambda i: (i, 0))],
        out_shape=[jax.ShapeDtypeStruct((n, D_MODEL), F32),
                   jax.ShapeDtypeStruct((n, D_MODEL), BF16)],
        compiler_params=_params("arbitrary"),
        name="out_proj",
    )(o_attn, o_rwkv, w_bf, x2, mod3, norm_g, mod3, mod3)


def _ffn_up_kernel(h_ref, wg_ref, wv_ref, cwg_ref, cwv_ref, cbg_ref, cbv_ref, o_ref,
                   wgb_ref, wvb_ref, carry_ref, *, tiles_per_batch):
    i = pl.program_id(1)

    @pl.when(i == 0)
    def _():
        wgb_ref[...] = wg_ref[...].astype(BF16)
        wvb_ref[...] = wv_ref[...].astype(BF16)
        carry_ref[...] = jnp.zeros(carry_ref.shape, F32)

    first = (i % tiles_per_batch) == 0

    def conv(up, prev, cw_ref, cb_ref):
        def taps(x, x1, x2):
            return cb_ref[...] + cw_ref[0:1, :] * x2 + cw_ref[1:2, :] * x1 + cw_ref[2:3, :] * x

        y = taps(up, pltpu.roll(up, 1, axis=0), pltpu.roll(up, 2, axis=0))
        ext = jnp.concatenate([prev, up[0:8, :]], axis=0)
        head = taps(ext, pltpu.roll(ext, 1, axis=0), pltpu.roll(ext, 2, axis=0))[8:16, :]
        return jnp.concatenate([head, y[8:, :]], axis=0)

    h = h_ref[...]
    up_g = jnp.dot(h, wgb_ref[...], preferred_element_type=F32)
    up_v = jnp.dot(h, wvb_ref[...], preferred_element_type=F32)
    gate = conv(up_g, jnp.where(first, 0.0, carry_ref[0]), cwg_ref, cbg_ref)
    val = conv(up_v, jnp.where(first, 0.0, carry_ref[1]), cwv_ref, cbv_ref)
    carry_ref[0] = up_g[TM - 8:, :]
    carry_ref[1] = up_v[TM - 8:, :]
    o_ref[...] = (gate * (1.0 / (1.0 + jnp.exp(-gate))) * val).astype(o_ref.dtype)


def _ffn_up(h2, w_up, conv_w8, conv_b, seq):
    n = h2.shape[0]
    tiles_per_batch = seq // TM
    nf = D_FF // TN
    return pl.pallas_call(
        functools.partial(_ffn_up_kernel, tiles_per_batch=tiles_per_batch),
        grid=(nf, n // TM),
        in_specs=[pl.BlockSpec((TM, D_MODEL), lambda j, i: (i, 0)),
                  pl.BlockSpec((D_MODEL, TN), lambda j, i: (0, j)),
                  pl.BlockSpec((D_MODEL, TN), lambda j, i: (0, nf + j)),
                  pl.BlockSpec((8, TN), lambda j, i: (0, j)),
                  pl.BlockSpec((8, TN), lambda j, i: (0, nf + j)),
                  pl.BlockSpec((1, TN), lambda j, i: (0, j)),
                  pl.BlockSpec((1, TN), lambda j, i: (0, nf + j))],
        out_specs=pl.BlockSpec((TM, TN), lambda j, i: (i, j)),
        out_shape=jax.ShapeDtypeStruct((n, D_FF), BF16),
        scratch_shapes=[pltpu.VMEM((D_MODEL, TN), BF16),
                        pltpu.VMEM((D_MODEL, TN), BF16),
                        pltpu.VMEM((2, 8, TN), F32)],
        compiler_params=_params("arbitrary", "arbitrary"),
        name="ffn_up",
    )(h2, w_up, w_up, conv_w8, conv_w8, conv_b, conv_b)


def _ffn_down_kernel(a_ref, w_ref, x_ref, gate_ref, o_ref, wb_ref):
    @pl.when(pl.program_id(1) == 0)
    def _():
        wb_ref[...] = w_ref[...].astype(BF16)

    acc = jnp.dot(a_ref[...], wb_ref[...], preferred_element_type=F32)
    o_ref[...] = x_ref[...] + gate_ref[...] * acc


def _ffn_down(act, w_down, x1, mod3, seq):
    n = x1.shape[0]
    tiles_per_batch = seq // TM_DOWN
    gate_col = 5 * (D_MODEL // TN)
    return pl.pallas_call(
        _ffn_down_kernel,
        grid=(D_MODEL // TN, n // TM_DOWN),
        in_specs=[pl.BlockSpec((TM_DOWN, D_FF), lambda j, i: (i, 0)),
                  pl.BlockSpec((D_FF, TN), lambda j, i: (0, j)),
                  pl.BlockSpec((TM_DOWN, TN), lambda j, i: (i, j)),
                  pl.BlockSpec((None, 1, TN), lambda j, i: (i // tiles_per_batch, 0, gate_col + j))],
        out_specs=pl.BlockSpec((TM_DOWN, TN), lambda j, i: (i, j)),
        out_shape=jax.ShapeDtypeStruct((n, D_MODEL), F32),
        scratch_shapes=[pltpu.VMEM((D_FF, TN), BF16)],
        compiler_params=_params("arbitrary", "arbitrary"),
        name="ffn_down",
    )(act, w_down, x1, mod3)


def _pad_cols(w, width):
    return jnp.pad(w, ((0, 0), (0, width - w.shape[1])))


def _pad_rows(w, height):
    return jnp.pad(w, ((0, height - w.shape[0]), (0, 0)))


def kernel(x, c, rel_bias, w_ada, b_ada, norm_mix_g, w_in, q_norm_g, k_norm_g, lambda_q1, lambda_k1,
           lambda_q2, lambda_k2, attn_subln_g, mu_rkv, mu_wag, w0, w1, w2, a0, a1, a2, g1, g2, k_k,
           k_a, r_k, ln_x_g, ln_x_b, w_out, norm_ffn_g, w_up, conv_w, conv_b, w_down):
    bsz, seq, _ = x.shape
    n = bsz * seq
    x2 = x.reshape(n, D_MODEL)

    mod = _adaln(c, w_ada[0], b_ada[0])
    mod3 = mod.reshape(bsz, 1, 6 * D_MODEL)

    idx = jnp.arange(GROUP) // HEAD_DIM
    e256 = (idx[:, None] == idx[None, :]).astype(BF16)
    tri = jnp.tile((jnp.arange(CHUNK)[:, None] >= jnp.arange(CHUNK)[None, :]).astype(BF16), (1, PREFIX_TERMS))

    gain = jnp.concatenate([jnp.tile(q_norm_g[0], D_ATTN // HEAD_DIM) * (HEAD_DIM ** -0.5 * LOG2E),
                            jnp.tile(k_norm_g[0], D_ATTN // HEAD_DIM)]).reshape(1, 2 * D_ATTN)
    mu = mu_wag[0]
    lora_w = (_pad_cols(w1[0], LORA_W), _pad_cols(a1[0], LORA_W), _pad_cols(g1[0], LORA_G))
    w_lora = jnp.concatenate([(1.0 - mu[i])[:, None] * w for i, w in enumerate(lora_w)]
                             + [mu[i][:, None] * w for i, w in enumerate(lora_w)], axis=1)
    qkv, pr = _proj(x2, norm_mix_g, mod3, w_in[0], w_lora, gain, e256, seq)

    bias_tiles, lam = _attn_tables(rel_bias, lambda_q1, lambda_k1, lambda_q2, lambda_k2)
    o_attn = _diff_attn(qkv, bias_tiles, lam, attn_subln_g, bsz, seq)

    pvec = jnp.concatenate([mu_rkv[0], w0, a0, k_k, k_a, r_k.reshape(1, D_RWKV), ln_x_g, ln_x_b,
                            jnp.zeros((P_ROWS - 10, D_RWKV), F32)], axis=0)
    o_rwkv = _rwkv(pr, pvec, _pad_rows(w2[0], LORA_W).astype(BF16), _pad_rows(a2[0], LORA_W).astype(BF16),
                   _pad_rows(g2[0], LORA_G).astype(BF16), e256, tri, bsz, seq)

    x1, h2 = _out_proj(o_attn, o_rwkv, w_out[0].astype(BF16), x2, mod3, norm_ffn_g, seq)

    act = _ffn_up(h2, w_up[0], _pad_rows(conv_w[0], 8), conv_b, seq)
    out = _ffn_down(act, w_down[0], x1, mod3, seq)
    return out.reshape(bsz, seq, D_MODEL)
```

```python
import functools
import math

import jax
import jax.numpy as jnp
from jax import lax
from jax.experimental import pallas as pl
from jax.experimental.pallas import tpu as pltpu

F32 = jnp.float32
BF16 = jnp.bfloat16

D_MODEL = 2048
D_ATTN = 1024
D_RWKV = 1024
HEAD_DIM = 64
ATTN_HEADS = 8
ATTN_V_DIM = 128
D_FF = 5632
NUM_BUCKETS = 32
MAX_EXACT = 16
MAX_DISTANCE = 128
NORM_EPS = 1e-6
RWKV_GN_EPS = 64e-5
NEG_INF = -1e30
LOG2E = math.log2(math.e)
LAMBDA_INIT = 0.8 - 0.6 * math.exp(0.0)

LORA_W = 128
LORA_G = 256
LORA_COLS = 2 * LORA_W + LORA_G

TM = 1024
TN = 512
TM_DOWN = 512
TM_OUT = 512
TQ = 256
HEADS_PER_STEP = 8
V_ROWS = ATTN_V_DIM + 16
CHUNK = 64
GROUP = 256
N_GROUPS = D_RWKV // GROUP
PREFIX_TERMS = 3
SEQS_PER_STEP = 2

VMEM_LIMIT = 56 * 1024 * 1024


def _mm(a, b):
    return jnp.dot(a.astype(BF16), b.astype(BF16), preferred_element_type=F32)


def _mm_nt(a, b):
    return lax.dot_general(a.astype(BF16), b.astype(BF16), (((1,), (1,)), ((), ())),
                           preferred_element_type=F32)


def _mm_tn(a, b):
    return lax.dot_general(a.astype(BF16), b.astype(BF16), (((0,), (0,)), ((), ())),
                           preferred_element_type=F32)


def _params(*sem):
    return pltpu.CompilerParams(dimension_semantics=sem, vmem_limit_bytes=VMEM_LIMIT)


def _adaln_kernel(c_ref, w_ref, b_ref, o_ref):
    c = c_ref[...]
    c_act = c * (1.0 / (1.0 + jnp.exp(-c)))
    w = w_ref[...]
    w_hi = w.astype(BF16)
    w_lo = (w - w_hi.astype(F32)).astype(BF16)
    c_hi = c_act.astype(BF16)
    c_lo = (c_act - c_hi.astype(F32)).astype(BF16)
    rows = c.shape[0]
    t = jnp.dot(jnp.concatenate([c_hi, c_lo], axis=0), w_hi, preferred_element_type=F32)
    acc = t[:rows] + t[rows:] + jnp.dot(c_hi, w_lo, preferred_element_type=F32)
    o_ref[...] = acc + b_ref[...]


def _adaln(c, w_ada, b_ada):
    bsz = c.shape[0]
    n_out = w_ada.shape[1]
    tn = 1024
    return pl.pallas_call(
        _adaln_kernel,
        grid=(n_out // tn,),
        in_specs=[pl.BlockSpec((bsz, D_MODEL), lambda j: (0, 0)),
                  pl.BlockSpec((D_MODEL, tn), lambda j: (0, j)),
                  pl.BlockSpec((1, tn), lambda j: (0, j))],
        out_specs=pl.BlockSpec((bsz, tn), lambda j: (0, j)),
        out_shape=jax.ShapeDtypeStruct((bsz, n_out), F32),
        compiler_params=_params("arbitrary"),
        name="adaln",
    )(c, w_ada, b_ada.reshape(1, n_out))


def _modulated_norm(x, g, scale, shift):
    y = x * lax.rsqrt(jnp.mean(x * x, axis=-1, keepdims=True) + NORM_EPS)
    return (y * g) * (1.0 + scale) + shift


QK_TILES = 2 * D_ATTN // TN
ATTN_TILES = 3 * D_ATTN // TN
W_IN_TILES = (3 * D_ATTN + 3 * D_RWKV) // TN
LORA_TILES = 2 * LORA_COLS // TN


def _proj_kernel(x_ref, g_ref, sc_ref, sh_ref, w_ref, wl_ref, gain_ref, e_ref, qkv_ref, pr_ref, h_ref):
    j = pl.program_id(1)

    @pl.when(j == 0)
    def _():
        h_ref[...] = _modulated_norm(x_ref[...], g_ref[...], sc_ref[...], sh_ref[...]).astype(BF16)

    def project(weights_ref):
        return jnp.dot(h_ref[...], weights_ref[...].astype(BF16), preferred_element_type=F32)

    @pl.when(j < QK_TILES)
    def _():
        acc = project(w_ref)
        half = TN // 2
        for s in range(2):
            a = acc[:, s * half:(s + 1) * half]
            ss = _mm(a * a, e_ref[...])
            y = a * lax.rsqrt(ss * (1.0 / HEAD_DIM) + NORM_EPS) * gain_ref[:, s * half:(s + 1) * half]
            qkv_ref[:, s * half:(s + 1) * half] = y.astype(qkv_ref.dtype)

    @pl.when(jnp.logical_and(j >= QK_TILES, j < ATTN_TILES))
    def _():
        qkv_ref[...] = project(w_ref).astype(qkv_ref.dtype)

    @pl.when(jnp.logical_and(j >= ATTN_TILES, j < W_IN_TILES))
    def _():
        pr_ref[...] = project(w_ref)

    @pl.when(j >= W_IN_TILES)
    def _():
        pr_ref[...] = project(wl_ref)


def _proj(x2, norm_g, mod3, w_in, w_lora, gain, e256, seq):
    n = x2.shape[0]
    tiles_per_batch = seq // TM
    batch = lambda i, j: i // tiles_per_batch
    return pl.pallas_call(
        _proj_kernel,
        grid=(n // TM, W_IN_TILES + LORA_TILES),
        in_specs=[pl.BlockSpec((TM, D_MODEL), lambda i, j: (i, 0)),
                  pl.BlockSpec((1, D_MODEL), lambda i, j: (0, 0)),
                  pl.BlockSpec((None, 1, D_MODEL), lambda i, j: (batch(i, j), 0, 1)),
                  pl.BlockSpec((None, 1, D_MODEL), lambda i, j: (batch(i, j), 0, 0)),
                  pl.BlockSpec((D_MODEL, TN), lambda i, j: (0, jnp.minimum(j, W_IN_TILES - 1))),
                  pl.BlockSpec((D_MODEL, TN), lambda i, j: (0, jnp.maximum(j - W_IN_TILES, 0))),
                  pl.BlockSpec((1, TN), lambda i, j: (0, jnp.minimum(j, QK_TILES - 1))),
                  pl.BlockSpec((TN // 2, TN // 2), lambda i, j: (0, 0))],
        out_specs=[pl.BlockSpec((TM, TN), lambda i, j: (i, jnp.minimum(j, ATTN_TILES - 1))),
                   pl.BlockSpec((TM, TN), lambda i, j: (i, jnp.maximum(j - ATTN_TILES, 0)))],
        out_shape=[jax.ShapeDtypeStruct((n, 3 * D_ATTN), BF16),
                   jax.ShapeDtypeStruct((n, 3 * D_RWKV + 2 * LORA_COLS), F32)],
        scratch_shapes=[pltpu.VMEM((TM, D_MODEL), BF16)],
        compiler_params=_params("arbitrary", "arbitrary"),
        name="proj",
    )(x2, norm_g, mod3, mod3, w_in, w_lora, gain, e256)


def _attn_tables_kernel(rb_ref, lq1_ref, lk1_ref, lq2_ref, lk2_ref, bias_ref, lam_ref):
    h = pl.program_id(0)
    rows = lax.broadcasted_iota(jnp.int32, (2 * TQ, TQ), 0)
    cols = lax.broadcasted_iota(jnp.int32, (2 * TQ, TQ), 1)
    far = rb_ref[NUM_BUCKETS - 1, h]
    dist = cols - rows + TQ
    n = jnp.maximum(dist, 0)
    nf = jnp.maximum(n, 1).astype(F32)
    large = MAX_EXACT + (jnp.log(nf / MAX_EXACT) / math.log(MAX_DISTANCE / MAX_EXACT)
                         * (NUM_BUCKETS - MAX_EXACT)).astype(jnp.int32)
    large = jnp.minimum(large, NUM_BUCKETS - 1)
    bucket = jnp.where(n < MAX_EXACT, n, large)
    bias = jnp.zeros((2 * TQ, TQ), F32)
    for b in range(NUM_BUCKETS):
        bias = jnp.where(bucket == b, rb_ref[b, h], bias)
    bias_ref[...] = jnp.where(dist >= 0, (bias - far) * LOG2E, NEG_INF)
    s1 = jnp.sum(lq1_ref[...] * lk1_ref[...], axis=1, keepdims=True)
    s2 = jnp.sum(lq2_ref[...] * lk2_ref[...], axis=1, keepdims=True)
    lam = jnp.exp(s1) - jnp.exp(s2) + LAMBDA_INIT
    lam_ref[...] = jnp.broadcast_to(lam, lam_ref.shape)


def _attn_tables(rel_bias, lq1, lk1, lq2, lk2):
    vec = pl.BlockSpec((1, HEAD_DIM), lambda h: (0, 0))
    return pl.pallas_call(
        _attn_tables_kernel,
        grid=(ATTN_HEADS,),
        in_specs=[pl.BlockSpec(memory_space=pltpu.SMEM), vec, vec, vec, vec],
        out_specs=[pl.BlockSpec((None, 2 * TQ, TQ), lambda h: (h, 0, 0)),
                   pl.BlockSpec((8, TQ), lambda h: (0, 0))],
        out_shape=[jax.ShapeDtypeStruct((ATTN_HEADS, 2 * TQ, TQ), F32),
                   jax.ShapeDtypeStruct((8, TQ), F32)],
        compiler_params=_params("arbitrary"),
        name="attn_tables",
    )(rel_bias, lq1, lk1, lq2, lk2)


def _attn_scores(qs_ref, k_ref, j):
    start = pl.multiple_of(j * TQ, TQ)
    return [lax.dot_general(k_ref[pl.ds(start, TQ), hh * ATTN_V_DIM:(hh + 1) * ATTN_V_DIM], qs_ref[hh],
                            (((1,), (1,)), ((), ())), preferred_element_type=F32)
            for hh in range(HEADS_PER_STEP)]


def _attn_update(vt_ref, m_ref, acc_ref, sts, j, biases):
    heads = range(HEADS_PER_STEP)
    if biases is not None:
        sts = [sts[hh] + jnp.concatenate([biases[hh], biases[hh]], axis=1) for hh in heads]
    m_prev = [m_ref[hh] for hh in heads]
    m_new = [jnp.maximum(m_prev[hh], jnp.max(sts[hh], axis=0, keepdims=True)) for hh in heads]
    alpha = [jnp.exp2(m_prev[hh] - m_new[hh]) for hh in heads]
    p = [jnp.exp2(sts[hh] - m_new[hh]) for hh in heads]
    start = pl.multiple_of(j * TQ, TQ)
    pv = [jnp.dot(vt_ref[hh, :, pl.ds(start, TQ)], p[hh].astype(BF16), preferred_element_type=F32)
          for hh in heads]
    for hh in heads:
        acc_ref[hh] = alpha[hh] * acc_ref[hh] + pv[hh]
        m_ref[hh] = m_new[hh]


def _diff_attn_kernel(q_ref, k_ref, v_ref, bias_ref, lam_ref, sg_ref, o_ref,
                      qs_ref, vt_ref, s_ref, m_ref, acc_ref, *, seq):
    qi = pl.program_id(2)
    heads = range(HEADS_PER_STEP)

    @pl.when(qi == 0)
    def _():
        step = 2 * TQ
        for hh in heads:
            for c in range(seq // step):
                v = v_ref[c * step:(c + 1) * step, hh * ATTN_V_DIM:(hh + 1) * ATTN_V_DIM]
                vt_ref[hh, 0:ATTN_V_DIM, c * step:(c + 1) * step] = v.astype(F32).T.astype(BF16)
            ones_row = lax.broadcasted_iota(jnp.int32, (V_ROWS - ATTN_V_DIM, seq), 0) == 0
            vt_ref[hh, ATTN_V_DIM:V_ROWS, :] = jnp.where(ones_row, 1.0, 0.0).astype(BF16)

    lane = lax.broadcasted_iota(jnp.int32, (TQ, ATTN_V_DIM), 1)
    for hh in heads:
        q = q_ref[:, hh * ATTN_V_DIM:(hh + 1) * ATTN_V_DIM]
        zero = jnp.zeros_like(q)
        qs_ref[hh, 0:TQ, :] = jnp.where(lane < HEAD_DIM, q, zero)
        qs_ref[hh, TQ:2 * TQ, :] = jnp.where(lane >= HEAD_DIM, q, zero)
    scores = functools.partial(_attn_scores, qs_ref, k_ref)
    update = functools.partial(_attn_update, vt_ref, m_ref, acc_ref)

    def put(buf, sts):
        for hh in heads:
            s_ref[buf, hh] = sts[hh]

    get = lambda buf: [s_ref[buf, hh] for hh in heads]
    bias_prev = lambda: [bias_ref[hh, 0:TQ, :] for hh in heads]
    bias_diag = lambda: [bias_ref[hh, TQ:2 * TQ, :] for hh in heads]

    n_far = jnp.maximum(qi - 1, 0)
    put(0, scores(0))
    m_ref[...] = jnp.full(m_ref.shape, NEG_INF, F32)
    acc_ref[...] = jnp.zeros(acc_ref.shape, F32)

    def far_pair(p, carry):
        j = 2 * p
        put(1, scores(j + 1))
        update(get(0), j, None)
        put(0, scores(j + 2))
        update(get(1), j + 1, None)
        return carry

    lax.fori_loop(0, n_far // 2, far_pair, 0)

    @pl.when(n_far % 2 == 1)
    def _():
        put(1, scores(qi - 1))
        update(get(0), qi - 2, None)
        put(0, scores(qi))
        update(get(1), qi - 1, bias_prev())
        update(get(0), qi, bias_diag())

    @pl.when(jnp.logical_and(qi >= 1, n_far % 2 == 0))
    def _():
        put(1, scores(qi))
        update(get(0), qi - 1, bias_prev())
        update(get(1), qi, bias_diag())

    @pl.when(qi == 0)
    def _():
        update(get(0), qi, bias_diag())

    for hh in heads:
        acc = acc_ref[hh]
        ot = acc[0:ATTN_V_DIM] / acc[ATTN_V_DIM:ATTN_V_DIM + 1]
        dt = ot[:, 0:TQ] - lam_ref[0:1, :] * ot[:, TQ:2 * TQ]
        yt = dt * lax.rsqrt(jnp.mean(dt * dt, axis=0, keepdims=True) + NORM_EPS)
        o_ref[:, hh * ATTN_V_DIM:(hh + 1) * ATTN_V_DIM] = (
            yt.T * sg_ref[...] * (1.0 - LAMBDA_INIT)).astype(o_ref.dtype)


def _diff_attn(qkv, bias_tiles, lam, subln_g, bsz, seq):
    nq = seq // TQ
    width = HEADS_PER_STEP * ATTN_V_DIM
    h_blocks = D_ATTN // width
    return pl.pallas_call(
        functools.partial(_diff_attn_kernel, seq=seq),
        grid=(bsz, ATTN_HEADS // HEADS_PER_STEP, nq),
        in_specs=[pl.BlockSpec((TQ, width), lambda b, h, i: (b * nq + i, h)),
                  pl.BlockSpec((seq, width), lambda b, h, i: (b, h_blocks + h), pipeline_mode=pl.Buffered(1)),
                  pl.BlockSpec((seq, width), lambda b, h, i: (b, 2 * h_blocks + h), pipeline_mode=pl.Buffered(1)),
                  pl.BlockSpec((HEADS_PER_STEP, 2 * TQ, TQ), lambda b, h, i: (h, 0, 0)),
                  pl.BlockSpec((8, TQ), lambda b, h, i: (0, 0)),
                  pl.BlockSpec((1, ATTN_V_DIM), lambda b, h, i: (0, 0))],
        out_specs=pl.BlockSpec((TQ, width), lambda b, h, i: (b * nq + i, h)),
        out_shape=jax.ShapeDtypeStruct((bsz * seq, D_ATTN), BF16),
        scratch_shapes=[pltpu.VMEM((HEADS_PER_STEP, 2 * TQ, ATTN_V_DIM), BF16),
                        pltpu.VMEM((HEADS_PER_STEP, V_ROWS, seq), BF16),
                        pltpu.VMEM((2, HEADS_PER_STEP, TQ, 2 * TQ), F32),
                        pltpu.VMEM((HEADS_PER_STEP, 1, 2 * TQ), F32),
                        pltpu.VMEM((HEADS_PER_STEP, V_ROWS, 2 * TQ), F32)],
        compiler_params=_params("arbitrary", "arbitrary", "arbitrary"),
        name="diff_attn",
    )(qkv, qkv, qkv, bias_tiles, lam, subln_g)


P_MU_R, P_MU_K, P_MU_V, P_W0, P_A0, P_KK, P_KA, P_RK, P_LNG, P_LNB = range(10)
P_ROWS = 16


S_AT, S_RT, S_BT, S_KT, S_BREM, S_KREM, S_V = range(7)
S_BONUS, S_GATE = range(2)
M_AM, M_ARB, M_ARK, M_RT, M_V, M_BREM, M_KREM = range(7)
F_U0, F_BONUS, F_GATE = range(3)


def _shifted(x, prev):
    row0 = lax.broadcasted_iota(jnp.int32, x.shape, 0) == 0
    return jnp.where(row0, prev, pltpu.roll(x, 1, axis=0))


def _rwkv_lora(lo_ref, prev_ref, w2_ref, a2_ref, g2_ref):
    lora = lo_ref[:, :LORA_COLS] + _shifted(lo_ref[:, LORA_COLS:], prev_ref[0:1, 3 * D_RWKV + LORA_COLS:])
    w_pre = _mm(jnp.tanh(lora[:, :LORA_W]), w2_ref[...])
    a_pre = _mm(lora[:, LORA_W:2 * LORA_W], a2_ref[...])
    gate = _mm(1.0 / (1.0 + jnp.exp(-lora[:, 2 * LORA_W:])), g2_ref[...])
    return w_pre, a_pre, gate


def _rwkv_prep(g, lora, rp_ref, kp_ref, vp_ref, pv_ref, e_ref, tri_ref, prev_ref, ops_ref, epi_ref, dec_ref):
    cols = slice(g * GROUP, (g + 1) * GROUP)
    shifted = _shifted
    w_pre, a_pre, gate = (t[:, cols] for t in lora)

    def mixed(ref, col, mu_row):
        x = ref[:, cols]
        prev = prev_ref[0:1, col * D_RWKV + g * GROUP:col * D_RWKV + (g + 1) * GROUP]
        return x + (shifted(x, prev) - x) * pv_ref[mu_row:mu_row + 1, cols]

    pv = lambda i: pv_ref[i:i + 1, cols]
    r = mixed(rp_ref, 0, P_MU_R)
    k = mixed(kp_ref, 1, P_MU_K)
    v = mixed(vp_ref, 2, P_MU_V)

    w_in = pv(P_W0) + w_pre
    w = -(jnp.maximum(-w_in, 0.0) + jnp.log(1.0 + jnp.exp(-jnp.abs(w_in)))) - 0.5
    wlog = -LOG2E * jnp.exp(w)
    a = 1.0 / (1.0 + jnp.exp(-(pv(P_A0) + a_pre)))

    kk = k * pv(P_KK)
    k2 = k * (1.0 + (a - 1.0) * pv(P_KA))
    sums = _mm(jnp.concatenate([kk * kk, r * k2 * pv(P_RK)], axis=0), e_ref[...])
    kk = kk / jnp.maximum(jnp.sqrt(sums[:CHUNK]), 1e-12)
    b = kk * a
    terms = []
    rest = wlog
    for _ in range(PREFIX_TERMS):
        terms.append(rest.astype(BF16))
        rest = rest - terms[-1].astype(F32)
    cum = jnp.dot(tri_ref[...], jnp.concatenate(terms, axis=0), preferred_element_type=F32)

    tot = cum[CHUNK - 1:CHUNK, :]
    e_neg = jnp.exp2(-cum)
    e_rem = jnp.exp2(tot - cum)
    staged = {S_AT: jnp.exp2(cum - wlog) * (-kk), S_RT: jnp.exp2(cum) * r, S_BT: e_neg * b, S_KT: e_neg * k2,
              S_BREM: e_rem * b, S_KREM: e_rem * k2, S_V: v}
    for slot, val in staged.items():
        ops_ref[slot, :, cols] = val.astype(BF16)
    epi_ref[S_BONUS, :, cols] = sums[CHUNK:] * v
    epi_ref[S_GATE, :, cols] = gate
    dec_ref[:, cols] = jnp.broadcast_to(jnp.exp2(tot), (dec_ref.shape[0], GROUP))


def _head_sum(x, e_ref):
    stacked = jnp.concatenate([x[:, i * GROUP:(i + 1) * GROUP] for i in range(N_GROUPS)], axis=0)
    sums = _mm(stacked, e_ref[...])
    return jnp.concatenate([sums[i * CHUNK:(i + 1) * CHUNK] for i in range(N_GROUPS)], axis=1)


def _rwkv_tables():
    rows_bd = lax.broadcasted_iota(jnp.int32, (GROUP, GROUP), 0) // HEAD_DIM
    lanes_bd = lax.broadcasted_iota(jnp.int32, (GROUP, GROUP), 1) // HEAD_DIM
    bd_mask = rows_bd == lanes_bd
    t_idx = lax.broadcasted_iota(jnp.int32, (CHUNK, GROUP), 0)
    s_idx = lax.broadcasted_iota(jnp.int32, (CHUNK, GROUP), 1) % HEAD_DIM

    def bd(x):
        return jnp.where(bd_mask, jnp.concatenate([x] * (GROUP // HEAD_DIM), axis=0), jnp.zeros((), x.dtype))

    return bd, s_idx < t_idx, s_idx <= t_idx, jnp.where(s_idx == t_idx, 1.0, 0.0).astype(F32)


def _rwkv_solve(ops_ref, epi_ref, dec_ref, mid_ref, midf_ref, mdec_ref):
    groups = range(N_GROUPS)
    cols = lambda g: slice(g * GROUP, (g + 1) * GROUP)
    grp = lambda slot, g: ops_ref[slot, :, cols(g)]
    bd, m_strict, m_incl, eye = _rwkv_tables()

    at = [grp(S_AT, g) for g in groups]
    aa = [_mm_nt(jnp.concatenate([at[g], grp(S_RT, g)], axis=0),
                 jnp.concatenate([bd(grp(S_BT, g)), bd(grp(S_KT, g))], axis=0)) for g in groups]
    a_ab = [jnp.where(m_strict, aa[g][:CHUNK, :GROUP], 0.0) for g in groups]
    a_ak = [jnp.where(m_strict, aa[g][:CHUNK, GROUP:], 0.0) for g in groups]
    for g in groups:
        mid_ref[M_ARB, :, cols(g)] = jnp.where(m_incl, aa[g][CHUNK:, :GROUP], 0.0).astype(BF16)
        mid_ref[M_ARK, :, cols(g)] = jnp.where(m_incl, aa[g][CHUNK:, GROUP:], 0.0).astype(BF16)
    yield

    minv = [eye + a_ab[g] for g in groups]
    nk = [_mm(a_ab[g], bd(a_ab[g])) for g in groups]
    yield
    for _ in range(int(math.log2(CHUNK)) - 2):
        res = [_mm(jnp.concatenate([nk[g], minv[g]], axis=0), bd(nk[g])) for g in groups]
        nk = [res[g][:CHUNK] for g in groups]
        minv = [minv[g] + res[g][CHUNK:] for g in groups]
        yield
    minv = [minv[g] + _mm(minv[g], bd(nk[g])) for g in groups]
    yield

    x1 = [_mm(a_ak[g], bd(grp(S_V, g))) for g in groups]
    yield
    ma = [_mm(minv[g], jnp.concatenate([bd(at[g]), bd(x1[g].astype(BF16))], axis=1)) for g in groups]
    for g in groups:
        mid_ref[M_AM, :, cols(g)] = ma[g][:, :GROUP].astype(BF16)
        midf_ref[F_U0, :, cols(g)] = ma[g][:, GROUP:]
    for src, dst in ((S_RT, M_RT), (S_V, M_V), (S_BREM, M_BREM), (S_KREM, M_KREM)):
        mid_ref[dst] = ops_ref[src]
    midf_ref[F_BONUS] = epi_ref[S_BONUS]
    midf_ref[F_GATE] = epi_ref[S_GATE]
    mdec_ref[...] = dec_ref[...]
    yield


def _rwkv_advance(mid_ref, midf_ref, mdec_ref, pv_ref, e_ref, st_ref, o_ref):
    groups = range(N_GROUPS)
    heads = GROUP // HEAD_DIM
    cols = lambda g: slice(g * GROUP, (g + 1) * GROUP)
    grp = lambda slot, g: mid_ref[slot, :, cols(g)]
    bd, _, _, _ = _rwkv_tables()

    st = [st_ref[g] for g in groups]
    c1 = [_mm_nt(jnp.concatenate([grp(M_AM, g), grp(M_RT, g)], axis=0), bd(st[g])) for g in groups]
    yield
    u = [c1[g][:CHUNK] + midf_ref[F_U0, :, cols(g)] for g in groups]
    uv = [jnp.concatenate([bd(u[g].astype(BF16)), bd(grp(M_V, g))], axis=0) for g in groups]
    y = [c1[g][CHUNK:] + _mm(jnp.concatenate([grp(M_ARB, g), grp(M_ARK, g)], axis=1), uv[g]) for g in groups]

    pair_rows = lax.broadcasted_iota(jnp.int32, (2 * heads * CHUNK, GROUP), 0) // (2 * CHUNK)
    pair_mask = pair_rows == lax.broadcasted_iota(jnp.int32, (2 * heads * CHUNK, GROUP), 1) // HEAD_DIM

    def state_delta(g):
        uv_t = jnp.concatenate([u[g], grp(M_V, g).astype(F32)], axis=0).T
        lhs = jnp.concatenate([uv_t[i * HEAD_DIM:(i + 1) * HEAD_DIM, :] for i in range(heads)], axis=1)
        bk = jnp.concatenate([grp(M_BREM, g), grp(M_KREM, g)], axis=0)
        rhs = jnp.where(pair_mask, jnp.concatenate([bk] * heads, axis=0), jnp.zeros((), BF16))
        return _mm(lhs, rhs)

    upd = [state_delta(g) for g in groups]
    yield
    for g in groups:
        st_ref[g] = st[g] * mdec_ref[0:1, cols(g)] + upd[g]
    y = jnp.concatenate(y, axis=1)
    pv = lambda i: pv_ref[i:i + 1, :]
    mu = _head_sum(y, e_ref) * (1.0 / HEAD_DIM)
    yield
    d = y - mu
    var = _head_sum(d * d, e_ref) * (1.0 / HEAD_DIM)
    yield
    yn = d * lax.rsqrt(var + RWKV_GN_EPS) * pv(P_LNG) + pv(P_LNB)
    o_ref[...] = ((yn + midf_ref[F_BONUS]) * midf_ref[F_GATE]).astype(o_ref.dtype)
    yield


def _interleave(*stages):
    live = list(stages)
    while live:
        for s in list(live):
            try:
                next(s)
            except StopIteration:
                live.remove(s)


def _rwkv_kernel(rp_ref, kp_ref, vp_ref, lo_ref, pv_ref, w2_ref, a2_ref, g2_ref, e_ref, tri_ref,
                 o_ref, st_ref, prev_ref, ops_ref, epi_ref, dec_ref, mid_ref, midf_ref, mdec_ref):
    c = pl.program_id(1)

    @pl.when(c == 0)
    def _():
        for ref in (st_ref, prev_ref, epi_ref, dec_ref, midf_ref, mdec_ref):
            ref[...] = jnp.zeros(ref.shape, F32)
        ops_ref[...] = jnp.zeros(ops_ref.shape, BF16)
        mid_ref[...] = jnp.zeros(mid_ref.shape, BF16)

    def prep_stage(s, slot):
        lora = _rwkv_lora(lo_ref.at[s], prev_ref.at[s], w2_ref, a2_ref, g2_ref)
        yield
        for g in range(N_GROUPS):
            _rwkv_prep(g, lora, rp_ref.at[s], kp_ref.at[s], vp_ref.at[s], pv_ref, e_ref, tri_ref,
                       prev_ref.at[s], ops_ref.at[s, slot], epi_ref.at[s, slot], dec_ref.at[s, slot])
            yield

    def step(new, old):
        seqs = range(SEQS_PER_STEP)
        _interleave(*[prep_stage(s, new) for s in seqs],
                    *[_rwkv_solve(ops_ref.at[s, old], epi_ref.at[s, old], dec_ref.at[s, old],
                                  mid_ref.at[s, old], midf_ref.at[s, old], mdec_ref.at[s, old]) for s in seqs],
                    *[_rwkv_advance(mid_ref.at[s, new], midf_ref.at[s, new], mdec_ref.at[s, new], pv_ref,
                                    e_ref, st_ref.at[s], o_ref.at[s]) for s in seqs])
        for s in seqs:
            for col, ref in enumerate((rp_ref, kp_ref, vp_ref, lo_ref)):
                prev_ref[s, 0:1, col * D_RWKV:(col + 1) * D_RWKV] = ref[s, CHUNK - 1:CHUNK, :]

    @pl.when(c % 2 == 0)
    def _():
        step(0, 1)

    @pl.when(c % 2 == 1)
    def _():
        step(1, 0)


def _rwkv(pr, pvec, w2p, a2p, g2p, e256, tri, bsz, seq):
    nc = seq // CHUNK
    n_seq = SEQS_PER_STEP
    pr3 = pr.reshape(bsz, seq, pr.shape[-1])
    col = lambda cb: pl.BlockSpec((n_seq, CHUNK, D_RWKV), lambda b, c: (b, jnp.minimum(c, nc - 1), cb))
    full = lambda shape: pl.BlockSpec(shape, lambda b, c: (0, 0))
    out = pl.pallas_call(
        _rwkv_kernel,
        grid=(bsz // n_seq, nc + 2),
        in_specs=[col(0), col(1), col(2), col(3),
                  full((P_ROWS, D_RWKV)), full((LORA_W, D_RWKV)), full((LORA_W, D_RWKV)),
                  full((LORA_G, D_RWKV)), full((GROUP, GROUP)), full((CHUNK, PREFIX_TERMS * CHUNK))],
        out_specs=pl.BlockSpec((n_seq, CHUNK, D_RWKV), lambda b, c: (b, jnp.maximum(c - 2, 0), 0)),
        out_shape=jax.ShapeDtypeStruct((bsz, seq, D_RWKV), BF16),
        scratch_shapes=[pltpu.VMEM((n_seq, N_GROUPS, HEAD_DIM, GROUP), F32),
                        pltpu.VMEM((n_seq, 8, 4 * D_RWKV), F32),
                        pltpu.VMEM((n_seq, 2, 7, CHUNK, D_RWKV), BF16),
                        pltpu.VMEM((n_seq, 2, 2, CHUNK, D_RWKV), F32),
                        pltpu.VMEM((n_seq, 2, 8, D_RWKV), F32),
                        pltpu.VMEM((n_seq, 2, 7, CHUNK, D_RWKV), BF16),
                        pltpu.VMEM((n_seq, 2, 3, CHUNK, D_RWKV), F32),
                        pltpu.VMEM((n_seq, 2, 8, D_RWKV), F32)],
        compiler_params=_params("arbitrary", "arbitrary"),
        name="rwkv7",
    )(pr3, pr3, pr3, pr3, pvec, w2p, a2p, g2p, e256, tri)
    return out.reshape(bsz * seq, D_RWKV)


def _out_proj_kernel(oa_ref, or_ref, w_ref, x_ref, gate_ref, g_ref, sc_ref, sh_ref, x1_ref, h2_ref):
    acc = jnp.dot(oa_ref[...], w_ref[0:D_ATTN, :], preferred_element_type=F32)
    acc = acc + jnp.dot(or_ref[...], w_ref[D_ATTN:, :], preferred_element_type=F32)
    x1 = x_ref[...] + gate_ref[...] * acc
    x1_ref[...] = x1
    h2_ref[...] = _modulated_norm(x1, g_ref[...], sc_ref[...], sh_ref[...]).astype(BF16)


def _out_proj(o_attn, o_rwkv, w_bf, x2, mod3, norm_g, seq):
    n = x2.shape[0]
    tiles_per_batch = seq // TM_OUT
    mod = lambda col: pl.BlockSpec((None, 1, D_MODEL), lambda i: (i // tiles_per_batch, 0, col))
    return pl.pallas_call(
        _out_proj_kernel,
        grid=(n // TM_OUT,),
        in_specs=[pl.BlockSpec((TM_OUT, D_ATTN), lambda i: (i, 0)),
                  pl.BlockSpec((TM_OUT, D_RWKV), lambda i: (i, 0)),
                  pl.BlockSpec((D_MODEL, D_MODEL), lambda i: (0, 0), pipeline_mode=pl.Buffered(1)),
                  pl.BlockSpec((TM_OUT, D_MODEL), lambda i: (i, 0)),
                  mod(2),
                  pl.BlockSpec((1, D_MODEL), lambda i: (0, 0)),
                  mod(4), mod(3)],
        out_specs=[pl.BlockSpec((TM_OUT, D_MODEL), lambda i: (i, 0)),
                   pl.BlockSpec((TM_OUT, D_MODEL), lambda i: (i, 0))],
        out_shape=[jax.ShapeDtypeStruct((n, D_MODEL), F32),
                   jax.ShapeDtypeStruct((n, D_MODEL), BF16)],
        compiler_params=_params("arbitrary"),
        name="out_proj",
    )(o_attn, o_rwkv, w_bf, x2, mod3, norm_g, mod3, mod3)


def _ffn_up_kernel(h_ref, wg_ref, wv_ref, cwg_ref, cwv_ref, cbg_ref, cbv_ref, o_ref,
                   wgb_ref, wvb_ref, carry_ref, *, tiles_per_batch):
    i = pl.program_id(1)

    @pl.when(i == 0)
    def _():
        wgb_ref[...] = wg_ref[...].astype(BF16)
        wvb_ref[...] = wv_ref[...].astype(BF16)
        carry_ref[...] = jnp.zeros(carry_ref.shape, F32)

    first = (i % tiles_per_batch) == 0

    def conv(up, prev, cw_ref, cb_ref):
        def taps(x, x1, x2):
            return cb_ref[...] + cw_ref[0:1, :] * x2 + cw_ref[1:2, :] * x1 + cw_ref[2:3, :] * x

        y = taps(up, pltpu.roll(up, 1, axis=0), pltpu.roll(up, 2, axis=0))
        ext = jnp.concatenate([prev, up[0:8, :]], axis=0)
        head = taps(ext, pltpu.roll(ext, 1, axis=0), pltpu.roll(ext, 2, axis=0))[8:16, :]
        return jnp.concatenate([head, y[8:, :]], axis=0)

    h = h_ref[...]
    up_g = jnp.dot(h, wgb_ref[...], preferred_element_type=F32)
    up_v = jnp.dot(h, wvb_ref[...], preferred_element_type=F32)
    gate = conv(up_g, jnp.where(first, 0.0, carry_ref[0]), cwg_ref, cbg_ref)
    val = conv(up_v, jnp.where(first, 0.0, carry_ref[1]), cwv_ref, cbv_ref)
    carry_ref[0] = up_g[TM - 8:, :]
    carry_ref[1] = up_v[TM - 8:, :]
    o_ref[...] = (gate * (1.0 / (1.0 + jnp.exp(-gate))) * val).astype(o_ref.dtype)


def _ffn_up(h2, w_up, conv_w8, conv_b, seq):
    n = h2.shape[0]
    tiles_per_batch = seq // TM
    nf = D_FF // TN
    return pl.pallas_call(
        functools.partial(_ffn_up_kernel, tiles_per_batch=tiles_per_batch),
        grid=(nf, n // TM),
        in_specs=[pl.BlockSpec((TM, D_MODEL), lambda j, i: (i, 0)),
                  pl.BlockSpec((D_MODEL, TN), lambda j, i: (0, j)),
                  pl.BlockSpec((D_MODEL, TN), lambda j, i: (0, nf + j)),
                  pl.BlockSpec((8, TN), lambda j, i: (0, j)),
                  pl.BlockSpec((8, TN), lambda j, i: (0, nf + j)),
                  pl.BlockSpec((1, TN), lambda j, i: (0, j)),
                  pl.BlockSpec((1, TN), lambda j, i: (0, nf + j))],
        out_specs=pl.BlockSpec((TM, TN), lambda j, i: (i, j)),
        out_shape=jax.ShapeDtypeStruct((n, D_FF), BF16),
        scratch_shapes=[pltpu.VMEM((D_MODEL, TN), BF16),
                        pltpu.VMEM((D_MODEL, TN), BF16),
                        pltpu.VMEM((2, 8, TN), F32)],
        compiler_params=_params("arbitrary", "arbitrary"),
        name="ffn_up",
    )(h2, w_up, w_up, conv_w8, conv_w8, conv_b, conv_b)


def _ffn_down_kernel(a_ref, w_ref, x_ref, gate_ref, o_ref, wb_ref):
    @pl.when(pl.program_id(1) == 0)
    def _():
        wb_ref[...] = w_ref[...].astype(BF16)

    acc = jnp.dot(a_ref[...], wb_ref[...], preferred_element_type=F32)
    o_ref[...] = x_ref[...] + gate_ref[...] * acc


def _ffn_down(act, w_down, x1, mod3, seq):
    n = x1.shape[0]
    tiles_per_batch = seq // TM_DOWN
    gate_col = 5 * (D_MODEL // TN)
    return pl.pallas_call(
        _ffn_down_kernel,
        grid=(D_MODEL // TN, n // TM_DOWN),
        in_specs=[pl.BlockSpec((TM_DOWN, D_FF), lambda j, i: (i, 0)),
                  pl.BlockSpec((D_FF, TN), lambda j, i: (0, j)),
                  pl.BlockSpec((TM_DOWN, TN), lambda j, i: (i, j)),
                  pl.BlockSpec((None, 1, TN), lambda j, i: (i // tiles_per_batch, 0, gate_col + j))],
        out_specs=pl.BlockSpec((TM_DOWN, TN), lambda j, i: (i, j)),
        out_shape=jax.ShapeDtypeStruct((n, D_MODEL), F32),
        scratch_shapes=[pltpu.VMEM((D_FF, TN), BF16)],
        compiler_params=_params("arbitrary", "arbitrary"),
        name="ffn_down",
    )(act, w_down, x1, mod3)


def _pad_cols(w, width):
    return jnp.pad(w, ((0, 0), (0, width - w.shape[1])))


def _pad_rows(w, height):
    return jnp.pad(w, ((0, height - w.shape[0]), (0, 0)))


def kernel(x, c, rel_bias, w_ada, b_ada, norm_mix_g, w_in, q_norm_g, k_norm_g, lambda_q1, lambda_k1,
           lambda_q2, lambda_k2, attn_subln_g, mu_rkv, mu_wag, w0, w1, w2, a0, a1, a2, g1, g2, k_k,
           k_a, r_k, ln_x_g, ln_x_b, w_out, norm_ffn_g, w_up, conv_w, conv_b, w_down):
    bsz, seq, _ = x.shape
    n = bsz * seq
    x2 = x.reshape(n, D_MODEL)

    mod = _adaln(c, w_ada[0], b_ada[0])
    mod3 = mod.reshape(bsz, 1, 6 * D_MODEL)

    idx = jnp.arange(GROUP) // HEAD_DIM
    e256 = (idx[:, None] == idx[None, :]).astype(BF16)
    tri = jnp.tile((jnp.arange(CHUNK)[:, None] >= jnp.arange(CHUNK)[None, :]).astype(BF16), (1, PREFIX_TERMS))

    gain = jnp.concatenate([jnp.tile(q_norm_g[0], D_ATTN // HEAD_DIM) * (HEAD_DIM ** -0.5 * LOG2E),
                            jnp.tile(k_norm_g[0], D_ATTN // HEAD_DIM)]).reshape(1, 2 * D_ATTN)
    mu = mu_wag[0]
    lora_w = (_pad_cols(w1[0], LORA_W), _pad_cols(a1[0], LORA_W), _pad_cols(g1[0], LORA_G))
    w_lora = jnp.concatenate([(1.0 - mu[i])[:, None] * w for i, w in enumerate(lora_w)]
                             + [mu[i][:, None] * w for i, w in enumerate(lora_w)], axis=1)
    qkv, pr = _proj(x2, norm_mix_g, mod3, w_in[0], w_lora, gain, e256, seq)

    bias_tiles, lam = _attn_tables(rel_bias, lambda_q1, lambda_k1, lambda_q2, lambda_k2)
    o_attn = _diff_attn(qkv, bias_tiles, lam, attn_subln_g, bsz, seq)

    pvec = jnp.concatenate([mu_rkv[0], w0, a0, k_k, k_a, r_k.reshape(1, D_RWKV), ln_x_g, ln_x_b,
                            jnp.zeros((P_ROWS - 10, D_RWKV), F32)], axis=0)
    o_rwkv = _rwkv(pr, pvec, _pad_rows(w2[0], LORA_W).astype(BF16), _pad_rows(a2[0], LORA_W).astype(BF16),
                   _pad_rows(g2[0], LORA_G).astype(BF16), e256, tri, bsz, seq)

    x1, h2 = _out_proj(o_attn, o_rwkv, w_out[0].astype(BF16), x2, mod3, norm_ffn_g, seq)

    act = _ffn_up(h2, w_up[0], _pad_rows(conv_w[0], 8), conv_b, seq)
    out = _ffn_down(act, w_down[0], x1, mod3, seq)
    return out.reshape(bsz, seq, D_MODEL)
```

```python
import functools
import math

import jax
import jax.numpy as jnp
from jax import lax
from jax.experimental import pallas as pl
from jax.experimental.pallas import tpu as pltpu

F32 = jnp.float32
BF16 = jnp.bfloat16

D_MODEL = 2048
D_ATTN = 1024
D_RWKV = 1024
HEAD_DIM = 64
ATTN_HEADS = 8
ATTN_V_DIM = 128
D_FF = 5632
NUM_BUCKETS = 32
MAX_EXACT = 16
MAX_DISTANCE = 128
NORM_EPS = 1e-6
RWKV_GN_EPS = 64e-5
NEG_INF = -1e30
LOG2E = math.log2(math.e)
LAMBDA_INIT = 0.8 - 0.6 * math.exp(0.0)

LORA_W = 128
LORA_G = 256
LORA_COLS = 2 * LORA_W + LORA_G

TM = 1024
TN = 512
TM_DOWN = 512
TM_OUT = 512
TQ = 256
HEADS_PER_STEP = 8
V_ROWS = ATTN_V_DIM + 16
CHUNK = 64
GROUP = 256
N_GROUPS = D_RWKV // GROUP
PREFIX_TERMS = 3
SEQS_PER_STEP = 4
CHAIN_LAG = 3

VMEM_LIMIT = 56 * 1024 * 1024


def _mm(a, b):
    return jnp.dot(a.astype(BF16), b.astype(BF16), preferred_element_type=F32)


def _mm_nt(a, b):
    return lax.dot_general(a.astype(BF16), b.astype(BF16), (((1,), (1,)), ((), ())),
                           preferred_element_type=F32)


def _mm_tn(a, b):
    return lax.dot_general(a.astype(BF16), b.astype(BF16), (((0,), (0,)), ((), ())),
                           preferred_element_type=F32)


def _params(*sem):
    return pltpu.CompilerParams(dimension_semantics=sem, vmem_limit_bytes=VMEM_LIMIT)


def _adaln_kernel(c_ref, w_ref, b_ref, o_ref):
    c = c_ref[...]
    c_act = c * (1.0 / (1.0 + jnp.exp(-c)))
    w = w_ref[...]
    w_hi = w.astype(BF16)
    w_lo = (w - w_hi.astype(F32)).astype(BF16)
    c_hi = c_act.astype(BF16)
    c_lo = (c_act - c_hi.astype(F32)).astype(BF16)
    rows = c.shape[0]
    t = jnp.dot(jnp.concatenate([c_hi, c_lo], axis=0), w_hi, preferred_element_type=F32)
    acc = t[:rows] + t[rows:] + jnp.dot(c_hi, w_lo, preferred_element_type=F32)
    o_ref[...] = acc + b_ref[...]


def _adaln(c, w_ada, b_ada):
    bsz = c.shape[0]
    n_out = w_ada.shape[1]
    tn = 1024
    return pl.pallas_call(
        _adaln_kernel,
        grid=(n_out // tn,),
        in_specs=[pl.BlockSpec((bsz, D_MODEL), lambda j: (0, 0)),
                  pl.BlockSpec((D_MODEL, tn), lambda j: (0, j)),
                  pl.BlockSpec((1, tn), lambda j: (0, j))],
        out_specs=pl.BlockSpec((bsz, tn), lambda j: (0, j)),
        out_shape=jax.ShapeDtypeStruct((bsz, n_out), F32),
        compiler_params=_params("arbitrary"),
        name="adaln",
    )(c, w_ada, b_ada.reshape(1, n_out))


def _modulated_norm(x, g, scale, shift):
    y = x * lax.rsqrt(jnp.mean(x * x, axis=-1, keepdims=True) + NORM_EPS)
    return (y * g) * (1.0 + scale) + shift


QK_TILES = 2 * D_ATTN // TN
ATTN_TILES = 3 * D_ATTN // TN
W_IN_TILES = (3 * D_ATTN + 3 * D_RWKV) // TN
LORA_TILES = 2 * LORA_COLS // TN


def _proj_kernel(x_ref, g_ref, sc_ref, sh_ref, w_ref, wl_ref, gain_ref, e_ref, qkv_ref, pr_ref, h_ref):
    j = pl.program_id(1)

    @pl.when(j == 0)
    def _():
        h_ref[...] = _modulated_norm(x_ref[...], g_ref[...], sc_ref[...], sh_ref[...]).astype(BF16)

    def project(weights_ref):
        return jnp.dot(h_ref[...], weights_ref[...].astype(BF16), preferred_element_type=F32)

    @pl.when(j < QK_TILES)
    def _():
        acc = project(w_ref)
        half = TN // 2
        for s in range(2):
            a = acc[:, s * half:(s + 1) * half]
            ss = _mm(a * a, e_ref[...])
            y = a * lax.rsqrt(ss * (1.0 / HEAD_DIM) + NORM_EPS) * gain_ref[:, s * half:(s + 1) * half]
            qkv_ref[:, s * half:(s + 1) * half] = y.astype(qkv_ref.dtype)

    @pl.when(jnp.logical_and(j >= QK_TILES, j < ATTN_TILES))
    def _():
        qkv_ref[...] = project(w_ref).astype(qkv_ref.dtype)

    @pl.when(jnp.logical_and(j >= ATTN_TILES, j < W_IN_TILES))
    def _():
        pr_ref[...] = project(w_ref)

    @pl.when(j >= W_IN_TILES)
    def _():
        pr_ref[...] = project(wl_ref)


def _proj(x2, norm_g, mod3, w_in, w_lora, gain, e256, seq):
    n = x2.shape[0]
    tiles_per_batch = seq // TM
    batch = lambda i, j: i // tiles_per_batch
    return pl.pallas_call(
        _proj_kernel,
        grid=(n // TM, W_IN_TILES + LORA_TILES),
        in_specs=[pl.BlockSpec((TM, D_MODEL), lambda i, j: (i, 0)),
                  pl.BlockSpec((1, D_MODEL), lambda i, j: (0, 0)),
                  pl.BlockSpec((None, 1, D_MODEL), lambda i, j: (batch(i, j), 0, 1)),
                  pl.BlockSpec((None, 1, D_MODEL), lambda i, j: (batch(i, j), 0, 0)),
                  pl.BlockSpec((D_MODEL, TN), lambda i, j: (0, jnp.minimum(j, W_IN_TILES - 1))),
                  pl.BlockSpec((D_MODEL, TN), lambda i, j: (0, jnp.maximum(j - W_IN_TILES, 0))),
                  pl.BlockSpec((1, TN), lambda i, j: (0, jnp.minimum(j, QK_TILES - 1))),
                  pl.BlockSpec((TN // 2, TN // 2), lambda i, j: (0, 0))],
        out_specs=[pl.BlockSpec((TM, TN), lambda i, j: (i, jnp.minimum(j, ATTN_TILES - 1))),
                   pl.BlockSpec((TM, TN), lambda i, j: (i, jnp.maximum(j - ATTN_TILES, 0)))],
        out_shape=[jax.ShapeDtypeStruct((n, 3 * D_ATTN), BF16),
                   jax.ShapeDtypeStruct((n, 3 * D_RWKV + 2 * LORA_COLS), F32)],
        scratch_shapes=[pltpu.VMEM((TM, D_MODEL), BF16)],
        compiler_params=_params("arbitrary", "arbitrary"),
        name="proj",
    )(x2, norm_g, mod3, mod3, w_in, w_lora, gain, e256)


def _attn_tables_kernel(rb_ref, lq1_ref, lk1_ref, lq2_ref, lk2_ref, bias_ref, lam_ref):
    h = pl.program_id(0)
    rows = lax.broadcasted_iota(jnp.int32, (2 * TQ, TQ), 0)
    cols = lax.broadcasted_iota(jnp.int32, (2 * TQ, TQ), 1)
    far = rb_ref[NUM_BUCKETS - 1, h]
    dist = cols - rows + TQ
    n = jnp.maximum(dist, 0)
    nf = jnp.maximum(n, 1).astype(F32)
    large = MAX_EXACT + (jnp.log(nf / MAX_EXACT) / math.log(MAX_DISTANCE / MAX_EXACT)
                         * (NUM_BUCKETS - MAX_EXACT)).astype(jnp.int32)
    large = jnp.minimum(large, NUM_BUCKETS - 1)
    bucket = jnp.where(n < MAX_EXACT, n, large)
    bias = jnp.zeros((2 * TQ, TQ), F32)
    for b in range(NUM_BUCKETS):
        bias = jnp.where(bucket == b, rb_ref[b, h], bias)
    bias_ref[...] = jnp.where(dist >= 0, (bias - far) * LOG2E, NEG_INF)
    s1 = jnp.sum(lq1_ref[...] * lk1_ref[...], axis=1, keepdims=True)
    s2 = jnp.sum(lq2_ref[...] * lk2_ref[...], axis=1, keepdims=True)
    lam = jnp.exp(s1) - jnp.exp(s2) + LAMBDA_INIT
    lam_ref[...] = jnp.broadcast_to(lam, lam_ref.shape)


def _attn_tables(rel_bias, lq1, lk1, lq2, lk2):
    vec = pl.BlockSpec((1, HEAD_DIM), lambda h: (0, 0))
    return pl.pallas_call(
        _attn_tables_kernel,
        grid=(ATTN_HEADS,),
        in_specs=[pl.BlockSpec(memory_space=pltpu.SMEM), vec, vec, vec, vec],
        out_specs=[pl.BlockSpec((None, 2 * TQ, TQ), lambda h: (h, 0, 0)),
                   pl.BlockSpec((8, TQ), lambda h: (0, 0))],
        out_shape=[jax.ShapeDtypeStruct((ATTN_HEADS, 2 * TQ, TQ), F32),
                   jax.ShapeDtypeStruct((8, TQ), F32)],
        compiler_params=_params("arbitrary"),
        name="attn_tables",
    )(rel_bias, lq1, lk1, lq2, lk2)


def _attn_scores(qs_ref, k_ref, j):
    start = pl.multiple_of(j * TQ, TQ)
    return [lax.dot_general(k_ref[pl.ds(start, TQ), hh * ATTN_V_DIM:(hh + 1) * ATTN_V_DIM], qs_ref[hh],
                            (((1,), (1,)), ((), ())), preferred_element_type=F32)
            for hh in range(HEADS_PER_STEP)]


def _attn_update(vt_ref, m_ref, acc_ref, sts, j, biases):
    heads = range(HEADS_PER_STEP)
    if biases is not None:
        sts = [sts[hh] + jnp.concatenate([biases[hh], biases[hh]], axis=1) for hh in heads]
    m_prev = [m_ref[hh] for hh in heads]
    m_new = [jnp.maximum(m_prev[hh], jnp.max(sts[hh], axis=0, keepdims=True)) for hh in heads]
    alpha = [jnp.exp2(m_prev[hh] - m_new[hh]) for hh in heads]
    p = [jnp.exp2(sts[hh] - m_new[hh]) for hh in heads]
    start = pl.multiple_of(j * TQ, TQ)
    pv = [jnp.dot(vt_ref[hh, :, pl.ds(start, TQ)], p[hh].astype(BF16), preferred_element_type=F32)
          for hh in heads]
    for hh in heads:
        acc_ref[hh] = alpha[hh] * acc_ref[hh] + pv[hh]
        m_ref[hh] = m_new[hh]


def _diff_attn_kernel(q_ref, k_ref, v_ref, bias_ref, lam_ref, sg_ref, o_ref,
                      qs_ref, vt_ref, s_ref, m_ref, acc_ref, *, seq):
    qi = pl.program_id(2)
    heads = range(HEADS_PER_STEP)

    @pl.when(qi == 0)
    def _():
        step = 2 * TQ
        for hh in heads:
            for c in range(seq // step):
                v = v_ref[c * step:(c + 1) * step, hh * ATTN_V_DIM:(hh + 1) * ATTN_V_DIM]
                vt_ref[hh, 0:ATTN_V_DIM, c * step:(c + 1) * step] = v.astype(F32).T.astype(BF16)
            ones_row = lax.broadcasted_iota(jnp.int32, (V_ROWS - ATTN_V_DIM, seq), 0) == 0
            vt_ref[hh, ATTN_V_DIM:V_ROWS, :] = jnp.where(ones_row, 1.0, 0.0).astype(BF16)

    lane = lax.broadcasted_iota(jnp.int32, (TQ, ATTN_V_DIM), 1)
    for hh in heads:
        q = q_ref[:, hh * ATTN_V_DIM:(hh + 1) * ATTN_V_DIM]
        zero = jnp.zeros_like(q)
        qs_ref[hh, 0:TQ, :] = jnp.where(lane < HEAD_DIM, q, zero)
        qs_ref[hh, TQ:2 * TQ, :] = jnp.where(lane >= HEAD_DIM, q, zero)
    scores = functools.partial(_attn_scores, qs_ref, k_ref)
    update = functools.partial(_attn_update, vt_ref, m_ref, acc_ref)

    def put(buf, sts):
        for hh in heads:
            s_ref[buf, hh] = sts[hh]

    get = lambda buf: [s_ref[buf, hh] for hh in heads]
    bias_prev = lambda: [bias_ref[hh, 0:TQ, :] for hh in heads]
    bias_diag = lambda: [bias_ref[hh, TQ:2 * TQ, :] for hh in heads]

    n_far = jnp.maximum(qi - 1, 0)
    put(0, scores(0))
    m_ref[...] = jnp.full(m_ref.shape, NEG_INF, F32)
    acc_ref[...] = jnp.zeros(acc_ref.shape, F32)

    def far_pair(p, carry):
        j = 2 * p
        put(1, scores(j + 1))
        update(get(0), j, None)
        put(0, scores(j + 2))
        update(get(1), j + 1, None)
        return carry

    lax.fori_loop(0, n_far // 2, far_pair, 0)

    @pl.when(n_far % 2 == 1)
    def _():
        put(1, scores(qi - 1))
        update(get(0), qi - 2, None)
        put(0, scores(qi))
        update(get(1), qi - 1, bias_prev())
        update(get(0), qi, bias_diag())

    @pl.when(jnp.logical_and(qi >= 1, n_far % 2 == 0))
    def _():
        put(1, scores(qi))
        update(get(0), qi - 1, bias_prev())
        update(get(1), qi, bias_diag())

    @pl.when(qi == 0)
    def _():
        update(get(0), qi, bias_diag())

    for hh in heads:
        acc = acc_ref[hh]
        ot = acc[0:ATTN_V_DIM] / acc[ATTN_V_DIM:ATTN_V_DIM + 1]
        dt = ot[:, 0:TQ] - lam_ref[0:1, :] * ot[:, TQ:2 * TQ]
        yt = dt * lax.rsqrt(jnp.mean(dt * dt, axis=0, keepdims=True) + NORM_EPS)
        o_ref[:, hh * ATTN_V_DIM:(hh + 1) * ATTN_V_DIM] = (
            yt.T * sg_ref[...] * (1.0 - LAMBDA_INIT)).astype(o_ref.dtype)


def _diff_attn(qkv, bias_tiles, lam, subln_g, bsz, seq):
    nq = seq // TQ
    width = HEADS_PER_STEP * ATTN_V_DIM
    h_blocks = D_ATTN // width
    return pl.pallas_call(
        functools.partial(_diff_attn_kernel, seq=seq),
        grid=(bsz, ATTN_HEADS // HEADS_PER_STEP, nq),
        in_specs=[pl.BlockSpec((TQ, width), lambda b, h, i: (b * nq + i, h)),
                  pl.BlockSpec((seq, width), lambda b, h, i: (b, h_blocks + h), pipeline_mode=pl.Buffered(1)),
                  pl.BlockSpec((seq, width), lambda b, h, i: (b, 2 * h_blocks + h), pipeline_mode=pl.Buffered(1)),
                  pl.BlockSpec((HEADS_PER_STEP, 2 * TQ, TQ), lambda b, h, i: (h, 0, 0)),
                  pl.BlockSpec((8, TQ), lambda b, h, i: (0, 0)),
                  pl.BlockSpec((1, ATTN_V_DIM), lambda b, h, i: (0, 0))],
        out_specs=pl.BlockSpec((TQ, width), lambda b, h, i: (b * nq + i, h)),
        out_shape=jax.ShapeDtypeStruct((bsz * seq, D_ATTN), BF16),
        scratch_shapes=[pltpu.VMEM((HEADS_PER_STEP, 2 * TQ, ATTN_V_DIM), BF16),
                        pltpu.VMEM((HEADS_PER_STEP, V_ROWS, seq), BF16),
                        pltpu.VMEM((2, HEADS_PER_STEP, TQ, 2 * TQ), F32),
                        pltpu.VMEM((HEADS_PER_STEP, 1, 2 * TQ), F32),
                        pltpu.VMEM((HEADS_PER_STEP, V_ROWS, 2 * TQ), F32)],
        compiler_params=_params("arbitrary", "arbitrary", "arbitrary"),
        name="diff_attn",
    )(qkv, qkv, qkv, bias_tiles, lam, subln_g)


P_MU_R, P_MU_K, P_MU_V, P_W0, P_A0, P_KK, P_KA, P_RK, P_LNG, P_LNB = range(10)
P_ROWS = 16


S_AT, S_RT, S_BT, S_KT, S_BREM, S_KREM, S_V = range(7)
S_BONUS, S_GATE = range(2)


def _shifted(x, prev):
    row0 = lax.broadcasted_iota(jnp.int32, x.shape, 0) == 0
    return jnp.where(row0, prev, pltpu.roll(x, 1, axis=0))


def _rwkv_lora(lo_ref, prev_ref, w2_ref, a2_ref, g2_ref):
    lora = lo_ref[:, :LORA_COLS] + _shifted(lo_ref[:, LORA_COLS:], prev_ref[0:1, 3 * D_RWKV + LORA_COLS:])
    w_pre = _mm(jnp.tanh(lora[:, :LORA_W]), w2_ref[...])
    a_pre = _mm(lora[:, LORA_W:2 * LORA_W], a2_ref[...])
    gate = _mm(1.0 / (1.0 + jnp.exp(-lora[:, 2 * LORA_W:])), g2_ref[...])
    return w_pre, a_pre, gate


def _rwkv_prep(g, lora, rp_ref, kp_ref, vp_ref, pv_ref, e_ref, tri_ref, prev_ref, ops_ref, epi_ref, dec_ref):
    cols = slice(g * GROUP, (g + 1) * GROUP)
    shifted = _shifted
    w_pre, a_pre, gate = (t[:, cols] for t in lora)

    def mixed(ref, col, mu_row):
        x = ref[:, cols]
        prev = prev_ref[0:1, col * D_RWKV + g * GROUP:col * D_RWKV + (g + 1) * GROUP]
        return x + (shifted(x, prev) - x) * pv_ref[mu_row:mu_row + 1, cols]

    pv = lambda i: pv_ref[i:i + 1, cols]
    r = mixed(rp_ref, 0, P_MU_R)
    k = mixed(kp_ref, 1, P_MU_K)
    v = mixed(vp_ref, 2, P_MU_V)

    w_in = pv(P_W0) + w_pre
    w = -(jnp.maximum(-w_in, 0.0) + jnp.log(1.0 + jnp.exp(-jnp.abs(w_in)))) - 0.5
    wlog = -LOG2E * jnp.exp(w)
    a = 1.0 / (1.0 + jnp.exp(-(pv(P_A0) + a_pre)))

    kk = k * pv(P_KK)
    k2 = k * (1.0 + (a - 1.0) * pv(P_KA))
    sums = _mm(jnp.concatenate([kk * kk, r * k2 * pv(P_RK)], axis=0), e_ref[...])
    kk = kk / jnp.maximum(jnp.sqrt(sums[:CHUNK]), 1e-12)
    b = kk * a
    terms = []
    rest = wlog
    for _ in range(PREFIX_TERMS):
        terms.append(rest.astype(BF16))
        rest = rest - terms[-1].astype(F32)
    cum = jnp.dot(tri_ref[...], jnp.concatenate(terms, axis=0), preferred_element_type=F32)

    tot = cum[CHUNK - 1:CHUNK, :]
    e_neg = jnp.exp2(-cum)
    e_rem = jnp.exp2(tot - cum)
    staged = {S_AT: jnp.exp2(cum - wlog) * (-kk), S_RT: jnp.exp2(cum) * r, S_BT: e_neg * b, S_KT: e_neg * k2,
              S_BREM: e_rem * b, S_KREM: e_rem * k2, S_V: v}
    for slot, val in staged.items():
        ops_ref[slot, :, cols] = val.astype(BF16)
    epi_ref[S_BONUS, :, cols] = sums[CHUNK:] * v
    epi_ref[S_GATE, :, cols] = gate
    dec_ref[:, cols] = jnp.broadcast_to(jnp.exp2(tot), (dec_ref.shape[0], GROUP))


def _head_sum(x, e_ref):
    stacked = jnp.concatenate([x[:, i * GROUP:(i + 1) * GROUP] for i in range(N_GROUPS)], axis=0)
    sums = _mm(stacked, e_ref[...])
    return jnp.concatenate([sums[i * CHUNK:(i + 1) * CHUNK] for i in range(N_GROUPS)], axis=1)


def _rwkv_tables():
    rows_bd = lax.broadcasted_iota(jnp.int32, (GROUP, GROUP), 0) // HEAD_DIM
    lanes_bd = lax.broadcasted_iota(jnp.int32, (GROUP, GROUP), 1) // HEAD_DIM
    bd_mask = rows_bd == lanes_bd
    t_idx = lax.broadcasted_iota(jnp.int32, (CHUNK, GROUP), 0)
    s_idx = lax.broadcasted_iota(jnp.int32, (CHUNK, GROUP), 1) % HEAD_DIM

    def bd(x):
        return jnp.where(bd_mask, jnp.concatenate([x] * (GROUP // HEAD_DIM), axis=0), jnp.zeros((), x.dtype))

    return bd, s_idx < t_idx, s_idx <= t_idx, jnp.where(s_idx == t_idx, 1.0, 0.0).astype(F32)


def _rwkv_recur(ops_ref, epi_ref, dec_ref, pv_ref, e_ref, st_ref, o_ref):
    groups = range(N_GROUPS)
    heads = GROUP // HEAD_DIM
    cols = lambda g: slice(g * GROUP, (g + 1) * GROUP)
    grp = lambda slot, g: ops_ref[slot, :, cols(g)]
    bd, m_strict, m_incl, eye = _rwkv_tables()

    at = [grp(S_AT, g) for g in groups]
    rt = [grp(S_RT, g) for g in groups]
    aa = [_mm_nt(jnp.concatenate([at[g], rt[g]], axis=0),
                 jnp.concatenate([bd(grp(S_BT, g)), bd(grp(S_KT, g))], axis=0)) for g in groups]
    a_ab = [jnp.where(m_strict, aa[g][:CHUNK, :GROUP], 0.0) for g in groups]
    a_ak = [jnp.where(m_strict, aa[g][:CHUNK, GROUP:], 0.0) for g in groups]
    a_r = [jnp.where(jnp.concatenate([m_incl, m_incl], axis=1), aa[g][CHUNK:], 0.0).astype(BF16)
           for g in groups]
    yield

    minv = [eye + a_ab[g] for g in groups]
    nk = [_mm(a_ab[g], bd(a_ab[g])) for g in groups]
    yield
    for _ in range(int(math.log2(CHUNK)) - 2):
        res = [_mm(jnp.concatenate([nk[g], minv[g]], axis=0), bd(nk[g])) for g in groups]
        nk = [res[g][:CHUNK] for g in groups]
        minv = [minv[g] + res[g][CHUNK:] for g in groups]
        yield
    minv = [minv[g] + _mm(minv[g], bd(nk[g])) for g in groups]
    yield

    bd_v = [bd(grp(S_V, g)) for g in groups]
    x1 = [_mm(a_ak[g], bd_v[g]) for g in groups]
    yield
    ma = [_mm(minv[g], jnp.concatenate([bd(at[g]), bd(x1[g].astype(BF16))], axis=1)) for g in groups]
    yield

    st = [st_ref[g] for g in groups]
    c1 = [_mm_nt(jnp.concatenate([ma[g][:, :GROUP].astype(BF16), rt[g]], axis=0), bd(st[g])) for g in groups]
    yield
    u = [c1[g][:CHUNK] + ma[g][:, GROUP:] for g in groups]
    uv = [jnp.concatenate([bd(u[g].astype(BF16)), bd_v[g]], axis=0) for g in groups]
    y = [c1[g][CHUNK:] + _mm(a_r[g], uv[g]) for g in groups]

    pair_rows = lax.broadcasted_iota(jnp.int32, (2 * heads * CHUNK, GROUP), 0) // (2 * CHUNK)
    pair_mask = pair_rows == lax.broadcasted_iota(jnp.int32, (2 * heads * CHUNK, GROUP), 1) // HEAD_DIM

    def state_delta(g):
        uv_t = jnp.concatenate([u[g], grp(S_V, g).astype(F32)], axis=0).T
        lhs = jnp.concatenate([uv_t[i * HEAD_DIM:(i + 1) * HEAD_DIM, :] for i in range(heads)], axis=1)
        bk = jnp.concatenate([grp(S_BREM, g), grp(S_KREM, g)], axis=0)
        rhs = jnp.where(pair_mask, jnp.concatenate([bk] * heads, axis=0), jnp.zeros((), BF16))
        return _mm(lhs, rhs)

    upd = [state_delta(g) for g in groups]
    yield
    for g in groups:
        st_ref[g] = st[g] * dec_ref[0:1, cols(g)] + upd[g]
    y = jnp.concatenate(y, axis=1)
    pv = lambda i: pv_ref[i:i + 1, :]
    mu = _head_sum(y, e_ref) * (1.0 / HEAD_DIM)
    yield
    d = y - mu
    var = _head_sum(d * d, e_ref) * (1.0 / HEAD_DIM)
    yield
    yn = d * lax.rsqrt(var + RWKV_GN_EPS) * pv(P_LNG) + pv(P_LNB)
    o_ref[...] = ((yn + epi_ref[S_BONUS]) * epi_ref[S_GATE]).astype(o_ref.dtype)
    yield


def _interleave(*stages, lag=0):
    live = list(enumerate(stages))
    rounds = 0
    while live:
        for i, s in list(live):
            if rounds < lag * i:
                continue
            try:
                next(s)
            except StopIteration:
                live.remove((i, s))
        rounds += 1


def _rwkv_kernel(rp_ref, kp_ref, vp_ref, lo_ref, pv_ref, w2_ref, a2_ref, g2_ref, e_ref, tri_ref,
                 o_ref, st_ref, prev_ref, ops_ref, epi_ref, dec_ref):
    c = pl.program_id(1)

    @pl.when(c == 0)
    def _():
        st_ref[...] = jnp.zeros(st_ref.shape, F32)
        prev_ref[...] = jnp.zeros(prev_ref.shape, F32)

    def chain(s):
        lora = _rwkv_lora(lo_ref.at[s], prev_ref.at[s], w2_ref, a2_ref, g2_ref)
        yield
        for g in range(N_GROUPS):
            _rwkv_prep(g, lora, rp_ref.at[s], kp_ref.at[s], vp_ref.at[s], pv_ref, e_ref, tri_ref,
                       prev_ref.at[s], ops_ref.at[s], epi_ref.at[s], dec_ref.at[s])
            yield
        yield from _rwkv_recur(ops_ref.at[s], epi_ref.at[s], dec_ref.at[s], pv_ref, e_ref, st_ref.at[s],
                               o_ref.at[s])

    seqs = range(SEQS_PER_STEP)
    _interleave(*[chain(s) for s in seqs], lag=CHAIN_LAG)
    for s in seqs:
        for col, ref in enumerate((rp_ref, kp_ref, vp_ref, lo_ref)):
            prev_ref[s, 0:1, col * D_RWKV:(col + 1) * D_RWKV] = ref[s, CHUNK - 1:CHUNK, :]


def _rwkv(pr, pvec, w2p, a2p, g2p, e256, tri, bsz, seq):
    nc = seq // CHUNK
    n_seq = SEQS_PER_STEP
    pr3 = pr.reshape(bsz, seq, pr.shape[-1])
    col = lambda cb: pl.BlockSpec((n_seq, CHUNK, D_RWKV), lambda b, c: (b, c, cb))
    full = lambda shape: pl.BlockSpec(shape, lambda b, c: (0, 0))
    out = pl.pallas_call(
        _rwkv_kernel,
        grid=(bsz // n_seq, nc),
        in_specs=[col(0), col(1), col(2), col(3),
                  full((P_ROWS, D_RWKV)), full((LORA_W, D_RWKV)), full((LORA_W, D_RWKV)),
                  full((LORA_G, D_RWKV)), full((GROUP, GROUP)), full((CHUNK, PREFIX_TERMS * CHUNK))],
        out_specs=pl.BlockSpec((n_seq, CHUNK, D_RWKV), lambda b, c: (b, c, 0)),
        out_shape=jax.ShapeDtypeStruct((bsz, seq, D_RWKV), BF16),
        scratch_shapes=[pltpu.VMEM((n_seq, N_GROUPS, HEAD_DIM, GROUP), F32),
                        pltpu.VMEM((n_seq, 8, 4 * D_RWKV), F32),
                        pltpu.VMEM((n_seq, 7, CHUNK, D_RWKV), BF16),
                        pltpu.VMEM((n_seq, 2, CHUNK, D_RWKV), F32),
                        pltpu.VMEM((n_seq, 8, D_RWKV), F32)],
        compiler_params=_params("arbitrary", "arbitrary"),
        name="rwkv7",
    )(pr3, pr3, pr3, pr3, pvec, w2p, a2p, g2p, e256, tri)
    return out.reshape(bsz * seq, D_RWKV)


def _out_proj_kernel(oa_ref, or_ref, w_ref, x_ref, gate_ref, g_ref, sc_ref, sh_ref, x1_ref, h2_ref):
    acc = jnp.dot(oa_ref[...], w_ref[0:D_ATTN, :], preferred_element_type=F32)
    acc = acc + jnp.dot(or_ref[...], w_ref[D_ATTN:, :], preferred_element_type=F32)
    x1 = x_ref[...] + gate_ref[...] * acc
    x1_ref[...] = x1
    h2_ref[...] = _modulated_norm(x1, g_ref[...], sc_ref[...], sh_ref[...]).astype(BF16)


def _out_proj(o_attn, o_rwkv, w_bf, x2, mod3, norm_g, seq):
    n = x2.shape[0]
    tiles_per_batch = seq // TM_OUT
    mod = lambda col: pl.BlockSpec((None, 1, D_MODEL), lambda i: (i // tiles_per_batch, 0, col))
    return pl.pallas_call(
        _out_proj_kernel,
        grid=(n // TM_OUT,),
        in_specs=[pl.BlockSpec((TM_OUT, D_ATTN), lambda i: (i, 0)),
                  pl.BlockSpec((TM_OUT, D_RWKV), lambda i: (i, 0)),
                  pl.BlockSpec((D_MODEL, D_MODEL), lambda i: (0, 0), pipeline_mode=pl.Buffered(1)),
                  pl.BlockSpec((TM_OUT, D_MODEL), lambda i: (i, 0)),
                  mod(2),
                  pl.BlockSpec((1, D_MODEL), lambda i: (0, 0)),
                  mod(4), mod(3)],
        out_specs=[pl.BlockSpec((TM_OUT, D_MODEL), lambda i: (i, 0)),
                   pl.BlockSpec((TM_OUT, D_MODEL), lambda i: (i, 0))],
        out_shape=[jax.ShapeDtypeStruct((n, D_MODEL), F32),
                   jax.ShapeDtypeStruct((n, D_MODEL), BF16)],
        compiler_params=_params("arbitrary"),
        name="out_proj",
    )(o_attn, o_rwkv, w_bf, x2, mod3, norm_g, mod3, mod3)


def _ffn_up_kernel(h_ref, wg_ref, wv_ref, cwg_ref, cwv_ref, cbg_ref, cbv_ref, o_ref,
                   wgb_ref, wvb_ref, carry_ref, *, tiles_per_batch):
    i = pl.program_id(1)

    @pl.when(i == 0)
    def _():
        wgb_ref[...] = wg_ref[...].astype(BF16)
        wvb_ref[...] = wv_ref[...].astype(BF16)
        carry_ref[...] = jnp.zeros(carry_ref.shape, F32)

    first = (i % tiles_per_batch) == 0

    def conv(up, prev, cw_ref, cb_ref):
        def taps(x, x1, x2):
            return cb_ref[...] + cw_ref[0:1, :] * x2 + cw_ref[1:2, :] * x1 + cw_ref[2:3, :] * x

        y = taps(up, pltpu.roll(up, 1, axis=0), pltpu.roll(up, 2, axis=0))
        ext = jnp.concatenate([prev, up[0:8, :]], axis=0)
        head = taps(ext, pltpu.roll(ext, 1, axis=0), pltpu.roll(ext, 2, axis=0))[8:16, :]
        return jnp.concatenate([head, y[8:, :]], axis=0)

    h = h_ref[...]
    up_g = jnp.dot(h, wgb_ref[...], preferred_element_type=F32)
    up_v = jnp.dot(h, wvb_ref[...], preferred_element_type=F32)
    gate = conv(up_g, jnp.where(first, 0.0, carry_ref[0]), cwg_ref, cbg_ref)
    val = conv(up_v, jnp.where(first, 0.0, carry_ref[1]), cwv_ref, cbv_ref)
    carry_ref[0] = up_g[TM - 8:, :]
    carry_ref[1] = up_v[TM - 8:, :]
    o_ref[...] = (gate * (1.0 / (1.0 + jnp.exp(-gate))) * val).astype(o_ref.dtype)


def _ffn_up(h2, w_up, conv_w8, conv_b, seq):
    n = h2.shape[0]
    tiles_per_batch = seq // TM
    nf = D_FF // TN
    return pl.pallas_call(
        functools.partial(_ffn_up_kernel, tiles_per_batch=tiles_per_batch),
        grid=(nf, n // TM),
        in_specs=[pl.BlockSpec((TM, D_MODEL), lambda j, i: (i, 0)),
                  pl.BlockSpec((D_MODEL, TN), lambda j, i: (0, j)),
                  pl.BlockSpec((D_MODEL, TN), lambda j, i: (0, nf + j)),
                  pl.BlockSpec((8, TN), lambda j, i: (0, j)),
                  pl.BlockSpec((8, TN), lambda j, i: (0, nf + j)),
                  pl.BlockSpec((1, TN), lambda j, i: (0, j)),
                  pl.BlockSpec((1, TN), lambda j, i: (0, nf + j))],
        out_specs=pl.BlockSpec((TM, TN), lambda j, i: (i, j)),
        out_shape=jax.ShapeDtypeStruct((n, D_FF), BF16),
        scratch_shapes=[pltpu.VMEM((D_MODEL, TN), BF16),
                        pltpu.VMEM((D_MODEL, TN), BF16),
                        pltpu.VMEM((2, 8, TN), F32)],
        compiler_params=_params("arbitrary", "arbitrary"),
        name="ffn_up",
    )(h2, w_up, w_up, conv_w8, conv_w8, conv_b, conv_b)


def _ffn_down_kernel(a_ref, w_ref, x_ref, gate_ref, o_ref, wb_ref):
    @pl.when(pl.program_id(1) == 0)
    def _():
        wb_ref[...] = w_ref[...].astype(BF16)

    acc = jnp.dot(a_ref[...], wb_ref[...], preferred_element_type=F32)
    o_ref[...] = x_ref[...] + gate_ref[...] * acc


def _ffn_down(act, w_down, x1, mod3, seq):
    n = x1.shape[0]
    tiles_per_batch = seq // TM_DOWN
    gate_col = 5 * (D_MODEL // TN)
    return pl.pallas_call(
        _ffn_down_kernel,
        grid=(D_MODEL // TN, n // TM_DOWN),
        in_specs=[pl.BlockSpec((TM_DOWN, D_FF), lambda j, i: (i, 0)),
                  pl.BlockSpec((D_FF, TN), lambda j, i: (0, j)),
                  pl.BlockSpec((TM_DOWN, TN), lambda j, i: (i, j)),
                  pl.BlockSpec((None, 1, TN), lambda j, i: (i // tiles_per_batch, 0, gate_col + j))],
        out_specs=pl.BlockSpec((TM_DOWN, TN), lambda j, i: (i, j)),
        out_shape=jax.ShapeDtypeStruct((n, D_MODEL), F32),
        scratch_shapes=[pltpu.VMEM((D_FF, TN), BF16)],
        compiler_params=_params("arbitrary", "arbitrary"),
        name="ffn_down",
    )(act, w_down, x1, mod3)


def _pad_cols(w, width):
    return jnp.pad(w, ((0, 0), (0, width - w.shape[1])))


def _pad_rows(w, height):
    return jnp.pad(w, ((0, height - w.shape[0]), (0, 0)))


def kernel(x, c, rel_bias, w_ada, b_ada, norm_mix_g, w_in, q_norm_g, k_norm_g, lambda_q1, lambda_k1,
           lambda_q2, lambda_k2, attn_subln_g, mu_rkv, mu_wag, w0, w1, w2, a0, a1, a2, g1, g2, k_k,
           k_a, r_k, ln_x_g, ln_x_b, w_out, norm_ffn_g, w_up, conv_w, conv_b, w_down):
    bsz, seq, _ = x.shape
    n = bsz * seq
    x2 = x.reshape(n, D_MODEL)

    mod = _adaln(c, w_ada[0], b_ada[0])
    mod3 = mod.reshape(bsz, 1, 6 * D_MODEL)

    idx = jnp.arange(GROUP) // HEAD_DIM
    e256 = (idx[:, None] == idx[None, :]).astype(BF16)
    tri = jnp.tile((jnp.arange(CHUNK)[:, None] >= jnp.arange(CHUNK)[None, :]).astype(BF16), (1, PREFIX_TERMS))

    gain = jnp.concatenate([jnp.tile(q_norm_g[0], D_ATTN // HEAD_DIM) * (HEAD_DIM ** -0.5 * LOG2E),
                            jnp.tile(k_norm_g[0], D_ATTN // HEAD_DIM)]).reshape(1, 2 * D_ATTN)
    mu = mu_wag[0]
    lora_w = (_pad_cols(w1[0], LORA_W), _pad_cols(a1[0], LORA_W), _pad_cols(g1[0], LORA_G))
    w_lora = jnp.concatenate([(1.0 - mu[i])[:, None] * w for i, w in enumerate(lora_w)]
                             + [mu[i][:, None] * w for i, w in enumerate(lora_w)], axis=1)
    qkv, pr = _proj(x2, norm_mix_g, mod3, w_in[0], w_lora, gain, e256, seq)

    bias_tiles, lam = _attn_tables(rel_bias, lambda_q1, lambda_k1, lambda_q2, lambda_k2)
    o_attn = _diff_attn(qkv, bias_tiles, lam, attn_subln_g, bsz, seq)

    pvec = jnp.concatenate([mu_rkv[0], w0, a0, k_k, k_a, r_k.reshape(1, D_RWKV), ln_x_g, ln_x_b,
                            jnp.zeros((P_ROWS - 10, D_RWKV), F32)], axis=0)
    o_rwkv = _rwkv(pr, pvec, _pad_rows(w2[0], LORA_W).astype(BF16), _pad_rows(a2[0], LORA_W).astype(BF16),
                   _pad_rows(g2[0], LORA_G).astype(BF16), e256, tri, bsz, seq)

    x1, h2 = _out_proj(o_attn, o_rwkv, w_out[0].astype(BF16), x2, mod3, norm_ffn_g, seq)

    act = _ffn_up(h2, w_up[0], _pad_rows(conv_w[0], 8), conv_b, seq)
    out = _ffn_down(act, w_down[0], x1, mod3, seq)
    return out.reshape(bsz, seq, D_MODEL)
```

```python
import functools
import math

import jax
import jax.numpy as jnp
from jax import lax
from jax.experimental import pallas as pl
from jax.experimental.pallas import tpu as pltpu

F32 = jnp.float32
BF16 = jnp.bfloat16

D_MODEL = 2048
D_ATTN = 1024
D_RWKV = 1024
HEAD_DIM = 64
ATTN_HEADS = 8
ATTN_V_DIM = 128
D_FF = 5632
NUM_BUCKETS = 32
MAX_EXACT = 16
MAX_DISTANCE = 128
NORM_EPS = 1e-6
RWKV_GN_EPS = 64e-5
NEG_INF = -1e30
LOG2E = math.log2(math.e)
LAMBDA_INIT = 0.8 - 0.6 * math.exp(0.0)

LORA_W = 128
LORA_G = 256
LORA_COLS = 2 * LORA_W + LORA_G

TM = 1024
TN = 512
TM_DOWN = 512
TM_OUT = 512
TQ = 256
HEADS_PER_STEP = 8
V_ROWS = ATTN_V_DIM + 16
HEAD_LAG = 1
CHUNK = 64
GROUP = 256
N_GROUPS = D_RWKV // GROUP
PREFIX_TERMS = 3
SEQS_PER_STEP = 4
CHAIN_LAG = 3

VMEM_LIMIT = 56 * 1024 * 1024


def _mm(a, b):
    return jnp.dot(a.astype(BF16), b.astype(BF16), preferred_element_type=F32)


def _mm_nt(a, b):
    return lax.dot_general(a.astype(BF16), b.astype(BF16), (((1,), (1,)), ((), ())),
                           preferred_element_type=F32)


def _mm_tn(a, b):
    return lax.dot_general(a.astype(BF16), b.astype(BF16), (((0,), (0,)), ((), ())),
                           preferred_element_type=F32)


def _params(*sem):
    return pltpu.CompilerParams(dimension_semantics=sem, vmem_limit_bytes=VMEM_LIMIT)


def _adaln_kernel(c_ref, w_ref, b_ref, o_ref):
    c = c_ref[...]
    c_act = c * (1.0 / (1.0 + jnp.exp(-c)))
    w = w_ref[...]
    w_hi = w.astype(BF16)
    w_lo = (w - w_hi.astype(F32)).astype(BF16)
    c_hi = c_act.astype(BF16)
    c_lo = (c_act - c_hi.astype(F32)).astype(BF16)
    rows = c.shape[0]
    t = jnp.dot(jnp.concatenate([c_hi, c_lo], axis=0), w_hi, preferred_element_type=F32)
    acc = t[:rows] + t[rows:] + jnp.dot(c_hi, w_lo, preferred_element_type=F32)
    o_ref[...] = acc + b_ref[...]


def _adaln(c, w_ada, b_ada):
    bsz = c.shape[0]
    n_out = w_ada.shape[1]
    tn = 1024
    return pl.pallas_call(
        _adaln_kernel,
        grid=(n_out // tn,),
        in_specs=[pl.BlockSpec((bsz, D_MODEL), lambda j: (0, 0)),
                  pl.BlockSpec((D_MODEL, tn), lambda j: (0, j)),
                  pl.BlockSpec((1, tn), lambda j: (0, j))],
        out_specs=pl.BlockSpec((bsz, tn), lambda j: (0, j)),
        out_shape=jax.ShapeDtypeStruct((bsz, n_out), F32),
        compiler_params=_params("arbitrary"),
        name="adaln",
    )(c, w_ada, b_ada.reshape(1, n_out))


def _modulated_norm(x, g, scale, shift):
    y = x * lax.rsqrt(jnp.mean(x * x, axis=-1, keepdims=True) + NORM_EPS)
    return (y * g) * (1.0 + scale) + shift


QK_TILES = 2 * D_ATTN // TN
ATTN_TILES = 3 * D_ATTN // TN
W_IN_TILES = (3 * D_ATTN + 3 * D_RWKV) // TN
LORA_TILES = 2 * LORA_COLS // TN


def _proj_kernel(x_ref, g_ref, sc_ref, sh_ref, w_ref, gain_ref, e_ref, qkv_ref, pr_ref, h_ref):
    j = pl.program_id(1)

    @pl.when(j == 0)
    def _():
        h_ref[...] = _modulated_norm(x_ref[...], g_ref[...], sc_ref[...], sh_ref[...]).astype(BF16)

    def project():
        return jnp.dot(h_ref[...], w_ref[...], preferred_element_type=F32)

    @pl.when(j < QK_TILES)
    def _():
        acc = project()
        half = TN // 2
        for s in range(2):
            a = acc[:, s * half:(s + 1) * half]
            ss = _mm(a * a, e_ref[...])
            y = a * lax.rsqrt(ss * (1.0 / HEAD_DIM) + NORM_EPS) * gain_ref[:, s * half:(s + 1) * half]
            qkv_ref[:, s * half:(s + 1) * half] = y.astype(qkv_ref.dtype)

    @pl.when(jnp.logical_and(j >= QK_TILES, j < ATTN_TILES))
    def _():
        qkv_ref[...] = project().astype(qkv_ref.dtype)

    @pl.when(j >= ATTN_TILES)
    def _():
        pr_ref[...] = project()


def _proj(x2, norm_g, mod3, w_tiles, gain, e256, seq):
    n = x2.shape[0]
    tiles_per_batch = seq // TM
    batch = lambda i, j: i // tiles_per_batch
    return pl.pallas_call(
        _proj_kernel,
        grid=(n // TM, W_IN_TILES + LORA_TILES),
        in_specs=[pl.BlockSpec((TM, D_MODEL), lambda i, j: (i, 0)),
                  pl.BlockSpec((1, D_MODEL), lambda i, j: (0, 0)),
                  pl.BlockSpec((None, 1, D_MODEL), lambda i, j: (batch(i, j), 0, 1)),
                  pl.BlockSpec((None, 1, D_MODEL), lambda i, j: (batch(i, j), 0, 0)),
                  pl.BlockSpec((None, D_MODEL, TN), lambda i, j: (j, 0, 0)),
                  pl.BlockSpec((1, TN), lambda i, j: (0, jnp.minimum(j, QK_TILES - 1))),
                  pl.BlockSpec((TN // 2, TN // 2), lambda i, j: (0, 0))],
        out_specs=[pl.BlockSpec((TM, TN), lambda i, j: (i, jnp.minimum(j, ATTN_TILES - 1))),
                   pl.BlockSpec((TM, TN), lambda i, j: (i, jnp.maximum(j - ATTN_TILES, 0)))],
        out_shape=[jax.ShapeDtypeStruct((n, 3 * D_ATTN), BF16),
                   jax.ShapeDtypeStruct((n, 3 * D_RWKV + 2 * LORA_COLS), F32)],
        scratch_shapes=[pltpu.VMEM((TM, D_MODEL), BF16)],
        compiler_params=_params("arbitrary", "arbitrary"),
        name="proj",
    )(x2, norm_g, mod3, mod3, w_tiles, gain, e256)


def _attn_tables_kernel(rb_ref, lq1_ref, lk1_ref, lq2_ref, lk2_ref, bias_ref, lam_ref):
    h = pl.program_id(0)
    rows = lax.broadcasted_iota(jnp.int32, (2 * TQ, TQ), 0)
    cols = lax.broadcasted_iota(jnp.int32, (2 * TQ, TQ), 1)
    far = rb_ref[NUM_BUCKETS - 1, h]
    dist = cols - rows + TQ
    n = jnp.maximum(dist, 0)
    nf = jnp.maximum(n, 1).astype(F32)
    large = MAX_EXACT + (jnp.log(nf / MAX_EXACT) / math.log(MAX_DISTANCE / MAX_EXACT)
                         * (NUM_BUCKETS - MAX_EXACT)).astype(jnp.int32)
    large = jnp.minimum(large, NUM_BUCKETS - 1)
    bucket = jnp.where(n < MAX_EXACT, n, large)
    bias = jnp.zeros((2 * TQ, TQ), F32)
    for b in range(NUM_BUCKETS):
        bias = jnp.where(bucket == b, rb_ref[b, h], bias)
    bias_ref[...] = jnp.where(dist >= 0, (bias - far) * LOG2E, NEG_INF)
    s1 = jnp.sum(lq1_ref[...] * lk1_ref[...], axis=1, keepdims=True)
    s2 = jnp.sum(lq2_ref[...] * lk2_ref[...], axis=1, keepdims=True)
    lam = jnp.exp(s1) - jnp.exp(s2) + LAMBDA_INIT
    lam_ref[...] = jnp.broadcast_to(lam, lam_ref.shape)


def _attn_tables(rel_bias, lq1, lk1, lq2, lk2):
    vec = pl.BlockSpec((1, HEAD_DIM), lambda h: (0, 0))
    return pl.pallas_call(
        _attn_tables_kernel,
        grid=(ATTN_HEADS,),
        in_specs=[pl.BlockSpec(memory_space=pltpu.SMEM), vec, vec, vec, vec],
        out_specs=[pl.BlockSpec((None, 2 * TQ, TQ), lambda h: (h, 0, 0)),
                   pl.BlockSpec((8, TQ), lambda h: (0, 0))],
        out_shape=[jax.ShapeDtypeStruct((ATTN_HEADS, 2 * TQ, TQ), F32),
                   jax.ShapeDtypeStruct((8, TQ), F32)],
        compiler_params=_params("arbitrary"),
        name="attn_tables",
    )(rel_bias, lq1, lk1, lq2, lk2)


def _attn_stage(qs_ref, k_ref, vt_ref, s_ref, m_ref, acc_ref, put, j_new, get=None, j_old=None, biases=None):
    def chain(hh):
        if j_new is not None:
            start = pl.multiple_of(j_new * TQ, TQ)
            s_ref[put, hh] = lax.dot_general(k_ref[pl.ds(start, TQ), hh * ATTN_V_DIM:(hh + 1) * ATTN_V_DIM],
                                             qs_ref[hh], (((1,), (1,)), ((), ())),
                                             preferred_element_type=F32)
            yield
        if get is not None:
            st = s_ref[get, hh]
            if biases is not None:
                st = st + jnp.concatenate([biases[hh], biases[hh]], axis=1)
            m_prev = m_ref[hh]
            m_new = jnp.maximum(m_prev, jnp.max(st, axis=0, keepdims=True))
            yield
            alpha = jnp.exp2(m_prev - m_new)
            p = jnp.exp2(st - m_new).astype(BF16)
            yield
            start = pl.multiple_of(j_old * TQ, TQ)
            pv = jnp.dot(vt_ref[hh, :, pl.ds(start, TQ)], p, preferred_element_type=F32)
            yield
            acc_ref[hh] = alpha * acc_ref[hh] + pv
            m_ref[hh] = m_new
            yield

    _interleave(*[chain(hh) for hh in range(HEADS_PER_STEP)], lag=HEAD_LAG)


def _diff_attn_kernel(q_ref, k_ref, v_ref, bias_ref, lam_ref, sg_ref, o_ref,
                      qs_ref, vt_ref, s_ref, m_ref, acc_ref, *, seq):
    qi = pl.program_id(2)
    heads = range(HEADS_PER_STEP)

    @pl.when(qi == 0)
    def _():
        step = 2 * TQ
        for hh in heads:
            for c in range(seq // step):
                v = v_ref[c * step:(c + 1) * step, hh * ATTN_V_DIM:(hh + 1) * ATTN_V_DIM]
                vt_ref[hh, 0:ATTN_V_DIM, c * step:(c + 1) * step] = v.astype(F32).T.astype(BF16)
            ones_row = lax.broadcasted_iota(jnp.int32, (V_ROWS - ATTN_V_DIM, seq), 0) == 0
            vt_ref[hh, ATTN_V_DIM:V_ROWS, :] = jnp.where(ones_row, 1.0, 0.0).astype(BF16)

    lane = lax.broadcasted_iota(jnp.int32, (TQ, ATTN_V_DIM), 1)
    for hh in heads:
        q = q_ref[:, hh * ATTN_V_DIM:(hh + 1) * ATTN_V_DIM]
        zero = jnp.zeros_like(q)
        qs_ref[hh, 0:TQ, :] = jnp.where(lane < HEAD_DIM, q, zero)
        qs_ref[hh, TQ:2 * TQ, :] = jnp.where(lane >= HEAD_DIM, q, zero)
    stage = functools.partial(_attn_stage, qs_ref, k_ref, vt_ref, s_ref, m_ref, acc_ref)
    bias_prev = lambda: [bias_ref[hh, 0:TQ, :] for hh in heads]
    bias_diag = lambda: [bias_ref[hh, TQ:2 * TQ, :] for hh in heads]

    n_far = jnp.maximum(qi - 1, 0)
    stage(0, 0)
    m_ref[...] = jnp.full(m_ref.shape, NEG_INF, F32)
    acc_ref[...] = jnp.zeros(acc_ref.shape, F32)

    def far_pair(p, carry):
        j = 2 * p
        stage(1, j + 1, 0, j)
        stage(0, j + 2, 1, j + 1)
        return carry

    lax.fori_loop(0, n_far // 2, far_pair, 0)

    @pl.when(n_far % 2 == 1)
    def _():
        stage(1, qi - 1, 0, qi - 2)
        stage(0, qi, 1, qi - 1, bias_prev())
        stage(None, None, 0, qi, bias_diag())

    @pl.when(jnp.logical_and(qi >= 1, n_far % 2 == 0))
    def _():
        stage(1, qi, 0, qi - 1, bias_prev())
        stage(None, None, 1, qi, bias_diag())

    @pl.when(qi == 0)
    def _():
        stage(None, None, 0, qi, bias_diag())

    for hh in heads:
        acc = acc_ref[hh]
        ot = acc[0:ATTN_V_DIM] / acc[ATTN_V_DIM:ATTN_V_DIM + 1]
        dt = ot[:, 0:TQ] - lam_ref[0:1, :] * ot[:, TQ:2 * TQ]
        yt = dt * lax.rsqrt(jnp.mean(dt * dt, axis=0, keepdims=True) + NORM_EPS)
        o_ref[:, hh * ATTN_V_DIM:(hh + 1) * ATTN_V_DIM] = (
            yt.T * sg_ref[...] * (1.0 - LAMBDA_INIT)).astype(o_ref.dtype)


def _diff_attn(qkv, bias_tiles, lam, subln_g, bsz, seq):
    nq = seq // TQ
    width = HEADS_PER_STEP * ATTN_V_DIM
    h_blocks = D_ATTN // width
    return pl.pallas_call(
        functools.partial(_diff_attn_kernel, seq=seq),
        grid=(bsz, ATTN_HEADS // HEADS_PER_STEP, nq),
        in_specs=[pl.BlockSpec((TQ, width), lambda b, h, i: (b * nq + i, h)),
                  pl.BlockSpec((seq, width), lambda b, h, i: (b, h_blocks + h), pipeline_mode=pl.Buffered(1)),
                  pl.BlockSpec((seq, width), lambda b, h, i: (b, 2 * h_blocks + h), pipeline_mode=pl.Buffered(1)),
                  pl.BlockSpec((HEADS_PER_STEP, 2 * TQ, TQ), lambda b, h, i: (h, 0, 0)),
                  pl.BlockSpec((8, TQ), lambda b, h, i: (0, 0)),
                  pl.BlockSpec((1, ATTN_V_DIM), lambda b, h, i: (0, 0))],
        out_specs=pl.BlockSpec((TQ, width), lambda b, h, i: (b * nq + i, h)),
        out_shape=jax.ShapeDtypeStruct((bsz * seq, D_ATTN), BF16),
        scratch_shapes=[pltpu.VMEM((HEADS_PER_STEP, 2 * TQ, ATTN_V_DIM), BF16),
                        pltpu.VMEM((HEADS_PER_STEP, V_ROWS, seq), BF16),
                        pltpu.VMEM((2, HEADS_PER_STEP, TQ, 2 * TQ), F32),
                        pltpu.VMEM((HEADS_PER_STEP, 1, 2 * TQ), F32),
                        pltpu.VMEM((HEADS_PER_STEP, V_ROWS, 2 * TQ), F32)],
        compiler_params=_params("arbitrary", "arbitrary", "arbitrary"),
        name="diff_attn",
    )(qkv, qkv, qkv, bias_tiles, lam, subln_g)


P_MU_R, P_MU_K, P_MU_V, P_W0, P_A0, P_KK, P_KA, P_RK, P_LNG, P_LNB = range(10)
P_ROWS = 16


S_AT, S_RT, S_BT, S_KT, S_BREM, S_KREM, S_V = range(7)
S_BONUS, S_GATE = range(2)


def _shifted(x, prev):
    row0 = lax.broadcasted_iota(jnp.int32, x.shape, 0) == 0
    return jnp.where(row0, prev, pltpu.roll(x, 1, axis=0))


def _rwkv_lora(lo_ref, prev_ref, w2_ref, a2_ref, g2_ref):
    lora = lo_ref[:, :LORA_COLS] + _shifted(lo_ref[:, LORA_COLS:], prev_ref[0:1, 3 * D_RWKV + LORA_COLS:])
    w_pre = _mm(jnp.tanh(lora[:, :LORA_W]), w2_ref[...])
    a_pre = _mm(lora[:, LORA_W:2 * LORA_W], a2_ref[...])
    gate = _mm(1.0 / (1.0 + jnp.exp(-lora[:, 2 * LORA_W:])), g2_ref[...])
    return w_pre, a_pre, gate


def _rwkv_prep(g, lora, rp_ref, kp_ref, vp_ref, pv_ref, e_ref, tri_ref, prev_ref, ops_ref, epi_ref, dec_ref):
    cols = slice(g * GROUP, (g + 1) * GROUP)
    shifted = _shifted
    w_pre, a_pre, gate = (t[:, cols] for t in lora)

    def mixed(ref, col, mu_row):
        x = ref[:, cols]
        prev = prev_ref[0:1, col * D_RWKV + g * GROUP:col * D_RWKV + (g + 1) * GROUP]
        return x + (shifted(x, prev) - x) * pv_ref[mu_row:mu_row + 1, cols]

    pv = lambda i: pv_ref[i:i + 1, cols]
    r = mixed(rp_ref, 0, P_MU_R)
    k = mixed(kp_ref, 1, P_MU_K)
    v = mixed(vp_ref, 2, P_MU_V)

    w_in = pv(P_W0) + w_pre
    w = -(jnp.maximum(-w_in, 0.0) + jnp.log(1.0 + jnp.exp(-jnp.abs(w_in)))) - 0.5
    wlog = -LOG2E * jnp.exp(w)
    a = 1.0 / (1.0 + jnp.exp(-(pv(P_A0) + a_pre)))

    kk = k * pv(P_KK)
    k2 = k * (1.0 + (a - 1.0) * pv(P_KA))
    sums = _mm(jnp.concatenate([kk * kk, r * k2 * pv(P_RK)], axis=0), e_ref[...])
    kk = kk / jnp.maximum(jnp.sqrt(sums[:CHUNK]), 1e-12)
    b = kk * a
    terms = []
    rest = wlog
    for _ in range(PREFIX_TERMS):
        terms.append(rest.astype(BF16))
        rest = rest - terms[-1].astype(F32)
    cum = jnp.dot(tri_ref[...], jnp.concatenate(terms, axis=0), preferred_element_type=F32)

    tot = cum[CHUNK - 1:CHUNK, :]
    e_neg = jnp.exp2(-cum)
    e_rem = jnp.exp2(tot - cum)
    staged = {S_AT: jnp.exp2(cum - wlog) * (-kk), S_RT: jnp.exp2(cum) * r, S_BT: e_neg * b, S_KT: e_neg * k2,
              S_BREM: e_rem * b, S_KREM: e_rem * k2, S_V: v}
    for slot, val in staged.items():
        ops_ref[slot, :, cols] = val.astype(BF16)
    epi_ref[S_BONUS, :, cols] = sums[CHUNK:] * v
    epi_ref[S_GATE, :, cols] = gate
    dec_ref[:, cols] = jnp.broadcast_to(jnp.exp2(tot), (dec_ref.shape[0], GROUP))


def _head_sum(x, e_ref):
    stacked = jnp.concatenate([x[:, i * GROUP:(i + 1) * GROUP] for i in range(N_GROUPS)], axis=0)
    sums = _mm(stacked, e_ref[...])
    return jnp.concatenate([sums[i * CHUNK:(i + 1) * CHUNK] for i in range(N_GROUPS)], axis=1)


def _rwkv_tables():
    rows_bd = lax.broadcasted_iota(jnp.int32, (GROUP, GROUP), 0) // HEAD_DIM
    lanes_bd = lax.broadcasted_iota(jnp.int32, (GROUP, GROUP), 1) // HEAD_DIM
    bd_mask = rows_bd == lanes_bd
    t_idx = lax.broadcasted_iota(jnp.int32, (CHUNK, GROUP), 0)
    s_idx = lax.broadcasted_iota(jnp.int32, (CHUNK, GROUP), 1) % HEAD_DIM

    def bd(x):
        return jnp.where(bd_mask, jnp.concatenate([x] * (GROUP // HEAD_DIM), axis=0), jnp.zeros((), x.dtype))

    return bd, s_idx < t_idx, s_idx <= t_idx, jnp.where(s_idx == t_idx, 1.0, 0.0).astype(F32)


def _rwkv_recur(ops_ref, epi_ref, dec_ref, pv_ref, e_ref, st_ref, o_ref):
    groups = range(N_GROUPS)
    heads = GROUP // HEAD_DIM
    cols = lambda g: slice(g * GROUP, (g + 1) * GROUP)
    grp = lambda slot, g: ops_ref[slot, :, cols(g)]
    bd, m_strict, m_incl, eye = _rwkv_tables()

    at = [grp(S_AT, g) for g in groups]
    rt = [grp(S_RT, g) for g in groups]
    aa = [_mm_nt(jnp.concatenate([at[g], rt[g]], axis=0),
                 jnp.concatenate([bd(grp(S_BT, g)), bd(grp(S_KT, g))], axis=0)) for g in groups]
    a_ab = [jnp.where(m_strict, aa[g][:CHUNK, :GROUP], 0.0) for g in groups]
    a_ak = [jnp.where(m_strict, aa[g][:CHUNK, GROUP:], 0.0) for g in groups]
    a_r = [jnp.where(jnp.concatenate([m_incl, m_incl], axis=1), aa[g][CHUNK:], 0.0).astype(BF16)
           for g in groups]
    yield

    minv = [eye + a_ab[g] for g in groups]
    nk = [_mm(a_ab[g], bd(a_ab[g])) for g in groups]
    yield
    for _ in range(int(math.log2(CHUNK)) - 2):
        res = [_mm(jnp.concatenate([nk[g], minv[g]], axis=0), bd(nk[g])) for g in groups]
        nk = [res[g][:CHUNK] for g in groups]
        minv = [minv[g] + res[g][CHUNK:] for g in groups]
        yield
    minv = [minv[g] + _mm(minv[g], bd(nk[g])) for g in groups]
    yield

    bd_v = [bd(grp(S_V, g)) for g in groups]
    x1 = [_mm(a_ak[g], bd_v[g]) for g in groups]
    yield
    ma = [_mm(minv[g], jnp.concatenate([bd(at[g]), bd(x1[g].astype(BF16))], axis=1)) for g in groups]
    yield

    st = [st_ref[g] for g in groups]
    c1 = [_mm_nt(jnp.concatenate([ma[g][:, :GROUP].astype(BF16), rt[g]], axis=0), bd(st[g])) for g in groups]
    yield
    u = [c1[g][:CHUNK] + ma[g][:, GROUP:] for g in groups]
    uv = [jnp.concatenate([bd(u[g].astype(BF16)), bd_v[g]], axis=0) for g in groups]
    y = [c1[g][CHUNK:] + _mm(a_r[g], uv[g]) for g in groups]

    pair_rows = lax.broadcasted_iota(jnp.int32, (2 * heads * CHUNK, GROUP), 0) // (2 * CHUNK)
    pair_mask = pair_rows == lax.broadcasted_iota(jnp.int32, (2 * heads * CHUNK, GROUP), 1) // HEAD_DIM

    def state_delta(g):
        uv_t = jnp.concatenate([u[g], grp(S_V, g).astype(F32)], axis=0).T
        lhs = jnp.concatenate([uv_t[i * HEAD_DIM:(i + 1) * HEAD_DIM, :] for i in range(heads)], axis=1)
        bk = jnp.concatenate([grp(S_BREM, g), grp(S_KREM, g)], axis=0)
        rhs = jnp.where(pair_mask, jnp.concatenate([bk] * heads, axis=0), jnp.zeros((), BF16))
        return _mm(lhs, rhs)

    upd = [state_delta(g) for g in groups]
    yield
    for g in groups:
        st_ref[g] = st[g] * dec_ref[0:1, cols(g)] + upd[g]
    y = jnp.concatenate(y, axis=1)
    pv = lambda i: pv_ref[i:i + 1, :]
    mu = _head_sum(y, e_ref) * (1.0 / HEAD_DIM)
    yield
    d = y - mu
    var = _head_sum(d * d, e_ref) * (1.0 / HEAD_DIM)
    yield
    yn = d * lax.rsqrt(var + RWKV_GN_EPS) * pv(P_LNG) + pv(P_LNB)
    o_ref[...] = ((yn + epi_ref[S_BONUS]) * epi_ref[S_GATE]).astype(o_ref.dtype)
    yield


def _interleave(*stages, lag=0):
    live = list(enumerate(stages))
    rounds = 0
    while live:
        for i, s in list(live):
            if rounds < lag * i:
                continue
            try:
                next(s)
            except StopIteration:
                live.remove((i, s))
        rounds += 1


def _rwkv_kernel(rp_ref, kp_ref, vp_ref, lo_ref, pv_ref, w2_ref, a2_ref, g2_ref, e_ref, tri_ref,
                 o_ref, st_ref, prev_ref, ops_ref, epi_ref, dec_ref):
    c = pl.program_id(1)

    @pl.when(c == 0)
    def _():
        st_ref[...] = jnp.zeros(st_ref.shape, F32)
        prev_ref[...] = jnp.zeros(prev_ref.shape, F32)

    def chain(s):
        lora = _rwkv_lora(lo_ref.at[s], prev_ref.at[s], w2_ref, a2_ref, g2_ref)
        yield
        for g in range(N_GROUPS):
            _rwkv_prep(g, lora, rp_ref.at[s], kp_ref.at[s], vp_ref.at[s], pv_ref, e_ref, tri_ref,
                       prev_ref.at[s], ops_ref.at[s], epi_ref.at[s], dec_ref.at[s])
            yield
        yield from _rwkv_recur(ops_ref.at[s], epi_ref.at[s], dec_ref.at[s], pv_ref, e_ref, st_ref.at[s],
                               o_ref.at[s])

    seqs = range(SEQS_PER_STEP)
    _interleave(*[chain(s) for s in seqs], lag=CHAIN_LAG)
    for s in seqs:
        for col, ref in enumerate((rp_ref, kp_ref, vp_ref, lo_ref)):
            prev_ref[s, 0:1, col * D_RWKV:(col + 1) * D_RWKV] = ref[s, CHUNK - 1:CHUNK, :]


def _rwkv(pr, pvec, w2p, a2p, g2p, e256, tri, bsz, seq):
    nc = seq // CHUNK
    n_seq = SEQS_PER_STEP
    pr3 = pr.reshape(bsz, seq, pr.shape[-1])
    col = lambda cb: pl.BlockSpec((n_seq, CHUNK, D_RWKV), lambda b, c: (b, c, cb))
    full = lambda shape: pl.BlockSpec(shape, lambda b, c: (0, 0))
    out = pl.pallas_call(
        _rwkv_kernel,
        grid=(bsz // n_seq, nc),
        in_specs=[col(0), col(1), col(2), col(3),
                  full((P_ROWS, D_RWKV)), full((LORA_W, D_RWKV)), full((LORA_W, D_RWKV)),
                  full((LORA_G, D_RWKV)), full((GROUP, GROUP)), full((CHUNK, PREFIX_TERMS * CHUNK))],
        out_specs=pl.BlockSpec((n_seq, CHUNK, D_RWKV), lambda b, c: (b, c, 0)),
        out_shape=jax.ShapeDtypeStruct((bsz, seq, D_RWKV), BF16),
        scratch_shapes=[pltpu.VMEM((n_seq, N_GROUPS, HEAD_DIM, GROUP), F32),
                        pltpu.VMEM((n_seq, 8, 4 * D_RWKV), F32),
                        pltpu.VMEM((n_seq, 7, CHUNK, D_RWKV), BF16),
                        pltpu.VMEM((n_seq, 2, CHUNK, D_RWKV), F32),
                        pltpu.VMEM((n_seq, 8, D_RWKV), F32)],
        compiler_params=_params("arbitrary", "arbitrary"),
        name="rwkv7",
    )(pr3, pr3, pr3, pr3, pvec, w2p, a2p, g2p, e256, tri)
    return out.reshape(bsz * seq, D_RWKV)


def _out_proj_kernel(oa_ref, or_ref, w_ref, x_ref, gate_ref, g_ref, sc_ref, sh_ref, x1_ref, h2_ref):
    acc = jnp.dot(oa_ref[...], w_ref[0:D_ATTN, :], preferred_element_type=F32)
    acc = acc + jnp.dot(or_ref[...], w_ref[D_ATTN:, :], preferred_element_type=F32)
    x1 = x_ref[...] + gate_ref[...] * acc
    x1_ref[...] = x1
    h2_ref[...] = _modulated_norm(x1, g_ref[...], sc_ref[...], sh_ref[...]).astype(BF16)


def _out_proj(o_attn, o_rwkv, w_bf, x2, mod3, norm_g, seq):
    n = x2.shape[0]
    tiles_per_batch = seq // TM_OUT
    mod = lambda col: pl.BlockSpec((None, 1, D_MODEL), lambda i: (i // tiles_per_batch, 0, col))
    return pl.pallas_call(
        _out_proj_kernel,
        grid=(n // TM_OUT,),
        in_specs=[pl.BlockSpec((TM_OUT, D_ATTN), lambda i: (i, 0)),
                  pl.BlockSpec((TM_OUT, D_RWKV), lambda i: (i, 0)),
                  pl.BlockSpec((D_MODEL, D_MODEL), lambda i: (0, 0), pipeline_mode=pl.Buffered(1)),
                  pl.BlockSpec((TM_OUT, D_MODEL), lambda i: (i, 0)),
                  mod(2),
                  pl.BlockSpec((1, D_MODEL), lambda i: (0, 0)),
                  mod(4), mod(3)],
        out_specs=[pl.BlockSpec((TM_OUT, D_MODEL), lambda i: (i, 0)),
                   pl.BlockSpec((TM_OUT, D_MODEL), lambda i: (i, 0))],
        out_shape=[jax.ShapeDtypeStruct((n, D_MODEL), F32),
                   jax.ShapeDtypeStruct((n, D_MODEL), BF16)],
        compiler_params=_params("arbitrary"),
        name="out_proj",
    )(o_attn, o_rwkv, w_bf, x2, mod3, norm_g, mod3, mod3)


def _ffn_up_kernel(h_ref, wg_ref, wv_ref, cwg_ref, cwv_ref, cbg_ref, cbv_ref, o_ref,
                   wgb_ref, wvb_ref, carry_ref, *, tiles_per_batch):
    i = pl.program_id(1)

    @pl.when(i == 0)
    def _():
        wgb_ref[...] = wg_ref[...].astype(BF16)
        wvb_ref[...] = wv_ref[...].astype(BF16)
        carry_ref[...] = jnp.zeros(carry_ref.shape, F32)

    first = (i % tiles_per_batch) == 0

    def conv(up, prev, cw_ref, cb_ref):
        def taps(x, x1, x2):
            return cb_ref[...] + cw_ref[0:1, :] * x2 + cw_ref[1:2, :] * x1 + cw_ref[2:3, :] * x

        y = taps(up, pltpu.roll(up, 1, axis=0), pltpu.roll(up, 2, axis=0))
        ext = jnp.concatenate([prev, up[0:8, :]], axis=0)
        head = taps(ext, pltpu.roll(ext, 1, axis=0), pltpu.roll(ext, 2, axis=0))[8:16, :]
        return jnp.concatenate([head, y[8:, :]], axis=0)

    h = h_ref[...]
    up_g = jnp.dot(h, wgb_ref[...], preferred_element_type=F32)
    up_v = jnp.dot(h, wvb_ref[...], preferred_element_type=F32)
    gate = conv(up_g, jnp.where(first, 0.0, carry_ref[0]), cwg_ref, cbg_ref)
    val = conv(up_v, jnp.where(first, 0.0, carry_ref[1]), cwv_ref, cbv_ref)
    carry_ref[0] = up_g[TM - 8:, :]
    carry_ref[1] = up_v[TM - 8:, :]
    o_ref[...] = (gate * (1.0 / (1.0 + jnp.exp(-gate))) * val).astype(o_ref.dtype)


def _ffn_up(h2, w_up, conv_w8, conv_b, seq):
    n = h2.shape[0]
    tiles_per_batch = seq // TM
    nf = D_FF // TN
    return pl.pallas_call(
        functools.partial(_ffn_up_kernel, tiles_per_batch=tiles_per_batch),
        grid=(nf, n // TM),
        in_specs=[pl.BlockSpec((TM, D_MODEL), lambda j, i: (i, 0)),
                  pl.BlockSpec((D_MODEL, TN), lambda j, i: (0, j)),
                  pl.BlockSpec((D_MODEL, TN), lambda j, i: (0, nf + j)),
                  pl.BlockSpec((8, TN), lambda j, i: (0, j)),
                  pl.BlockSpec((8, TN), lambda j, i: (0, nf + j)),
                  pl.BlockSpec((1, TN), lambda j, i: (0, j)),
                  pl.BlockSpec((1, TN), lambda j, i: (0, nf + j))],
        out_specs=pl.BlockSpec((TM, TN), lambda j, i: (i, j)),
        out_shape=jax.ShapeDtypeStruct((n, D_FF), BF16),
        scratch_shapes=[pltpu.VMEM((D_MODEL, TN), BF16),
                        pltpu.VMEM((D_MODEL, TN), BF16),
                        pltpu.VMEM((2, 8, TN), F32)],
        compiler_params=_params("arbitrary", "arbitrary"),
        name="ffn_up",
    )(h2, w_up, w_up, conv_w8, conv_w8, conv_b, conv_b)


def _ffn_down_kernel(a_ref, w_ref, x_ref, gate_ref, o_ref, wb_ref):
    @pl.when(pl.program_id(1) == 0)
    def _():
        wb_ref[...] = w_ref[...].astype(BF16)

    acc = jnp.dot(a_ref[...], wb_ref[...], preferred_element_type=F32)
    o_ref[...] = x_ref[...] + gate_ref[...] * acc


def _ffn_down(act, w_down, x1, mod3, seq):
    n = x1.shape[0]
    tiles_per_batch = seq // TM_DOWN
    gate_col = 5 * (D_MODEL // TN)
    return pl.pallas_call(
        _ffn_down_kernel,
        grid=(D_MODEL // TN, n // TM_DOWN),
        in_specs=[pl.BlockSpec((TM_DOWN, D_FF), lambda j, i: (i, 0)),
                  pl.BlockSpec((D_FF, TN), lambda j, i: (0, j)),
                  pl.BlockSpec((TM_DOWN, TN), lambda j, i: (i, j)),
                  pl.BlockSpec((None, 1, TN), lambda j, i: (i // tiles_per_batch, 0, gate_col + j))],
        out_specs=pl.BlockSpec((TM_DOWN, TN), lambda j, i: (i, j)),
        out_shape=jax.ShapeDtypeStruct((n, D_MODEL), F32),
        scratch_shapes=[pltpu.VMEM((D_FF, TN), BF16)],
        compiler_params=_params("arbitrary", "arbitrary"),
        name="ffn_down",
    )(act, w_down, x1, mod3)


def _pad_cols(w, width):
    return jnp.pad(w, ((0, 0), (0, width - w.shape[1])))


def _pad_rows(w, height):
    return jnp.pad(w, ((0, height - w.shape[0]), (0, 0)))


def kernel(x, c, rel_bias, w_ada, b_ada, norm_mix_g, w_in, q_norm_g, k_norm_g, lambda_q1, lambda_k1,
           lambda_q2, lambda_k2, attn_subln_g, mu_rkv, mu_wag, w0, w1, w2, a0, a1, a2, g1, g2, k_k,
           k_a, r_k, ln_x_g, ln_x_b, w_out, norm_ffn_g, w_up, conv_w, conv_b, w_down):
    bsz, seq, _ = x.shape
    n = bsz * seq
    x2 = x.reshape(n, D_MODEL)

    mod = _adaln(c, w_ada[0], b_ada[0])
    mod3 = mod.reshape(bsz, 1, 6 * D_MODEL)

    idx = jnp.arange(GROUP) // HEAD_DIM
    e256 = (idx[:, None] == idx[None, :]).astype(BF16)
    tri = jnp.tile((jnp.arange(CHUNK)[:, None] >= jnp.arange(CHUNK)[None, :]).astype(BF16), (1, PREFIX_TERMS))

    gain = jnp.concatenate([jnp.tile(q_norm_g[0], D_ATTN // HEAD_DIM) * (HEAD_DIM ** -0.5 * LOG2E),
                            jnp.tile(k_norm_g[0], D_ATTN // HEAD_DIM)]).reshape(1, 2 * D_ATTN)
    mu = mu_wag[0]
    lora_w = (_pad_cols(w1[0], LORA_W), _pad_cols(a1[0], LORA_W), _pad_cols(g1[0], LORA_G))
    w_lora = jnp.concatenate([(1.0 - mu[i])[:, None] * w for i, w in enumerate(lora_w)]
                             + [mu[i][:, None] * w for i, w in enumerate(lora_w)], axis=1)
    w_all = jnp.concatenate([w_in[0], w_lora], axis=1).astype(BF16)
    w_tiles = w_all.reshape(D_MODEL, W_IN_TILES + LORA_TILES, TN).transpose(1, 0, 2)
    qkv, pr = _proj(x2, norm_mix_g, mod3, w_tiles, gain, e256, seq)

    bias_tiles, lam = _attn_tables(rel_bias, lambda_q1, lambda_k1, lambda_q2, lambda_k2)
    o_attn = _diff_attn(qkv, bias_tiles, lam, attn_subln_g, bsz, seq)

    pvec = jnp.concatenate([mu_rkv[0], w0, a0, k_k, k_a, r_k.reshape(1, D_RWKV), ln_x_g, ln_x_b,
                            jnp.zeros((P_ROWS - 10, D_RWKV), F32)], axis=0)
    o_rwkv = _rwkv(pr, pvec, _pad_rows(w2[0], LORA_W).astype(BF16), _pad_rows(a2[0], LORA_W).astype(BF16),
                   _pad_rows(g2[0], LORA_G).astype(BF16), e256, tri, bsz, seq)

    x1, h2 = _out_proj(o_attn, o_rwkv, w_out[0].astype(BF16), x2, mod3, norm_ffn_g, seq)

    act = _ffn_up(h2, w_up[0], _pad_rows(conv_w[0], 8), conv_b, seq)
    out = _ffn_down(act, w_down[0], x1, mod3, seq)
    return out.reshape(bsz, seq, D_MODEL)
```
